```python
import jax, jax.numpy as jnp
from jax import lax
import numpy as np

D_MODEL = 2048
BATCH = 1
SEQ = 8192
DEPTH = 4

BRANCH_WIDTH = D_MODEL // 2
N_BRANCH = 3
MLA_NOPE = 128
MLA_ROPE = 64
MLA_V = 128
MLA_HEADS = BRANCH_WIDTH // MLA_V
MLA_Q_RANK = 512
MLA_KV_RANK = 512
ROPE_THETA = 10000.0
GLA_HEADS = 4
GLA_DV = BRANCH_WIDTH // GLA_HEADS
GLA_DK = GLA_DV // 2
GLA_GATE_RANK = 16
GLA_TAU = 16.0
GLA_CHUNK = 64
FOX_DH = 128
FOX_HEADS = BRANCH_WIDTH // FOX_DH
FORGET_BIAS_INIT = 3.0
FFN_HIDDEN = ((8 * D_MODEL + 3 * 256 - 1) // (3 * 256)) * 256
Q_BLOCK = 128
EPS = 1e-6
NEG_INF = -1e30

SPLIT_SIZES = (
    MLA_Q_RANK,
    MLA_KV_RANK,
    MLA_ROPE,
    GLA_HEADS * GLA_DK,
    GLA_HEADS * GLA_DK,
    GLA_HEADS * GLA_DV,
    GLA_GATE_RANK,
    GLA_HEADS * GLA_DV,
    FOX_HEADS * FOX_DH,
    FOX_HEADS * FOX_DH,
    FOX_HEADS * FOX_DH,
    FOX_HEADS,
    N_BRANCH * D_MODEL,
)
D_IN = sum(SPLIT_SIZES)
SPLIT_IDX = tuple(int(v) for v in np.cumsum(SPLIT_SIZES)[:-1])

kernel_name = "hybrid_mla_gla_fox_gated_block"


def rms_norm(x, g):
    xf = x.astype(jnp.float32)
    y = xf * lax.rsqrt(jnp.mean(xf * xf, axis=-1, keepdims=True) + EPS)
    return (y * g.astype(jnp.float32)).astype(x.dtype)


def rope(x, pos):
    half = x.shape[-1] // 2
    inv = ROPE_THETA ** (-jnp.arange(half, dtype=jnp.float32) / half)
    ang = pos.astype(jnp.float32)[:, :, None] * inv
    cos = jnp.cos(ang)[:, :, None, :]
    sin = jnp.sin(ang)[:, :, None, :]
    x1 = x[..., :half].astype(jnp.float32)
    x2 = x[..., half:].astype(jnp.float32)
    return jnp.concatenate([x1 * cos - x2 * sin, x2 * cos + x1 * sin], axis=-1).astype(x.dtype)


def block_causal_attention(q, k, v, scale, log_f_cum=None):
    B, S, H, dk = q.shape
    nb = S // Q_BLOCK
    q_blocks = q.reshape(B, nb, Q_BLOCK, H, dk).swapaxes(0, 1)
    key_pos = jnp.arange(S)
    use_forget = log_f_cum is not None
    if use_forget:
        f_keys = log_f_cum.astype(jnp.float32).swapaxes(1, 2)
        f_blocks = log_f_cum.astype(jnp.float32).reshape(B, nb, Q_BLOCK, H).swapaxes(0, 1)
        xs = (jnp.arange(nb), q_blocks, f_blocks)
    else:
        xs = (jnp.arange(nb), q_blocks)

    def one_block(args):
        i, q_blk = args[0], args[1]
        s = jnp.einsum('bqhd,bkhd->bhqk', q_blk, k, preferred_element_type=jnp.float32) * scale
        if use_forget:
            s = s + args[2].swapaxes(1, 2)[..., None] - f_keys[:, :, None, :]
        q_pos = i * Q_BLOCK + jnp.arange(Q_BLOCK)
        s = jnp.where(key_pos[None, :] <= q_pos[:, None], s, NEG_INF)
        p = jax.nn.softmax(s, axis=-1).astype(v.dtype)
        return jnp.einsum('bhqk,bkhd->bqhd', p, v)

    out = lax.map(one_block, xs)
    return out.swapaxes(0, 1).reshape(B, S, H, v.shape[-1])


def gla_chunked(q, k, v, log_a):
    B, S, H, dk = q.shape
    dv = v.shape[-1]
    C = GLA_CHUNK
    n = S // C

    def to_chunks(t):
        return t.astype(jnp.float32).reshape(B, n, C, H, t.shape[-1]).transpose(1, 0, 3, 2, 4)

    qc, kc, vc, ac = to_chunks(q), to_chunks(k), to_chunks(v), to_chunks(log_a)
    causal = jnp.tril(jnp.ones((C, C), dtype=bool))[:, :, None]

    def step(state, inp):
        qi, ki, vi, ai = inp
        b = jnp.cumsum(ai, axis=-2)
        diff = b[:, :, :, None, :] - b[:, :, None, :, :]
        decay = jnp.exp(jnp.where(causal, diff, -jnp.inf))
        attn = jnp.einsum('bhid,bhjd,bhijd->bhij', qi, ki, decay)
        o = (jnp.einsum('bhij,bhjv->bhiv', attn, vi)
             + jnp.einsum('bhid,bhdv->bhiv', qi * jnp.exp(b), state))
        b_last = b[:, :, -1:, :]
        new_state = (jnp.exp(b_last[:, :, 0, :])[..., None] * state
                     + jnp.einsum('bhjd,bhjv->bhdv', ki * jnp.exp(b_last - b), vi))
        return new_state, o

    init = jnp.zeros((B, H, dk, dv), jnp.float32)
    _, o = lax.scan(step, init, (qc, kc, vc, ac))
    return o.transpose(1, 0, 3, 2, 4).reshape(B, S, H, dv)


def hybrid_layer(x, pos, g_mix, w_in, g_cq, w_uq, g_ckv, w_ukv, g_mla_q, g_mla_k,
                 w_a2, b_a, g_gla_o, g_fox_q, g_fox_k, b_f, w_branch, w_out,
                 g_ffn, w_gu, w_down):
    B, S, _ = x.shape
    h = rms_norm(x, g_mix)
    proj = h @ w_in
    (cq, ckv, kr, gq, gk, gv, ga, gr, fq, fk, fv, fl, gates) = jnp.split(proj, SPLIT_IDX, axis=-1)

    q_a = (rms_norm(cq, g_cq) @ w_uq).reshape(B, S, MLA_HEADS, MLA_NOPE + MLA_ROPE)
    kv_a = (rms_norm(ckv, g_ckv) @ w_ukv).reshape(B, S, MLA_HEADS, MLA_NOPE + MLA_V)
    k_nope, v_a = kv_a[..., :MLA_NOPE], kv_a[..., MLA_NOPE:]
    k_a = jnp.concatenate([k_nope, jnp.broadcast_to(kr[:, :, None, :], (B, S, MLA_HEADS, MLA_ROPE))], axis=-1)
    q_a = rms_norm(q_a, g_mla_q)
    k_a = rms_norm(k_a, g_mla_k)
    q_a = jnp.concatenate([q_a[..., :MLA_NOPE], rope(q_a[..., MLA_NOPE:], pos)], axis=-1)
    k_a = jnp.concatenate([k_a[..., :MLA_NOPE], rope(k_a[..., MLA_NOPE:], pos)], axis=-1)
    o_a = block_causal_attention(q_a, k_a, v_a, (MLA_NOPE + MLA_ROPE) ** -0.5)

    q_b = gq.reshape(B, S, GLA_HEADS, GLA_DK) * (GLA_DK ** -0.5)
    k_b = gk.reshape(B, S, GLA_HEADS, GLA_DK)
    v_b = gv.reshape(B, S, GLA_HEADS, GLA_DV)
    log_a = (jax.nn.log_sigmoid((ga @ w_a2 + b_a).astype(jnp.float32)) / GLA_TAU).reshape(B, S, GLA_HEADS, GLA_DK)
    o_b = gla_chunked(q_b, k_b, v_b, log_a).astype(x.dtype)
    o_b = rms_norm(o_b, g_gla_o) * jax.nn.silu(gr.reshape(B, S, GLA_HEADS, GLA_DV))

    q_c = rms_norm(fq.reshape(B, S, FOX_HEADS, FOX_DH), g_fox_q)
    k_c = rms_norm(fk.reshape(B, S, FOX_HEADS, FOX_DH), g_fox_k)
    v_c = fv.reshape(B, S, FOX_HEADS, FOX_DH)
    log_f_cum = jnp.cumsum(jax.nn.log_sigmoid((fl + b_f).astype(jnp.float32)), axis=1)
    o_c = block_causal_attention(q_c, k_c, v_c, FOX_DH ** -0.5, log_f_cum)

    branches = jnp.stack([o_a.reshape(B, S, BRANCH_WIDTH), o_b.reshape(B, S, BRANCH_WIDTH),
                          o_c.reshape(B, S, BRANCH_WIDTH)], axis=2)
    y = jnp.einsum('bsnc,ncd->bsnd', branches, w_branch)
    g = jax.nn.sigmoid(gates.reshape(B, S, N_BRANCH, D_MODEL))
    merged = jnp.einsum('bsnd,bsnd->bsd', y, g)
    x = x + merged @ w_out

    h2 = rms_norm(x, g_ffn)
    gate, up = jnp.split(h2 @ w_gu, 2, axis=-1)
    return x + (jax.nn.silu(gate) * up) @ w_down


def setup_inputs(seed: int = 0) -> dict:
    key = jax.random.key(seed)
    ks = jax.random.split(key, 24)
    f32 = jnp.float32

    def w(k, shape, fan_in):
        return jax.random.normal(k, shape, f32) * (fan_in ** -0.5)

    def gain(k, shape):
        return 1.0 + 0.02 * jax.random.normal(k, shape, f32)

    L = DEPTH
    x = jax.random.normal(ks[0], (BATCH, SEQ, D_MODEL), f32)
    offset = jax.random.randint(ks[1], (BATCH, 1), 0, SEQ, dtype=jnp.int32)
    positions = (jnp.arange(SEQ, dtype=jnp.int32)[None, :] + offset).astype(jnp.int32)
    return {
        "x": x,
        "positions": positions,
        "g_mix": gain(ks[2], (L, D_MODEL)),
        "w_in": w(ks[3], (L, D_MODEL, D_IN), D_MODEL),
        "g_cq": gain(ks[4], (L, MLA_Q_RANK)),
        "w_uq": w(ks[5], (L, MLA_Q_RANK, MLA_HEADS * (MLA_NOPE + MLA_ROPE)), MLA_Q_RANK),
        "g_ckv": gain(ks[6], (L, MLA_KV_RANK)),
        "w_ukv": w(ks[7], (L, MLA_KV_RANK, MLA_HEADS * (MLA_NOPE + MLA_V)), MLA_KV_RANK),
        "g_mla_q": gain(ks[8], (L, MLA_NOPE + MLA_ROPE)),
        "g_mla_k": gain(ks[9], (L, MLA_NOPE + MLA_ROPE)),
        "w_a2": w(ks[10], (L, GLA_GATE_RANK, GLA_HEADS * GLA_DK), GLA_GATE_RANK),
        "b_a": 0.1 * jax.random.normal(ks[11], (L, GLA_HEADS * GLA_DK), f32),
        "g_gla_o": gain(ks[12], (L, GLA_DV)),
        "g_fox_q": gain(ks[13], (L, FOX_DH)),
        "g_fox_k": gain(ks[14], (L, FOX_DH)),
        "b_f": FORGET_BIAS_INIT + 0.1 * jax.random.normal(ks[15], (L, FOX_HEADS), f32),
        "w_branch": w(ks[16], (L, N_BRANCH, BRANCH_WIDTH, D_MODEL), BRANCH_WIDTH),
        "w_out": w(ks[17], (L, D_MODEL, D_MODEL), D_MODEL),
        "g_ffn": gain(ks[18], (L, D_MODEL)),
        "w_gu": w(ks[19], (L, D_MODEL, 2 * FFN_HIDDEN), D_MODEL),
        "w_down": w(ks[20], (L, FFN_HIDDEN, D_MODEL), FFN_HIDDEN),
    }


def reference(x, positions, g_mix, w_in, g_cq, w_uq, g_ckv, w_ukv, g_mla_q, g_mla_k,
              w_a2, b_a, g_gla_o, g_fox_q, g_fox_k, b_f, w_branch, w_out,
              g_ffn, w_gu, w_down):
    h = x
    for l in range(DEPTH):
        h = hybrid_layer(h, positions, g_mix[l], w_in[l], g_cq[l], w_uq[l], g_ckv[l], w_ukv[l],
                         g_mla_q[l], g_mla_k[l], w_a2[l], b_a[l], g_gla_o[l], g_fox_q[l],
                         g_fox_k[l], b_f[l], w_branch[l], w_out[l], g_ffn[l], w_gu[l], w_down[l])
    return h
```

```python
import functools

import jax
import jax.numpy as jnp
from jax import lax
from jax.experimental import pallas as pl
from jax.experimental.pallas import tpu as pltpu

F32 = jnp.float32
BF16 = jnp.bfloat16

D_MODEL = 2048
BRANCH_WIDTH = D_MODEL // 2
N_BRANCH = 3
MLA_NOPE = 128
MLA_ROPE = 64
MLA_V = 128
MLA_HEADS = BRANCH_WIDTH // MLA_V
MLA_QK = MLA_NOPE + MLA_ROPE
MLA_RANK = 512
ROPE_THETA = 10000.0
GLA_HEADS = 4
GLA_DV = BRANCH_WIDTH // GLA_HEADS
GLA_DK = GLA_DV // 2
GLA_GATE_RANK = 16
GLA_TAU = 16.0
FOX_DH = 128
FOX_HEADS = BRANCH_WIDTH // FOX_DH
FFN_HIDDEN = ((8 * D_MODEL + 3 * 256 - 1) // (3 * 256)) * 256
EPS = 1e-6
NEG_INF = -1e30

LANES = 128
HEAD_PAD = 256
VMEM_LIMIT = 48 * 1024 * 1024

A_CQ, A_CKV, A_KR, A_MISC = 0, 512, 1024, 1152
A_WIDTH = 1280
MISC_FL = 0
MISC_GA = FOX_HEADS
B_GQ, B_GK, B_GV, B_GR, B_FQ, B_FK, B_FV = 0, 512, 1024, 2048, 3072, 4096, 5120
B_WIDTH = 6144

GLA_CHUNK = 64
GLA_SUB = 16
GLA_BLOCK = 512


def _cparams(sem):
    return pltpu.CompilerParams(dimension_semantics=sem, vmem_limit_bytes=VMEM_LIMIT)


def _split3(x):
    x1 = x.astype(BF16)
    r1 = x - x1.astype(F32)
    x2 = r1.astype(BF16)
    x3 = (r1 - x2.astype(F32)).astype(BF16)
    return x1, x2, x3


def _log_sigmoid(z):
    return -(jnp.maximum(-z, 0.0) + jnp.log1p(jnp.exp(-jnp.abs(z))))


def _rmsnorm_kernel(x_ref, g_ref, o_ref):
    x = x_ref[...]
    ms = jnp.mean(x * x, axis=-1, keepdims=True)
    o_ref[...] = (x * lax.rsqrt(ms + EPS) * g_ref[...]).astype(o_ref.dtype)


def _rmsnorm(x, g, tm=512):
    m, d = x.shape
    return pl.pallas_call(
        _rmsnorm_kernel,
        grid=(m // tm,),
        in_specs=[pl.BlockSpec((tm, d), lambda i: (i, 0)),
                  pl.BlockSpec((1, d), lambda i: (0, 0))],
        out_specs=pl.BlockSpec((tm, d), lambda i: (i, 0)),
        out_shape=jax.ShapeDtypeStruct((m, d), BF16),
        compiler_params=_cparams(("parallel",)),
        name="rmsnorm",
    )(x, g.reshape(1, d))


def _mm_kernel(a_ref, w_ref, o_ref, *, act):
    y = jnp.dot(a_ref[...], w_ref[...], preferred_element_type=F32)
    if act == "sigmoid":
        y = jax.nn.sigmoid(y)
    o_ref[...] = y.astype(o_ref.dtype)


def _matmul(a, w, out_dtype, tm, tn, act=None, name="matmul"):
    m, k = a.shape
    n = w.shape[1]
    return pl.pallas_call(
        functools.partial(_mm_kernel, act=act),
        grid=(m // tm, n // tn),
        in_specs=[pl.BlockSpec((tm, k), lambda i, j: (i, 0)),
                  pl.BlockSpec((k, tn), lambda i, j: (0, j))],
        out_specs=pl.BlockSpec((tm, tn), lambda i, j: (i, j)),
        out_shape=jax.ShapeDtypeStruct((m, n), out_dtype),
        compiler_params=_cparams(("parallel", "arbitrary")),
        name=name,
    )(a, w)


def _mm_res_kernel(a_ref, w_ref, r_ref, o_ref):
    o_ref[...] = r_ref[...] + jnp.dot(a_ref[...], w_ref[...], preferred_element_type=F32)


def _matmul_residual(a, w, res, tm, tn, name):
    m, k = a.shape
    n = w.shape[1]
    return pl.pallas_call(
        _mm_res_kernel,
        grid=(m // tm, n // tn),
        in_specs=[pl.BlockSpec((tm, k), lambda i, j: (i, 0)),
                  pl.BlockSpec((k, tn), lambda i, j: (0, j)),
                  pl.BlockSpec((tm, tn), lambda i, j: (i, j))],
        out_specs=pl.BlockSpec((tm, tn), lambda i, j: (i, j)),
        out_shape=jax.ShapeDtypeStruct((m, n), F32),
        compiler_params=_cparams(("parallel", "arbitrary")),
        name=name,
    )(a, w, res)


def _swiglu_kernel(a_ref, wg_ref, wu_ref, o_ref):
    a = a_ref[...]
    g = jnp.dot(a, wg_ref[...], preferred_element_type=F32)
    u = jnp.dot(a, wu_ref[...], preferred_element_type=F32)
    o_ref[...] = (g * jax.nn.sigmoid(g) * u).astype(o_ref.dtype)


def _swiglu_up(a, w_gu, tm, tn):
    m, k = a.shape
    hidden = w_gu.shape[1] // 2
    nj = hidden // tn
    return pl.pallas_call(
        _swiglu_kernel,
        grid=(m // tm, nj),
        in_specs=[pl.BlockSpec((tm, k), lambda i, j: (i, 0)),
                  pl.BlockSpec((k, tn), lambda i, j: (0, j)),
                  pl.BlockSpec((k, tn), lambda i, j: (0, j + nj))],
        out_specs=pl.BlockSpec((tm, tn), lambda i, j: (i, j)),
        out_shape=jax.ShapeDtypeStruct((m, hidden), BF16),
        compiler_params=_cparams(("parallel", "arbitrary")),
        name="ffn_up",
    )(a, w_gu, w_gu)


def _merge_kernel(oa_ref, ob_ref, oc_ref, wa_ref, wb_ref, wc_ref, ga_ref, gb_ref, gc_ref, o_ref):
    acc = jnp.dot(oa_ref[...], wa_ref[...], preferred_element_type=F32) * ga_ref[...].astype(F32)
    acc += jnp.dot(ob_ref[...], wb_ref[...], preferred_element_type=F32) * gb_ref[...].astype(F32)
    acc += jnp.dot(oc_ref[...], wc_ref[...], preferred_element_type=F32) * gc_ref[...].astype(F32)
    o_ref[...] = acc.astype(o_ref.dtype)


def _merge(o_a, o_b, o_c, w_branch2d, gsig, tm, tn):
    m, bw = o_a.shape
    d = w_branch2d.shape[1]
    nj = d // tn
    o_spec = pl.BlockSpec((tm, bw), lambda i, j: (i, 0))

    def w_spec(n):
        return pl.BlockSpec((bw, tn), lambda i, j: (n, j))

    def g_spec(n):
        return pl.BlockSpec((tm, tn), lambda i, j: (i, n * nj + j))

    return pl.pallas_call(
        _merge_kernel,
        grid=(m // tm, nj),
        in_specs=[o_spec, o_spec, o_spec, w_spec(0), w_spec(1), w_spec(2), g_spec(0), g_spec(1), g_spec(2)],
        out_specs=pl.BlockSpec((tm, tn), lambda i, j: (i, j)),
        out_shape=jax.ShapeDtypeStruct((m, d), BF16),
        compiler_params=_cparams(("parallel", "arbitrary")),
        name="branch_merge",
    )(o_a, o_b, o_c, w_branch2d, w_branch2d, w_branch2d, gsig, gsig, gsig)


def _rope_table_kernel(pos_ref, inv_ref, c_ref, s1_ref, s2_ref):
    ang = pos_ref[...].astype(F32) * inv_ref[...]
    c = jnp.cos(ang)
    s = jnp.sin(ang)
    lane = lax.broadcasted_iota(jnp.int32, ang.shape, 1)
    half = MLA_ROPE // 2
    c_ref[...] = jnp.where(lane < MLA_ROPE, c, 0.0)
    s1_ref[...] = jnp.where((lane >= half) & (lane < MLA_ROPE), s, 0.0)
    s2_ref[...] = jnp.where(lane < half, -s, 0.0)


def _rope_tables(pos_col, inv128, tm=512):
    s = pos_col.shape[0]
    spec = pl.BlockSpec((tm, LANES), lambda i: (i, 0))
    shp = jax.ShapeDtypeStruct((s, LANES), F32)
    return pl.pallas_call(
        _rope_table_kernel,
        grid=(s // tm,),
        in_specs=[pl.BlockSpec((tm, 1), lambda i: (i, 0)),
                  pl.BlockSpec((1, LANES), lambda i: (0, 0))],
        out_specs=[spec, spec, spec],
        out_shape=[shp, shp, shp],
        compiler_params=_cparams(("parallel",)),
        name="rope_tables",
    )(pos_col, inv128)


def _rope128(x, c, s1, s2):
    half = MLA_ROPE // 2
    return x * c + pltpu.roll(x, half, 1) * s1 + pltpu.roll(x, LANES - half, 1) * s2


def _mla_prep_kernel(cq_ref, ckv_ref, kr_ref, wuq_ref, wukv_ref, gcq_ref, gckv_ref, gq_ref, gk_ref, gkr_ref,
                     c_ref, s1_ref, s2_ref, q_out, k_out, v_out):
    def norm(x, g):
        return x * lax.rsqrt(jnp.mean(x * x, axis=-1, keepdims=True) + EPS) * g

    cqn = norm(cq_ref[...], gcq_ref[...]).astype(BF16)
    ckvn = norm(ckv_ref[...], gckv_ref[...]).astype(BF16)
    qa = jnp.dot(cqn, wuq_ref[...], preferred_element_type=F32)
    kva = jnp.dot(ckvn, wukv_ref[...], preferred_element_type=F32)
    krp = kr_ref[...]
    kr_ss = jnp.sum(krp * krp, axis=-1, keepdims=True)
    c, s1, s2 = c_ref[...], s1_ref[...], s2_ref[...]
    gq, gk, gkr = gq_ref[...], gk_ref[...], gkr_ref[...]
    scale = MLA_QK ** -0.5
    for h in range(MLA_HEADS):
        qh = qa[:, h * HEAD_PAD:(h + 1) * HEAD_PAD]
        rq = lax.rsqrt(jnp.sum(qh * qh, axis=-1, keepdims=True) * (1.0 / MLA_QK) + EPS)
        qn = qh * rq * gq
        q_out[h, :, 0:LANES] = (qn[:, 0:LANES] * scale).astype(BF16)
        q_out[h, :, LANES:HEAD_PAD] = (_rope128(qn[:, LANES:HEAD_PAD], c, s1, s2) * scale).astype(BF16)
        kn = kva[:, h * HEAD_PAD:h * HEAD_PAD + MLA_NOPE]
        rk = lax.rsqrt((jnp.sum(kn * kn, axis=-1, keepdims=True) + kr_ss) * (1.0 / MLA_QK) + EPS)
        k_out[h, :, 0:LANES] = (kn * rk * gk).astype(BF16)
        k_out[h, :, LANES:HEAD_PAD] = _rope128(krp * rk * gkr, c, s1, s2).astype(BF16)
        v_out[h] = kva[:, h * HEAD_PAD + MLA_NOPE:(h + 1) * HEAD_PAD].astype(BF16)


def _mla_prep(pa, wuq_p, wukv, g_cq, g_ckv, g_q, g_k, g_kr, tabs, tm=256):
    s = pa.shape[0]
    h = MLA_HEADS
    row = lambda w: pl.BlockSpec((1, w), lambda i: (0, 0))
    tab = pl.BlockSpec((tm, LANES), lambda i: (i, 0))
    return pl.pallas_call(
        _mla_prep_kernel,
        grid=(s // tm,),
        in_specs=[pl.BlockSpec((tm, MLA_RANK), lambda i: (i, A_CQ // MLA_RANK)),
                  pl.BlockSpec((tm, MLA_RANK), lambda i: (i, A_CKV // MLA_RANK)),
                  pl.BlockSpec((tm, LANES), lambda i: (i, A_KR // LANES)),
                  pl.BlockSpec(wuq_p.shape, lambda i: (0, 0)),
                  pl.BlockSpec(wukv.shape, lambda i: (0, 0)),
                  row(MLA_RANK), row(MLA_RANK), row(HEAD_PAD), row(LANES), row(LANES),
                  tab, tab, tab],
        out_specs=[pl.BlockSpec((h, tm, HEAD_PAD), lambda i: (0, i, 0)),
                   pl.BlockSpec((h, tm, HEAD_PAD), lambda i: (0, i, 0)),
                   pl.BlockSpec((h, tm, MLA_V), lambda i: (0, i, 0))],
        out_shape=[jax.ShapeDtypeStruct((h, s, HEAD_PAD), BF16),
                   jax.ShapeDtypeStruct((h, s, HEAD_PAD), BF16),
                   jax.ShapeDtypeStruct((h, s, MLA_V), BF16)],
        compiler_params=_cparams(("parallel",)),
        name="mla_prep",
    )(pa, pa, pa, wuq_p, wukv, g_cq, g_ckv, g_q, g_k, g_kr, *tabs)


def _fox_cumsum_kernel(misc_ref, bf_ref, f_ref, *, blk):
    n = misc_ref.shape[0] // blk
    r = lax.broadcasted_iota(jnp.int32, (blk, blk), 0)
    c = lax.broadcasted_iota(jnp.int32, (blk, blk), 1)
    tri = jnp.where(c <= r, 1.0, 0.0).astype(BF16)
    bias = bf_ref[...]

    def body(i, carry):
        off = pl.multiple_of(i * blk, blk)
        ls = _log_sigmoid(misc_ref[pl.ds(off, blk), :] + bias)
        x1, x2, x3 = _split3(ls)
        cs = (jnp.dot(tri, x1, preferred_element_type=F32)
              + jnp.dot(tri, x2, preferred_element_type=F32)
              + jnp.dot(tri, x3, preferred_element_type=F32)) + carry
        f_ref[pl.ds(off, blk), :] = cs
        return cs[blk - 1:blk, :]

    lax.fori_loop(0, n, body, jnp.zeros((1, LANES), F32))


def _fox_cumsum(pa, bf128, blk=256):
    s = pa.shape[0]
    return pl.pallas_call(
        functools.partial(_fox_cumsum_kernel, blk=blk),
        grid=(1,),
        in_specs=[pl.BlockSpec((s, LANES), lambda i: (0, A_MISC // LANES)),
                  pl.BlockSpec((1, LANES), lambda i: (0, 0))],
        out_specs=pl.BlockSpec((s, LANES), lambda i: (0, 0)),
        out_shape=jax.ShapeDtypeStruct((s, LANES), F32),
        compiler_params=_cparams(("arbitrary",)),
        name="fox_cumsum",
    )(pa, bf128)


def _fox_prep_kernel(fq_ref, fk_ref, fv_ref, f_ref, gq_ref, gk_ref, q_out, k_out, v_out):
    tm = fq_ref.shape[0]
    lane = lax.broadcasted_iota(jnp.int32, (tm, LANES), 1)
    ones_tail = jnp.where(lane < 3, 1.0, 0.0).astype(BF16)
    f_all = f_ref[...]
    scale = FOX_DH ** -0.5
    gq, gk = gq_ref[...], gk_ref[...]
    for h in range(FOX_HEADS):
        sl = slice(h * FOX_DH, (h + 1) * FOX_DH)
        q = fq_ref[:, sl].astype(F32)
        k = fk_ref[:, sl].astype(F32)
        qn = q * lax.rsqrt(jnp.mean(q * q, axis=-1, keepdims=True) + EPS) * gq
        kn = k * lax.rsqrt(jnp.mean(k * k, axis=-1, keepdims=True) + EPS) * gk
        q_out[h, :, 0:LANES] = (qn * scale).astype(BF16)
        q_out[h, :, LANES:HEAD_PAD] = ones_tail
        k_out[h, :, 0:LANES] = kn.astype(BF16)
        nf = -f_all[:, h:h + 1]
        n1, n2, n3 = _split3(nf)
        tail = jnp.where(lane == 0, n1.astype(F32),
                         jnp.where(lane == 1, n2.astype(F32),
                                   jnp.where(lane == 2, n3.astype(F32), 0.0)))
        k_out[h, :, LANES:HEAD_PAD] = tail.astype(BF16)
        v_out[h] = fv_ref[:, sl]


def _fox_prep(pb, f_cum, g_q, g_k, tm=256):
    s = pb.shape[0]
    h = FOX_HEADS
    w = FOX_HEADS * FOX_DH
    row = pl.BlockSpec((1, FOX_DH), lambda i: (0, 0))
    return pl.pallas_call(
        _fox_prep_kernel,
        grid=(s // tm,),
        in_specs=[pl.BlockSpec((tm, w), lambda i: (i, B_FQ // w)),
                  pl.BlockSpec((tm, w), lambda i: (i, B_FK // w)),
                  pl.BlockSpec((tm, w), lambda i: (i, B_FV // w)),
                  pl.BlockSpec((tm, LANES), lambda i: (i, 0)),
                  row, row],
        out_specs=[pl.BlockSpec((h, tm, HEAD_PAD), lambda i: (0, i, 0)),
                   pl.BlockSpec((h, tm, HEAD_PAD), lambda i: (0, i, 0)),
                   pl.BlockSpec((h, tm, FOX_DH), lambda i: (0, i, 0))],
        out_shape=[jax.ShapeDtypeStruct((h, s, HEAD_PAD), BF16),
                   jax.ShapeDtypeStruct((h, s, HEAD_PAD), BF16),
                   jax.ShapeDtypeStruct((h, s, FOX_DH), BF16)],
        compiler_params=_cparams(("parallel",)),
        name="fox_prep",
    )(pb, pb, pb, f_cum, g_q, g_k)


def _attn_kernel(*refs, tq, use_f):
    if use_f:
        q_ref, k_ref, v_ref, f_ref, o_ref, m_sc, l_sc, acc_sc = refs
    else:
        q_ref, k_ref, v_ref, o_ref, m_sc, l_sc, acc_sc = refs
    h = pl.program_id(0)
    qi = pl.program_id(1)
    q = q_ref[0]
    if use_f:
        lane = lax.broadcasted_iota(jnp.int32, (tq, LANES), 1)
        ft = jnp.sum(jnp.where(lane == h, f_ref[...], 0.0), axis=-1, keepdims=True)
    m_sc[...] = jnp.full(m_sc.shape, NEG_INF, F32)
    l_sc[...] = jnp.zeros(l_sc.shape, F32)
    acc_sc[...] = jnp.zeros(acc_sc.shape, F32)

    def update(start, mask):
        k = k_ref[0, pl.ds(start, tq), :]
        v = v_ref[0, pl.ds(start, tq), :]
        s = lax.dot_general(q, k, (((1,), (1,)), ((), ())), preferred_element_type=F32)
        if mask is not None:
            s = jnp.where(mask, s, NEG_INF)
        m_old = m_sc[...]
        m_cur = jnp.max(s, axis=-1, keepdims=True)
        if use_f:
            m_cur = m_cur + ft
        m_new = jnp.maximum(m_old, m_cur)
        shift = m_new - ft if use_f else m_new
        p = jnp.exp(s - shift)
        alpha = jnp.exp(m_old - m_new)
        l_sc[...] = alpha * l_sc[...] + jnp.sum(p, axis=-1, keepdims=True)
        acc_sc[...] = alpha * acc_sc[...] + jnp.dot(p.astype(BF16), v, preferred_element_type=F32)
        m_sc[...] = m_new

    def body(kb, carry):
        update(pl.multiple_of(kb * tq, tq), None)
        return carry

    lax.fori_loop(0, qi, body, 0)
    row = lax.broadcasted_iota(jnp.int32, (tq, tq), 0)
    col = lax.broadcasted_iota(jnp.int32, (tq, tq), 1)
    update(pl.multiple_of(qi * tq, tq), col <= row)
    o_ref[...] = (acc_sc[...] / l_sc[...]).astype(o_ref.dtype)


def _attention(q, k, v, f_cum=None, tq=512):
    hh, s, dk = q.shape
    dv = v.shape[-1]
    use_f = f_cum is not None
    in_specs = [pl.BlockSpec((1, tq, dk), lambda h, i: (h, i, 0)),
                pl.BlockSpec((1, s, dk), lambda h, i: (h, 0, 0)),
                pl.BlockSpec((1, s, dv), lambda h, i: (h, 0, 0))]
    args = [q, k, v]
    if use_f:
        in_specs.append(pl.BlockSpec((tq, LANES), lambda h, i: (i, 0)))
        args.append(f_cum)
    return pl.pallas_call(
        functools.partial(_attn_kernel, tq=tq, use_f=use_f),
        grid=(hh, s // tq),
        in_specs=in_specs,
        out_specs=pl.BlockSpec((tq, dv), lambda h, i: (i, h)),
        out_shape=jax.ShapeDtypeStruct((s, hh * dv), BF16),
        scratch_shapes=[pltpu.VMEM((tq, 1), F32), pltpu.VMEM((tq, 1), F32), pltpu.VMEM((tq, dv), F32)],
        compiler_params=_cparams(("parallel", "arbitrary")),
        name="fox_attention" if use_f else "mla_attention",
    )(*args)


def _gla_kernel(q_ref, k_ref, v_ref, r_ref, misc_ref, wa_ref, ba_ref, go_ref, o_ref, st_ref, la_ref):
    tb = q_ref.shape[0]
    nchunk = tb // GLA_CHUNK
    nsub = GLA_CHUNK // GLA_SUB
    C = GLA_CHUNK

    @pl.when(pl.program_id(1) == 0)
    def _():
        st_ref[...] = jnp.zeros(st_ref.shape, F32)

    m1, m2, _ = _split3(misc_ref[...])
    w1, w2, _ = _split3(wa_ref[...])
    z = (jnp.dot(m1, w1, preferred_element_type=F32) + jnp.dot(m1, w2, preferred_element_type=F32)
         + jnp.dot(m2, w1, preferred_element_type=F32)) + ba_ref[...]
    la_ref[...] = _log_sigmoid(z) * (1.0 / GLA_TAU)

    r_i = lax.broadcasted_iota(jnp.int32, (C, C), 0)
    c_i = lax.broadcasted_iota(jnp.int32, (C, C), 1)
    tri = jnp.where(c_i <= r_i, 1.0, 0.0).astype(BF16)
    sub_row = lax.broadcasted_iota(jnp.int32, (GLA_SUB, GLA_DK), 0)
    col_sub = lax.broadcasted_iota(jnp.int32, (GLA_SUB, C), 1)
    scale = GLA_DK ** -0.5
    g_o = go_ref[...]

    def chunk(ci, carry):
        rows = pl.ds(pl.multiple_of(ci * C, C), C)
        l1, l2, l3 = _split3(la_ref[rows, :])
        b = (jnp.dot(tri, l1, preferred_element_type=F32) + jnp.dot(tri, l2, preferred_element_type=F32)
             + jnp.dot(tri, l3, preferred_element_type=F32))
        qc = q_ref[rows, :].astype(F32) * scale
        kc = k_ref[rows, :].astype(F32)
        vc = v_ref[rows, :]
        vcf = vc.astype(F32)
        st = st_ref[...]

        o = lax.dot_general((qc * jnp.exp(b)).astype(BF16), st.astype(BF16),
                            (((1,), (1,)), ((), ())), preferred_element_type=F32)

        a_rows = [jnp.zeros((GLA_SUB, C), F32)]
        for si in range(1, nsub):
            lo = si * GLA_SUB
            ref_row = b[lo - 1:lo, :]
            q_i = qc[lo:lo + GLA_SUB] * jnp.exp(b[lo:lo + GLA_SUB] - ref_row)
            k_j = kc * jnp.exp(jnp.minimum(ref_row - b, 0.0))
            a = lax.dot_general(q_i.astype(BF16), k_j.astype(BF16), (((1,), (1,)), ((), ())),
                                preferred_element_type=F32)
            a_rows.append(jnp.where(col_sub < lo, a, 0.0))
        a_off = jnp.concatenate(a_rows, axis=0)
        o = o + jnp.dot(a_off.astype(BF16), vc, preferred_element_type=F32)

        o_diag = []
        for si in range(nsub):
            lo = si * GLA_SUB
            qs, ks, bs, vs = qc[lo:lo + GLA_SUB], kc[lo:lo + GLA_SUB], b[lo:lo + GLA_SUB], vcf[lo:lo + GLA_SUB]
            od = jnp.zeros((GLA_SUB, GLA_DV), F32)
            for j in range(GLA_SUB):
                e = jnp.exp(jnp.where(sub_row >= j, bs - bs[j:j + 1, :], NEG_INF))
                w = jnp.sum(qs * (ks[j:j + 1, :] * e), axis=-1, keepdims=True)
                od = od + w * vs[j:j + 1, :]
            o_diag.append(od)
        o = o + jnp.concatenate(o_diag, axis=0)

        b_last = b[C - 1:C, :]
        ke = (kc * jnp.exp(b_last - b)).astype(BF16)
        st_ref[...] = st * jnp.exp(b_last) + lax.dot_general(
            vc, ke, (((0,), (0,)), ((), ())), preferred_element_type=F32)

        on = o * lax.rsqrt(jnp.mean(o * o, axis=-1, keepdims=True) + EPS) * g_o
        r = r_ref[rows, :].astype(F32)
        o_ref[rows, :] = (on * (r * jax.nn.sigmoid(r))).astype(o_ref.dtype)
        return carry

    lax.fori_loop(0, nchunk, chunk, 0)


def _gla(pb, pa, wa_p, b_a, g_o, tb=GLA_BLOCK):
    s = pb.shape[0]
    return pl.pallas_call(
        _gla_kernel,
        grid=(GLA_HEADS, s // tb),
        in_specs=[pl.BlockSpec((tb, GLA_DK), lambda h, i: (i, B_GQ // GLA_DK + h)),
                  pl.BlockSpec((tb, GLA_DK), lambda h, i: (i, B_GK // GLA_DK + h)),
                  pl.BlockSpec((tb, GLA_DV), lambda h, i: (i, B_GV // GLA_DV + h)),
                  pl.BlockSpec((tb, GLA_DV), lambda h, i: (i, B_GR // GLA_DV + h)),
                  pl.BlockSpec((tb, LANES), lambda h, i: (i, A_MISC // LANES)),
                  pl.BlockSpec((LANES, GLA_DK), lambda h, i: (0, h)),
                  pl.BlockSpec((1, GLA_DK), lambda h, i: (0, h)),
                  pl.BlockSpec((1, GLA_DV), lambda h, i: (0, 0))],
        out_specs=pl.BlockSpec((tb, GLA_DV), lambda h, i: (i, h)),
        out_shape=jax.ShapeDtypeStruct((s, GLA_HEADS * GLA_DV), BF16),
        scratch_shapes=[pltpu.VMEM((GLA_DV, GLA_DK), F32), pltpu.VMEM((tb, GLA_DK), F32)],
        compiler_params=_cparams(("parallel", "arbitrary")),
        name="gla",
    )(pb, pb, pb, pb, pa, wa_p, b_a, g_o)


def _group_w_in(w):
    d = w.shape[0]
    off = [0]
    for sz in (512, 512, 64, 512, 512, 1024, 16, 1024, 1024, 1024, 1024, 8, N_BRANCH * D_MODEL):
        off.append(off[-1] + sz)
    cq, ckv, kr, gq, gk, gv, ga, gr, fq, fk, fv, fl, gates = [w[:, off[i]:off[i + 1]] for i in range(13)]
    z = lambda n: jnp.zeros((d, n), w.dtype)
    wa = jnp.concatenate([cq, ckv, kr, z(LANES - MLA_ROPE), fl, ga, z(LANES - FOX_HEADS - GLA_GATE_RANK)], axis=1)
    wb = jnp.concatenate([gq, gk, gv, gr, fq, fk, fv], axis=1)
    return wa.astype(BF16), wb.astype(BF16), gates.astype(BF16)


def _pad_lanes(v, n):
    return jnp.pad(v, (0, n - v.shape[0])).reshape(1, n)


def _layer(x, tabs, g_mix, w_in, g_cq, w_uq, g_ckv, w_ukv, g_mla_q, g_mla_k, w_a2, b_a, g_gla_o,
           g_fox_q, g_fox_k, b_f, w_branch, w_out, g_ffn, w_gu, w_down):
    wa, wb, wg = _group_w_in(w_in)
    h = _rmsnorm(x, g_mix)
    pa = _matmul(h, wa, F32, 1024, A_WIDTH, name="in_proj_a")
    pb = _matmul(h, wb, BF16, 1024, 512, name="in_proj_b")
    gsig = _matmul(h, wg, BF16, 1024, 512, act="sigmoid", name="in_proj_gates")

    wuq_p = jnp.pad(w_uq.reshape(MLA_RANK, MLA_HEADS, MLA_QK),
                    ((0, 0), (0, 0), (0, HEAD_PAD - MLA_QK))).reshape(MLA_RANK, MLA_HEADS * HEAD_PAD).astype(BF16)
    q_a, k_a, v_a = _mla_prep(pa, wuq_p, w_ukv.astype(BF16), g_cq.reshape(1, -1), g_ckv.reshape(1, -1),
                              _pad_lanes(g_mla_q, HEAD_PAD), g_mla_k[:MLA_NOPE].reshape(1, -1),
                              _pad_lanes(g_mla_k[MLA_NOPE:], LANES), tabs)
    o_a = _attention(q_a, k_a, v_a)

    wa_p = jnp.zeros((LANES, GLA_HEADS * GLA_DK), F32).at[MISC_GA:MISC_GA + GLA_GATE_RANK].set(w_a2)
    o_b = _gla(pb, pa, wa_p, b_a.reshape(1, -1), g_gla_o.reshape(1, -1))

    f_cum = _fox_cumsum(pa, _pad_lanes(b_f, LANES))
    q_c, k_c, v_c = _fox_prep(pb, f_cum, g_fox_q.reshape(1, -1), g_fox_k.reshape(1, -1))
    o_c = _attention(q_c, k_c, v_c, f_cum)

    merged = _merge(o_a, o_b, o_c, w_branch.reshape(N_BRANCH * BRANCH_WIDTH, D_MODEL).astype(BF16), gsig, 512, 512)
    x1 = _matmul_residual(merged, w_out.astype(BF16), x, 512, 512, name="out_proj")

    h2 = _rmsnorm(x1, g_ffn)
    act = _swiglu_up(h2, w_gu.astype(BF16), 1024, 512)
    return _matmul_residual(act, w_down.astype(BF16), x1, 512, 512, name="ffn_down")


def kernel(x, positions, g_mix, w_in, g_cq, w_uq, g_ckv, w_ukv, g_mla_q, g_mla_k, w_a2, b_a, g_gla_o,
           g_fox_q, g_fox_k, b_f, w_branch, w_out, g_ffn, w_gu, w_down):
    bsz, s, d = x.shape
    assert bsz == 1 and d == D_MODEL
    half = MLA_ROPE // 2
    inv = ROPE_THETA ** (-jnp.arange(half, dtype=F32) / half)
    inv128 = jnp.concatenate([inv, inv, jnp.zeros((LANES - MLA_ROPE,), F32)]).reshape(1, LANES)
    tabs = _rope_tables(positions.reshape(s, 1), inv128)
    hcur = x.reshape(s, d)
    for l in range(w_in.shape[0]):
        hcur = _layer(hcur, tabs, g_mix[l], w_in[l], g_cq[l], w_uq[l], g_ckv[l], w_ukv[l], g_mla_q[l], g_mla_k[l],
                      w_a2[l], b_a[l], g_gla_o[l], g_fox_q[l], g_fox_k[l], b_f[l], w_branch[l], w_out[l],
                      g_ffn[l], w_gu[l], w_down[l])
    return hcur.reshape(bsz, s, d)
```

```python
import functools

import jax
import jax.numpy as jnp
from jax import lax
from jax.experimental import pallas as pl
from jax.experimental.pallas import tpu as pltpu

F32 = jnp.float32
BF16 = jnp.bfloat16

D_MODEL = 2048
BRANCH_WIDTH = D_MODEL // 2
N_BRANCH = 3
MLA_NOPE = 128
MLA_ROPE = 64
MLA_V = 128
MLA_HEADS = BRANCH_WIDTH // MLA_V
MLA_QK = MLA_NOPE + MLA_ROPE
MLA_RANK = 512
ROPE_THETA = 10000.0
GLA_HEADS = 4
GLA_DV = BRANCH_WIDTH // GLA_HEADS
GLA_DK = GLA_DV // 2
GLA_GATE_RANK = 16
GLA_TAU = 16.0
FOX_DH = 128
FOX_HEADS = BRANCH_WIDTH // FOX_DH
FFN_HIDDEN = ((8 * D_MODEL + 3 * 256 - 1) // (3 * 256)) * 256
EPS = 1e-6
NEG_INF = -1e30

LANES = 128
HEAD_PAD = 256
VMEM_LIMIT = 48 * 1024 * 1024

A_CQ, A_CKV, A_KR, A_MISC = 0, 512, 1024, 1152
A_WIDTH = 1280
MISC_FL = 0
MISC_GA = FOX_HEADS
B_GQ, B_GK, B_GV, B_GR, B_FQ, B_FK, B_FV = 0, 512, 1024, 2048, 3072, 4096, 5120
B_WIDTH = 6144

ATT_SUB = 256
LOG2E = 1.4426950408889634

GLA_CHUNK = 64
GLA_SUB = 16
GLA_BLOCK = 512


def _cparams(sem):
    return pltpu.CompilerParams(dimension_semantics=sem, vmem_limit_bytes=VMEM_LIMIT)


def _split3(x):
    x1 = x.astype(BF16)
    r1 = x - x1.astype(F32)
    x2 = r1.astype(BF16)
    x3 = (r1 - x2.astype(F32)).astype(BF16)
    return x1, x2, x3


def _log_sigmoid(z):
    return -(jnp.maximum(-z, 0.0) + jnp.log1p(jnp.exp(-jnp.abs(z))))


def _rmsnorm_kernel(x_ref, g_ref, o_ref):
    x = x_ref[...]
    ms = jnp.mean(x * x, axis=-1, keepdims=True)
    o_ref[...] = (x * lax.rsqrt(ms + EPS) * g_ref[...]).astype(o_ref.dtype)


def _rmsnorm(x, g, tm=512):
    m, d = x.shape
    return pl.pallas_call(
        _rmsnorm_kernel,
        grid=(m // tm,),
        in_specs=[pl.BlockSpec((tm, d), lambda i: (i, 0)),
                  pl.BlockSpec((1, d), lambda i: (0, 0))],
        out_specs=pl.BlockSpec((tm, d), lambda i: (i, 0)),
        out_shape=jax.ShapeDtypeStruct((m, d), BF16),
        compiler_params=_cparams(("parallel",)),
        name="rmsnorm",
    )(x, g.reshape(1, d))


def _mm_kernel(a_ref, w_ref, o_ref, *, act):
    y = jnp.dot(a_ref[...], w_ref[...], preferred_element_type=F32)
    if act == "sigmoid":
        y = jax.nn.sigmoid(y)
    o_ref[...] = y.astype(o_ref.dtype)


def _matmul(a, w, out_dtype, tm, tn, act=None, name="matmul"):
    m, k = a.shape
    n = w.shape[1]
    return pl.pallas_call(
        functools.partial(_mm_kernel, act=act),
        grid=(m // tm, n // tn),
        in_specs=[pl.BlockSpec((tm, k), lambda i, j: (i, 0)),
                  pl.BlockSpec((k, tn), lambda i, j: (0, j))],
        out_specs=pl.BlockSpec((tm, tn), lambda i, j: (i, j)),
        out_shape=jax.ShapeDtypeStruct((m, n), out_dtype),
        compiler_params=_cparams(("parallel", "arbitrary")),
        name=name,
    )(a, w)


def _mm_res_kernel(a_ref, w_ref, r_ref, o_ref):
    o_ref[...] = r_ref[...] + jnp.dot(a_ref[...], w_ref[...], preferred_element_type=F32)


def _matmul_residual(a, w, res, tm, tn, name):
    m, k = a.shape
    n = w.shape[1]
    return pl.pallas_call(
        _mm_res_kernel,
        grid=(m // tm, n // tn),
        in_specs=[pl.BlockSpec((tm, k), lambda i, j: (i, 0)),
                  pl.BlockSpec((k, tn), lambda i, j: (0, j)),
                  pl.BlockSpec((tm, tn), lambda i, j: (i, j))],
        out_specs=pl.BlockSpec((tm, tn), lambda i, j: (i, j)),
        out_shape=jax.ShapeDtypeStruct((m, n), F32),
        compiler_params=_cparams(("parallel", "arbitrary")),
        name=name,
    )(a, w, res)


def _swiglu_kernel(a_ref, wg_ref, wu_ref, o_ref):
    a = a_ref[...]
    g = jnp.dot(a, wg_ref[...], preferred_element_type=F32)
    u = jnp.dot(a, wu_ref[...], preferred_element_type=F32)
    o_ref[...] = (g * jax.nn.sigmoid(g) * u).astype(o_ref.dtype)


def _swiglu_up(a, w_gu, tm, tn):
    m, k = a.shape
    hidden = w_gu.shape[1] // 2
    nj = hidden // tn
    return pl.pallas_call(
        _swiglu_kernel,
        grid=(m // tm, nj),
        in_specs=[pl.BlockSpec((tm, k), lambda i, j: (i, 0)),
                  pl.BlockSpec((k, tn), lambda i, j: (0, j)),
                  pl.BlockSpec((k, tn), lambda i, j: (0, j + nj))],
        out_specs=pl.BlockSpec((tm, tn), lambda i, j: (i, j)),
        out_shape=jax.ShapeDtypeStruct((m, hidden), BF16),
        compiler_params=_cparams(("parallel", "arbitrary")),
        name="ffn_up",
    )(a, w_gu, w_gu)


def _merge_kernel(oa_ref, ob_ref, oc_ref, wa_ref, wb_ref, wc_ref, ga_ref, gb_ref, gc_ref, o_ref):
    acc = jnp.dot(oa_ref[...], wa_ref[...], preferred_element_type=F32) * ga_ref[...].astype(F32)
    acc += jnp.dot(ob_ref[...], wb_ref[...], preferred_element_type=F32) * gb_ref[...].astype(F32)
    acc += jnp.dot(oc_ref[...], wc_ref[...], preferred_element_type=F32) * gc_ref[...].astype(F32)
    o_ref[...] = acc.astype(o_ref.dtype)


def _merge(o_a, o_b, o_c, w_branch2d, gsig, tm, tn):
    m, bw = o_a.shape
    d = w_branch2d.shape[1]
    nj = d // tn
    o_spec = pl.BlockSpec((tm, bw), lambda i, j: (i, 0))

    def w_spec(n):
        return pl.BlockSpec((bw, tn), lambda i, j: (n, j))

    def g_spec(n):
        return pl.BlockSpec((tm, tn), lambda i, j: (i, n * nj + j))

    return pl.pallas_call(
        _merge_kernel,
        grid=(m // tm, nj),
        in_specs=[o_spec, o_spec, o_spec, w_spec(0), w_spec(1), w_spec(2), g_spec(0), g_spec(1), g_spec(2)],
        out_specs=pl.BlockSpec((tm, tn), lambda i, j: (i, j)),
        out_shape=jax.ShapeDtypeStruct((m, d), BF16),
        compiler_params=_cparams(("parallel", "arbitrary")),
        name="branch_merge",
    )(o_a, o_b, o_c, w_branch2d, w_branch2d, w_branch2d, gsig, gsig, gsig)


def _rope_table_kernel(pos_ref, inv_ref, c_ref, s1_ref, s2_ref):
    ang = pos_ref[...].astype(F32) * inv_ref[...]
    c = jnp.cos(ang)
    s = jnp.sin(ang)
    lane = lax.broadcasted_iota(jnp.int32, ang.shape, 1)
    half = MLA_ROPE // 2
    c_ref[...] = jnp.where(lane < MLA_ROPE, c, 0.0)
    s1_ref[...] = jnp.where((lane >= half) & (lane < MLA_ROPE), s, 0.0)
    s2_ref[...] = jnp.where(lane < half, -s, 0.0)


def _rope_tables(pos_col, inv128, tm=512):
    s = pos_col.shape[0]
    spec = pl.BlockSpec((tm, LANES), lambda i: (i, 0))
    shp = jax.ShapeDtypeStruct((s, LANES), F32)
    return pl.pallas_call(
        _rope_table_kernel,
        grid=(s // tm,),
        in_specs=[pl.BlockSpec((tm, 1), lambda i: (i, 0)),
                  pl.BlockSpec((1, LANES), lambda i: (0, 0))],
        out_specs=[spec, spec, spec],
        out_shape=[shp, shp, shp],
        compiler_params=_cparams(("parallel",)),
        name="rope_tables",
    )(pos_col, inv128)


def _rope128(x, c, s1, s2):
    half = MLA_ROPE // 2
    return x * c + pltpu.roll(x, half, 1) * s1 + pltpu.roll(x, LANES - half, 1) * s2


def _mla_prep_kernel(cq_ref, ckv_ref, kr_ref, wuq_ref, wukv_ref, gcq_ref, gckv_ref, gq_ref, gk_ref, gkr_ref,
                     c_ref, s1_ref, s2_ref, q_out, k_out, v_out):
    def norm(x, g):
        return x * lax.rsqrt(jnp.mean(x * x, axis=-1, keepdims=True) + EPS) * g

    cqn = norm(cq_ref[...], gcq_ref[...]).astype(BF16)
    ckvn = norm(ckv_ref[...], gckv_ref[...]).astype(BF16)
    qa = jnp.dot(cqn, wuq_ref[...], preferred_element_type=F32)
    kva = jnp.dot(ckvn, wukv_ref[...], preferred_element_type=F32)
    krp = kr_ref[...]
    kr_ss = jnp.sum(krp * krp, axis=-1, keepdims=True)
    c, s1, s2 = c_ref[...], s1_ref[...], s2_ref[...]
    gq, gk, gkr = gq_ref[...], gk_ref[...], gkr_ref[...]
    scale = MLA_QK ** -0.5 * LOG2E
    for h in range(MLA_HEADS):
        qh = qa[:, h * HEAD_PAD:(h + 1) * HEAD_PAD]
        rq = lax.rsqrt(jnp.sum(qh * qh, axis=-1, keepdims=True) * (1.0 / MLA_QK) + EPS)
        qn = qh * rq * gq
        q_out[h, :, 0:LANES] = (qn[:, 0:LANES] * scale).astype(BF16)
        q_out[h, :, LANES:HEAD_PAD] = (_rope128(qn[:, LANES:HEAD_PAD], c, s1, s2) * scale).astype(BF16)
        kn = kva[:, h * HEAD_PAD:h * HEAD_PAD + MLA_NOPE]
        rk = lax.rsqrt((jnp.sum(kn * kn, axis=-1, keepdims=True) + kr_ss) * (1.0 / MLA_QK) + EPS)
        k_out[h, :, 0:LANES] = (kn * rk * gk).astype(BF16)
        k_out[h, :, LANES:HEAD_PAD] = _rope128(krp * rk * gkr, c, s1, s2).astype(BF16)
        v_out[h] = kva[:, h * HEAD_PAD + MLA_NOPE:(h + 1) * HEAD_PAD].astype(BF16)


def _mla_prep(pa, wuq_p, wukv, g_cq, g_ckv, g_q, g_k, g_kr, tabs, tm=256):
    s = pa.shape[0]
    h = MLA_HEADS
    row = lambda w: pl.BlockSpec((1, w), lambda i: (0, 0))
    tab = pl.BlockSpec((tm, LANES), lambda i: (i, 0))
    return pl.pallas_call(
        _mla_prep_kernel,
        grid=(s // tm,),
        in_specs=[pl.BlockSpec((tm, MLA_RANK), lambda i: (i, A_CQ // MLA_RANK)),
                  pl.BlockSpec((tm, MLA_RANK), lambda i: (i, A_CKV // MLA_RANK)),
                  pl.BlockSpec((tm, LANES), lambda i: (i, A_KR // LANES)),
                  pl.BlockSpec(wuq_p.shape, lambda i: (0, 0)),
                  pl.BlockSpec(wukv.shape, lambda i: (0, 0)),
                  row(MLA_RANK), row(MLA_RANK), row(HEAD_PAD), row(LANES), row(LANES),
                  tab, tab, tab],
        out_specs=[pl.BlockSpec((h, tm, HEAD_PAD), lambda i: (0, i, 0)),
                   pl.BlockSpec((h, tm, HEAD_PAD), lambda i: (0, i, 0)),
                   pl.BlockSpec((h, tm, MLA_V), lambda i: (0, i, 0))],
        out_shape=[jax.ShapeDtypeStruct((h, s, HEAD_PAD), BF16),
                   jax.ShapeDtypeStruct((h, s, HEAD_PAD), BF16),
                   jax.ShapeDtypeStruct((h, s, MLA_V), BF16)],
        compiler_params=_cparams(("parallel",)),
        name="mla_prep",
    )(pa, pa, pa, wuq_p, wukv, g_cq, g_ckv, g_q, g_k, g_kr, *tabs)


def _fox_cumsum_kernel(misc_ref, bf_ref, f_ref, *, blk):
    n = misc_ref.shape[0] // blk
    r = lax.broadcasted_iota(jnp.int32, (blk, blk), 0)
    c = lax.broadcasted_iota(jnp.int32, (blk, blk), 1)
    tri = jnp.where(c <= r, 1.0, 0.0).astype(BF16)
    bias = bf_ref[...]

    def body(i, carry):
        off = pl.multiple_of(i * blk, blk)
        ls = _log_sigmoid(misc_ref[pl.ds(off, blk), :] + bias)
        x1, x2, x3 = _split3(ls)
        cs = (jnp.dot(tri, x1, preferred_element_type=F32)
              + jnp.dot(tri, x2, preferred_element_type=F32)
              + jnp.dot(tri, x3, preferred_element_type=F32)) + carry
        f_ref[pl.ds(off, blk), :] = cs
        return cs[blk - 1:blk, :]

    lax.fori_loop(0, n, body, jnp.zeros((1, LANES), F32))


def _fox_cumsum(pa, bf128, blk=256):
    s = pa.shape[0]
    return pl.pallas_call(
        functools.partial(_fox_cumsum_kernel, blk=blk),
        grid=(1,),
        in_specs=[pl.BlockSpec((s, LANES), lambda i: (0, A_MISC // LANES)),
                  pl.BlockSpec((1, LANES), lambda i: (0, 0))],
        out_specs=pl.BlockSpec((s, LANES), lambda i: (0, 0)),
        out_shape=jax.ShapeDtypeStruct((s, LANES), F32),
        compiler_params=_cparams(("arbitrary",)),
        name="fox_cumsum",
    )(pa, bf128)


def _fox_prep_kernel(fq_ref, fk_ref, fv_ref, f_ref, gq_ref, gk_ref, q_out, k_out, v_out):
    tm = fq_ref.shape[0]
    lane = lax.broadcasted_iota(jnp.int32, (tm, LANES), 1)
    ones_tail = jnp.where(lane < 3, 1.0, 0.0).astype(BF16)
    f_all = f_ref[...]
    scale = FOX_DH ** -0.5 * LOG2E
    gq, gk = gq_ref[...], gk_ref[...]
    for h in range(FOX_HEADS):
        sl = slice(h * FOX_DH, (h + 1) * FOX_DH)
        q = fq_ref[:, sl].astype(F32)
        k = fk_ref[:, sl].astype(F32)
        qn = q * lax.rsqrt(jnp.mean(q * q, axis=-1, keepdims=True) + EPS) * gq
        kn = k * lax.rsqrt(jnp.mean(k * k, axis=-1, keepdims=True) + EPS) * gk
        q_out[h, :, 0:LANES] = (qn * scale).astype(BF16)
        q_out[h, :, LANES:HEAD_PAD] = ones_tail
        k_out[h, :, 0:LANES] = kn.astype(BF16)
        nf = f_all[:, h:h + 1] * (-LOG2E)
        n1, n2, n3 = _split3(nf)
        tail = jnp.where(lane == 0, n1.astype(F32),
                         jnp.where(lane == 1, n2.astype(F32),
                                   jnp.where(lane == 2, n3.astype(F32), 0.0)))
        k_out[h, :, LANES:HEAD_PAD] = tail.astype(BF16)
        v_out[h] = fv_ref[:, sl]


def _fox_prep(pb, f_cum, g_q, g_k, tm=256):
    s = pb.shape[0]
    h = FOX_HEADS
    w = FOX_HEADS * FOX_DH
    row = pl.BlockSpec((1, FOX_DH), lambda i: (0, 0))
    return pl.pallas_call(
        _fox_prep_kernel,
        grid=(s // tm,),
        in_specs=[pl.BlockSpec((tm, w), lambda i: (i, B_FQ // w)),
                  pl.BlockSpec((tm, w), lambda i: (i, B_FK // w)),
                  pl.BlockSpec((tm, w), lambda i: (i, B_FV // w)),
                  pl.BlockSpec((tm, LANES), lambda i: (i, 0)),
                  row, row],
        out_specs=[pl.BlockSpec((h, tm, HEAD_PAD), lambda i: (0, i, 0)),
                   pl.BlockSpec((h, tm, HEAD_PAD), lambda i: (0, i, 0)),
                   pl.BlockSpec((h, tm, FOX_DH), lambda i: (0, i, 0))],
        out_shape=[jax.ShapeDtypeStruct((h, s, HEAD_PAD), BF16),
                   jax.ShapeDtypeStruct((h, s, HEAD_PAD), BF16),
                   jax.ShapeDtypeStruct((h, s, FOX_DH), BF16)],
        compiler_params=_cparams(("parallel",)),
        name="fox_prep",
    )(pb, pb, pb, f_cum, g_q, g_k)


def _attn_kernel(*refs, tq, use_f):
    if use_f:
        q_ref, k_ref, v_ref, f_ref, o_ref, m_sc, l_sc, acc_sc = refs
    else:
        q_ref, k_ref, v_ref, o_ref, m_sc, l_sc, acc_sc = refs
    h = pl.program_id(0)
    qi = pl.program_id(1)
    ns = tq // ATT_SUB
    if use_f:
        lane = lax.broadcasted_iota(jnp.int32, (ATT_SUB, LANES), 1)
        ft = []
        for j in range(ns):
            fj = f_ref[j * ATT_SUB:(j + 1) * ATT_SUB, :]
            ft.append(jnp.broadcast_to(jnp.sum(jnp.where(lane == h, fj, 0.0), axis=-1, keepdims=True) * LOG2E,
                                       (ATT_SUB, LANES)))
    m_sc[...] = jnp.full(m_sc.shape, NEG_INF, F32)
    l_sc[...] = jnp.zeros(l_sc.shape, F32)
    acc_sc[...] = jnp.zeros(acc_sc.shape, F32)

    def update(j, start, size, mask):
        q = q_ref[0, j * ATT_SUB:(j + 1) * ATT_SUB, :]
        k = k_ref[0, pl.ds(start, size), :]
        v = v_ref[0, pl.ds(start, size), :]
        s = lax.dot_general(q, k, (((1,), (1,)), ((), ())), preferred_element_type=F32)
        if mask is not None:
            s = jnp.where(mask, s, NEG_INF)
        m_old = m_sc[j]
        m_cur = jnp.max(s, axis=-1, keepdims=True)
        if use_f:
            m_new = jnp.maximum(m_old, m_cur + ft[j])
            shift = m_new - ft[j]
        else:
            m_new = jnp.maximum(m_old, m_cur)
            shift = m_new
        p = jnp.exp2(s - jnp.concatenate([shift] * (size // LANES), axis=1))
        alpha = jnp.exp2(m_old - m_new)
        l_sc[j] = alpha * l_sc[j] + jnp.sum(p, axis=-1, keepdims=True)
        acc_sc[j] = alpha * acc_sc[j] + jnp.dot(p.astype(BF16), v, preferred_element_type=F32)
        m_sc[j] = m_new

    def body(kb, carry):
        start = pl.multiple_of(kb * tq, tq)
        for j in range(ns):
            update(j, start, tq, None)
        return carry

    lax.fori_loop(0, qi, body, 0)
    row = lax.broadcasted_iota(jnp.int32, (ATT_SUB, ATT_SUB), 0)
    col = lax.broadcasted_iota(jnp.int32, (ATT_SUB, ATT_SUB), 1)
    tri = col <= row
    base = pl.multiple_of(qi * tq, tq)
    for j in range(ns):
        if j > 0:
            update(j, base, j * ATT_SUB, None)
        update(j, pl.multiple_of(base + j * ATT_SUB, ATT_SUB), ATT_SUB, tri)
    for j in range(ns):
        o_ref[j * ATT_SUB:(j + 1) * ATT_SUB, :] = (acc_sc[j] / l_sc[j]).astype(o_ref.dtype)


def _attention(q, k, v, f_cum=None, tq=1024):
    hh, s, dk = q.shape
    dv = v.shape[-1]
    assert dv == LANES
    use_f = f_cum is not None
    in_specs = [pl.BlockSpec((1, tq, dk), lambda h, i: (h, i, 0)),
                pl.BlockSpec((1, s, dk), lambda h, i: (h, 0, 0)),
                pl.BlockSpec((1, s, dv), lambda h, i: (h, 0, 0))]
    args = [q, k, v]
    if use_f:
        in_specs.append(pl.BlockSpec((tq, LANES), lambda h, i: (i, 0)))
        args.append(f_cum)
    return pl.pallas_call(
        functools.partial(_attn_kernel, tq=tq, use_f=use_f),
        grid=(hh, s // tq),
        in_specs=in_specs,
        out_specs=pl.BlockSpec((tq, dv), lambda h, i: (i, h)),
        out_shape=jax.ShapeDtypeStruct((s, hh * dv), BF16),
        scratch_shapes=[pltpu.VMEM((tq // ATT_SUB, ATT_SUB, LANES), F32) for _ in range(3)],
        compiler_params=_cparams(("parallel", "arbitrary")),
        name="fox_attention" if use_f else "mla_attention",
    )(*args)


def _gla_kernel(q_ref, k_ref, v_ref, r_ref, misc_ref, wa_ref, ba_ref, go_ref, o_ref, st_ref, la_ref):
    tb = q_ref.shape[0]
    nchunk = tb // GLA_CHUNK
    nsub = GLA_CHUNK // GLA_SUB
    C = GLA_CHUNK

    @pl.when(pl.program_id(1) == 0)
    def _():
        st_ref[...] = jnp.zeros(st_ref.shape, F32)

    m1, m2, _ = _split3(misc_ref[...])
    w1, w2, _ = _split3(wa_ref[...])
    z = (jnp.dot(m1, w1, preferred_element_type=F32) + jnp.dot(m1, w2, preferred_element_type=F32)
         + jnp.dot(m2, w1, preferred_element_type=F32)) + ba_ref[...]
    la_ref[...] = _log_sigmoid(z) * (1.0 / GLA_TAU)

    r_i = lax.broadcasted_iota(jnp.int32, (C, C), 0)
    c_i = lax.broadcasted_iota(jnp.int32, (C, C), 1)
    tri = jnp.where(c_i <= r_i, 1.0, 0.0).astype(BF16)
    sub_row = lax.broadcasted_iota(jnp.int32, (GLA_SUB, GLA_DK), 0)
    col_sub = lax.broadcasted_iota(jnp.int32, (GLA_SUB, C), 1)
    scale = GLA_DK ** -0.5
    g_o = go_ref[...]

    def chunk(ci, carry):
        rows = pl.ds(pl.multiple_of(ci * C, C), C)
        l1, l2, l3 = _split3(la_ref[rows, :])
        b = (jnp.dot(tri, l1, preferred_element_type=F32) + jnp.dot(tri, l2, preferred_element_type=F32)
             + jnp.dot(tri, l3, preferred_element_type=F32))
        qc = q_ref[rows, :].astype(F32) * scale
        kc = k_ref[rows, :].astype(F32)
        vc = v_ref[rows, :]
        vcf = vc.astype(F32)
        st = st_ref[...]

        o = lax.dot_general((qc * jnp.exp(b)).astype(BF16), st.astype(BF16),
                            (((1,), (1,)), ((), ())), preferred_element_type=F32)

        a_rows = [jnp.zeros((GLA_SUB, C), F32)]
        for si in range(1, nsub):
            lo = si * GLA_SUB
            ref_row = b[lo - 1:lo, :]
            q_i = qc[lo:lo + GLA_SUB] * jnp.exp(b[lo:lo + GLA_SUB] - ref_row)
            k_j = kc * jnp.exp(jnp.minimum(ref_row - b, 0.0))
            a = lax.dot_general(q_i.astype(BF16), k_j.astype(BF16), (((1,), (1,)), ((), ())),
                                preferred_element_type=F32)
            a_rows.append(jnp.where(col_sub < lo, a, 0.0))
        a_off = jnp.concatenate(a_rows, axis=0)
        o = o + jnp.dot(a_off.astype(BF16), vc, preferred_element_type=F32)

        o_diag = []
        for si in range(nsub):
            lo = si * GLA_SUB
            qs, ks, bs, vs = qc[lo:lo + GLA_SUB], kc[lo:lo + GLA_SUB], b[lo:lo + GLA_SUB], vcf[lo:lo + GLA_SUB]
            od = jnp.zeros((GLA_SUB, GLA_DV), F32)
            for j in range(GLA_SUB):
                e = jnp.exp(jnp.where(sub_row >= j, bs - bs[j:j + 1, :], NEG_INF))
                w = jnp.sum(qs * (ks[j:j + 1, :] * e), axis=-1, keepdims=True)
                od = od + w * vs[j:j + 1, :]
            o_diag.append(od)
        o = o + jnp.concatenate(o_diag, axis=0)

        b_last = b[C - 1:C, :]
        ke = (kc * jnp.exp(b_last - b)).astype(BF16)
        st_ref[...] = st * jnp.exp(b_last) + lax.dot_general(
            vc, ke, (((0,), (0,)), ((), ())), preferred_element_type=F32)

        on = o * lax.rsqrt(jnp.mean(o * o, axis=-1, keepdims=True) + EPS) * g_o
        r = r_ref[rows, :].astype(F32)
        o_ref[rows, :] = (on * (r * jax.nn.sigmoid(r))).astype(o_ref.dtype)
        return carry

    lax.fori_loop(0, nchunk, chunk, 0)


def _gla(pb, pa, wa_p, b_a, g_o, tb=GLA_BLOCK):
    s = pb.shape[0]
    return pl.pallas_call(
        _gla_kernel,
        grid=(GLA_HEADS, s // tb),
        in_specs=[pl.BlockSpec((tb, GLA_DK), lambda h, i: (i, B_GQ // GLA_DK + h)),
                  pl.BlockSpec((tb, GLA_DK), lambda h, i: (i, B_GK // GLA_DK + h)),
                  pl.BlockSpec((tb, GLA_DV), lambda h, i: (i, B_GV // GLA_DV + h)),
                  pl.BlockSpec((tb, GLA_DV), lambda h, i: (i, B_GR // GLA_DV + h)),
                  pl.BlockSpec((tb, LANES), lambda h, i: (i, A_MISC // LANES)),
                  pl.BlockSpec((LANES, GLA_DK), lambda h, i: (0, h)),
                  pl.BlockSpec((1, GLA_DK), lambda h, i: (0, h)),
                  pl.BlockSpec((1, GLA_DV), lambda h, i: (0, 0))],
        out_specs=pl.BlockSpec((tb, GLA_DV), lambda h, i: (i, h)),
        out_shape=jax.ShapeDtypeStruct((s, GLA_HEADS * GLA_DV), BF16),
        scratch_shapes=[pltpu.VMEM((GLA_DV, GLA_DK), F32), pltpu.VMEM((tb, GLA_DK), F32)],
        compiler_params=_cparams(("parallel", "arbitrary")),
        name="gla",
    )(pb, pb, pb, pb, pa, wa_p, b_a, g_o)


def _group_w_in(w):
    d = w.shape[0]
    off = [0]
    for sz in (512, 512, 64, 512, 512, 1024, 16, 1024, 1024, 1024, 1024, 8, N_BRANCH * D_MODEL):
        off.append(off[-1] + sz)
    cq, ckv, kr, gq, gk, gv, ga, gr, fq, fk, fv, fl, gates = [w[:, off[i]:off[i + 1]] for i in range(13)]
    z = lambda n: jnp.zeros((d, n), w.dtype)
    wa = jnp.concatenate([cq, ckv, kr, z(LANES - MLA_ROPE), fl, ga, z(LANES - FOX_HEADS - GLA_GATE_RANK)], axis=1)
    wb = jnp.concatenate([gq, gk, gv, gr, fq, fk, fv], axis=1)
    return wa.astype(BF16), wb.astype(BF16), gates.astype(BF16)


def _pad_lanes(v, n):
    return jnp.pad(v, (0, n - v.shape[0])).reshape(1, n)


def _layer(x, tabs, g_mix, w_in, g_cq, w_uq, g_ckv, w_ukv, g_mla_q, g_mla_k, w_a2, b_a, g_gla_o,
           g_fox_q, g_fox_k, b_f, w_branch, w_out, g_ffn, w_gu, w_down):
    wa, wb, wg = _group_w_in(w_in)
    h = _rmsnorm(x, g_mix)
    pa = _matmul(h, wa, F32, 1024, A_WIDTH, name="in_proj_a")
    pb = _matmul(h, wb, BF16, 1024, 512, name="in_proj_b")
    gsig = _matmul(h, wg, BF16, 1024, 512, act="sigmoid", name="in_proj_gates")

    wuq_p = jnp.pad(w_uq.reshape(MLA_RANK, MLA_HEADS, MLA_QK),
                    ((0, 0), (0, 0), (0, HEAD_PAD - MLA_QK))).reshape(MLA_RANK, MLA_HEADS * HEAD_PAD).astype(BF16)
    q_a, k_a, v_a = _mla_prep(pa, wuq_p, w_ukv.astype(BF16), g_cq.reshape(1, -1), g_ckv.reshape(1, -1),
                              _pad_lanes(g_mla_q, HEAD_PAD), g_mla_k[:MLA_NOPE].reshape(1, -1),
                              _pad_lanes(g_mla_k[MLA_NOPE:], LANES), tabs)
    o_a = _attention(q_a, k_a, v_a)

    wa_p = jnp.zeros((LANES, GLA_HEADS * GLA_DK), F32).at[MISC_GA:MISC_GA + GLA_GATE_RANK].set(w_a2)
    o_b = _gla(pb, pa, wa_p, b_a.reshape(1, -1), g_gla_o.reshape(1, -1))

    f_cum = _fox_cumsum(pa, _pad_lanes(b_f, LANES))
    q_c, k_c, v_c = _fox_prep(pb, f_cum, g_fox_q.reshape(1, -1), g_fox_k.reshape(1, -1))
    o_c = _attention(q_c, k_c, v_c, f_cum)

    merged = _merge(o_a, o_b, o_c, w_branch.reshape(N_BRANCH * BRANCH_WIDTH, D_MODEL).astype(BF16), gsig, 512, 512)
    x1 = _matmul_residual(merged, w_out.astype(BF16), x, 512, 512, name="out_proj")

    h2 = _rmsnorm(x1, g_ffn)
    act = _swiglu_up(h2, w_gu.astype(BF16), 1024, 512)
    return _matmul_residual(act, w_down.astype(BF16), x1, 512, 512, name="ffn_down")


def kernel(x, positions, g_mix, w_in, g_cq, w_uq, g_ckv, w_ukv, g_mla_q, g_mla_k, w_a2, b_a, g_gla_o,
           g_fox_q, g_fox_k, b_f, w_branch, w_out, g_ffn, w_gu, w_down):
    bsz, s, d = x.shape
    assert bsz == 1 and d == D_MODEL
    half = MLA_ROPE // 2
    inv = ROPE_THETA ** (-jnp.arange(half, dtype=F32) / half)
    inv128 = jnp.concatenate([inv, inv, jnp.zeros((LANES - MLA_ROPE,), F32)]).reshape(1, LANES)
    tabs = _rope_tables(positions.reshape(s, 1), inv128)
    hcur = x.reshape(s, d)
    for l in range(w_in.shape[0]):
        hcur = _layer(hcur, tabs, g_mix[l], w_in[l], g_cq[l], w_uq[l], g_ckv[l], w_ukv[l], g_mla_q[l], g_mla_k[l],
                      w_a2[l], b_a[l], g_gla_o[l], g_fox_q[l], g_fox_k[l], b_f[l], w_branch[l], w_out[l],
                      g_ffn[l], w_gu[l], w_down[l])
    return hcur.reshape(bsz, s, d)
```

```python
import functools

import jax
import jax.numpy as jnp
from jax import lax
from jax.experimental import pallas as pl
from jax.experimental.pallas import tpu as pltpu

F32 = jnp.float32
BF16 = jnp.bfloat16

D_MODEL = 2048
BRANCH_WIDTH = D_MODEL // 2
N_BRANCH = 3
MLA_NOPE = 128
MLA_ROPE = 64
MLA_V = 128
MLA_HEADS = BRANCH_WIDTH // MLA_V
MLA_QK = MLA_NOPE + MLA_ROPE
MLA_RANK = 512
ROPE_THETA = 10000.0
GLA_HEADS = 4
GLA_DV = BRANCH_WIDTH // GLA_HEADS
GLA_DK = GLA_DV // 2
GLA_GATE_RANK = 16
GLA_TAU = 16.0
FOX_DH = 128
FOX_HEADS = BRANCH_WIDTH // FOX_DH
FFN_HIDDEN = ((8 * D_MODEL + 3 * 256 - 1) // (3 * 256)) * 256
EPS = 1e-6
NEG_INF = -1e30

LANES = 128
BF16_SUBLANES = 16
HEAD_PAD = 256
VMEM_LIMIT = 48 * 1024 * 1024

A_CQ, A_CKV, A_KR, A_MISC = 0, 512, 1024, 1152
A_WIDTH = 1280
MISC_FL = 0
MISC_GA = FOX_HEADS
B_GQ, B_GK, B_GV, B_GR, B_FQ, B_FK, B_FV = 0, 512, 1024, 2048, 3072, 4096, 5120
B_WIDTH = 6144

ATT_TILE = 1024
ATT_SUB = 256
ATT_VROWS = LANES + BF16_SUBLANES
LOG2E = 1.4426950408889634

GLA_CHUNK = 64
GLA_SUB = 16
GLA_BLOCK = 512


def _aligned(start, m):
    return start if isinstance(start, int) else pl.multiple_of(start, m)


def _cparams(sem):
    return pltpu.CompilerParams(dimension_semantics=sem, vmem_limit_bytes=VMEM_LIMIT)


def _split3(x):
    x1 = x.astype(BF16)
    r1 = x - x1.astype(F32)
    x2 = r1.astype(BF16)
    x3 = (r1 - x2.astype(F32)).astype(BF16)
    return x1, x2, x3


def _log_sigmoid(z):
    return -(jnp.maximum(-z, 0.0) + jnp.log1p(jnp.exp(-jnp.abs(z))))


def _value_rows_t(v_f32):
    rows = v_f32.shape[0]
    r = lax.broadcasted_iota(jnp.int32, (BF16_SUBLANES, rows), 0)
    return v_f32.T.astype(BF16), jnp.where(r == 0, 1.0, 0.0).astype(BF16)


def _rmsnorm_kernel(x_ref, g_ref, o_ref):
    x = x_ref[...]
    ms = jnp.mean(x * x, axis=-1, keepdims=True)
    o_ref[...] = (x * lax.rsqrt(ms + EPS) * g_ref[...]).astype(o_ref.dtype)


def _rmsnorm(x, g, tm=512):
    m, d = x.shape
    return pl.pallas_call(
        _rmsnorm_kernel,
        grid=(m // tm,),
        in_specs=[pl.BlockSpec((tm, d), lambda i: (i, 0)),
                  pl.BlockSpec((1, d), lambda i: (0, 0))],
        out_specs=pl.BlockSpec((tm, d), lambda i: (i, 0)),
        out_shape=jax.ShapeDtypeStruct((m, d), BF16),
        compiler_params=_cparams(("parallel",)),
        name="rmsnorm",
    )(x, g.reshape(1, d))


def _mm_kernel(a_ref, w_ref, o_ref, *, act):
    y = jnp.dot(a_ref[...], w_ref[...], preferred_element_type=F32)
    if act == "sigmoid":
        y = jax.nn.sigmoid(y)
    o_ref[...] = y.astype(o_ref.dtype)


def _matmul(a, w, out_dtype, tm, tn, act=None, name="matmul"):
    m, k = a.shape
    n = w.shape[1]
    return pl.pallas_call(
        functools.partial(_mm_kernel, act=act),
        grid=(m // tm, n // tn),
        in_specs=[pl.BlockSpec((tm, k), lambda i, j: (i, 0)),
                  pl.BlockSpec((k, tn), lambda i, j: (0, j))],
        out_specs=pl.BlockSpec((tm, tn), lambda i, j: (i, j)),
        out_shape=jax.ShapeDtypeStruct((m, n), out_dtype),
        compiler_params=_cparams(("parallel", "arbitrary")),
        name=name,
    )(a, w)


def _mm_res_kernel(a_ref, w_ref, r_ref, o_ref):
    o_ref[...] = r_ref[...] + jnp.dot(a_ref[...], w_ref[...], preferred_element_type=F32)


def _matmul_residual(a, w, res, tm, tn, name):
    m, k = a.shape
    n = w.shape[1]
    return pl.pallas_call(
        _mm_res_kernel,
        grid=(m // tm, n // tn),
        in_specs=[pl.BlockSpec((tm, k), lambda i, j: (i, 0)),
                  pl.BlockSpec((k, tn), lambda i, j: (0, j)),
                  pl.BlockSpec((tm, tn), lambda i, j: (i, j))],
        out_specs=pl.BlockSpec((tm, tn), lambda i, j: (i, j)),
        out_shape=jax.ShapeDtypeStruct((m, n), F32),
        compiler_params=_cparams(("parallel", "arbitrary")),
        name=name,
    )(a, w, res)


def _swiglu_kernel(a_ref, wg_ref, wu_ref, o_ref):
    a = a_ref[...]
    g = jnp.dot(a, wg_ref[...], preferred_element_type=F32)
    u = jnp.dot(a, wu_ref[...], preferred_element_type=F32)
    o_ref[...] = (g * jax.nn.sigmoid(g) * u).astype(o_ref.dtype)


def _swiglu_up(a, w_gu, tm, tn):
    m, k = a.shape
    hidden = w_gu.shape[1] // 2
    nj = hidden // tn
    return pl.pallas_call(
        _swiglu_kernel,
        grid=(m // tm, nj),
        in_specs=[pl.BlockSpec((tm, k), lambda i, j: (i, 0)),
                  pl.BlockSpec((k, tn), lambda i, j: (0, j)),
                  pl.BlockSpec((k, tn), lambda i, j: (0, j + nj))],
        out_specs=pl.BlockSpec((tm, tn), lambda i, j: (i, j)),
        out_shape=jax.ShapeDtypeStruct((m, hidden), BF16),
        compiler_params=_cparams(("parallel", "arbitrary")),
        name="ffn_up",
    )(a, w_gu, w_gu)


def _merge_kernel(oa_ref, ob_ref, oc_ref, wa_ref, wb_ref, wc_ref, ga_ref, gb_ref, gc_ref, o_ref):
    acc = jnp.dot(oa_ref[...], wa_ref[...], preferred_element_type=F32) * ga_ref[...].astype(F32)
    acc += jnp.dot(ob_ref[...], wb_ref[...], preferred_element_type=F32) * gb_ref[...].astype(F32)
    acc += jnp.dot(oc_ref[...], wc_ref[...], preferred_element_type=F32) * gc_ref[...].astype(F32)
    o_ref[...] = acc.astype(o_ref.dtype)


def _merge(o_a, o_b, o_c, w_branch2d, gsig, tm, tn):
    m, bw = o_a.shape
    d = w_branch2d.shape[1]
    nj = d // tn
    o_spec = pl.BlockSpec((tm, bw), lambda i, j: (i, 0))

    def w_spec(n):
        return pl.BlockSpec((bw, tn), lambda i, j: (n, j))

    def g_spec(n):
        return pl.BlockSpec((tm, tn), lambda i, j: (i, n * nj + j))

    return pl.pallas_call(
        _merge_kernel,
        grid=(m // tm, nj),
        in_specs=[o_spec, o_spec, o_spec, w_spec(0), w_spec(1), w_spec(2), g_spec(0), g_spec(1), g_spec(2)],
        out_specs=pl.BlockSpec((tm, tn), lambda i, j: (i, j)),
        out_shape=jax.ShapeDtypeStruct((m, d), BF16),
        compiler_params=_cparams(("parallel", "arbitrary")),
        name="branch_merge",
    )(o_a, o_b, o_c, w_branch2d, w_branch2d, w_branch2d, gsig, gsig, gsig)


def _rope_table_kernel(pos_ref, inv_ref, c_ref, s1_ref, s2_ref):
    ang = pos_ref[...].astype(F32) * inv_ref[...]
    c = jnp.cos(ang)
    s = jnp.sin(ang)
    lane = lax.broadcasted_iota(jnp.int32, ang.shape, 1)
    half = MLA_ROPE // 2
    c_ref[...] = jnp.where(lane < MLA_ROPE, c, 0.0)
    s1_ref[...] = jnp.where((lane >= half) & (lane < MLA_ROPE), s, 0.0)
    s2_ref[...] = jnp.where(lane < half, -s, 0.0)


def _rope_tables(pos_col, inv128, tm=512):
    s = pos_col.shape[0]
    spec = pl.BlockSpec((tm, LANES), lambda i: (i, 0))
    shp = jax.ShapeDtypeStruct((s, LANES), F32)
    return pl.pallas_call(
        _rope_table_kernel,
        grid=(s // tm,),
        in_specs=[pl.BlockSpec((tm, 1), lambda i: (i, 0)),
                  pl.BlockSpec((1, LANES), lambda i: (0, 0))],
        out_specs=[spec, spec, spec],
        out_shape=[shp, shp, shp],
        compiler_params=_cparams(("parallel",)),
        name="rope_tables",
    )(pos_col, inv128)


def _rope128(x, c, s1, s2):
    half = MLA_ROPE // 2
    return x * c + pltpu.roll(x, half, 1) * s1 + pltpu.roll(x, LANES - half, 1) * s2


def _mla_prep_kernel(cq_ref, ckv_ref, kr_ref, wuq_ref, wukv_ref, gcq_ref, gckv_ref, gq_ref, gk_ref, gkr_ref,
                     c_ref, s1_ref, s2_ref, q_out, k_out, vt_out):
    def norm(x, g):
        return x * lax.rsqrt(jnp.mean(x * x, axis=-1, keepdims=True) + EPS) * g

    cqn = norm(cq_ref[...], gcq_ref[...]).astype(BF16)
    ckvn = norm(ckv_ref[...], gckv_ref[...]).astype(BF16)
    qa = jnp.dot(cqn, wuq_ref[...], preferred_element_type=F32)
    kva = jnp.dot(ckvn, wukv_ref[...], preferred_element_type=F32)
    krp = kr_ref[...]
    kr_ss = jnp.sum(krp * krp, axis=-1, keepdims=True)
    c, s1, s2 = c_ref[...], s1_ref[...], s2_ref[...]
    gq, gk, gkr = gq_ref[...], gk_ref[...], gkr_ref[...]
    scale = MLA_QK ** -0.5 * LOG2E
    for h in range(MLA_HEADS):
        qh = qa[:, h * HEAD_PAD:(h + 1) * HEAD_PAD]
        rq = lax.rsqrt(jnp.sum(qh * qh, axis=-1, keepdims=True) * (1.0 / MLA_QK) + EPS)
        qn = qh * rq * gq
        q_out[h, :, 0:LANES] = (qn[:, 0:LANES] * scale).astype(BF16)
        q_out[h, :, LANES:HEAD_PAD] = (_rope128(qn[:, LANES:HEAD_PAD], c, s1, s2) * scale).astype(BF16)
        kn = kva[:, h * HEAD_PAD:h * HEAD_PAD + MLA_NOPE]
        rk = lax.rsqrt((jnp.sum(kn * kn, axis=-1, keepdims=True) + kr_ss) * (1.0 / MLA_QK) + EPS)
        k_out[h, :, 0:LANES] = (kn * rk * gk).astype(BF16)
        k_out[h, :, LANES:HEAD_PAD] = _rope128(krp * rk * gkr, c, s1, s2).astype(BF16)
        v_t, ones_rows = _value_rows_t(kva[:, h * HEAD_PAD + MLA_NOPE:(h + 1) * HEAD_PAD])
        vt_out[h, 0, 0:LANES, :] = v_t
        vt_out[h, 0, LANES:ATT_VROWS, :] = ones_rows


def _vt_spec(heads, tm):
    per_tile = ATT_TILE // tm
    return pl.BlockSpec((heads, 1, ATT_VROWS, tm), lambda i: (0, i // per_tile, 0, i % per_tile))


def _mla_prep(pa, wuq_p, wukv, g_cq, g_ckv, g_q, g_k, g_kr, tabs, tm=256):
    s = pa.shape[0]
    h = MLA_HEADS
    row = lambda w: pl.BlockSpec((1, w), lambda i: (0, 0))
    tab = pl.BlockSpec((tm, LANES), lambda i: (i, 0))
    return pl.pallas_call(
        _mla_prep_kernel,
        grid=(s // tm,),
        in_specs=[pl.BlockSpec((tm, MLA_RANK), lambda i: (i, A_CQ // MLA_RANK)),
                  pl.BlockSpec((tm, MLA_RANK), lambda i: (i, A_CKV // MLA_RANK)),
                  pl.BlockSpec((tm, LANES), lambda i: (i, A_KR // LANES)),
                  pl.BlockSpec(wuq_p.shape, lambda i: (0, 0)),
                  pl.BlockSpec(wukv.shape, lambda i: (0, 0)),
                  row(MLA_RANK), row(MLA_RANK), row(HEAD_PAD), row(LANES), row(LANES),
                  tab, tab, tab],
        out_specs=[pl.BlockSpec((h, tm, HEAD_PAD), lambda i: (0, i, 0)),
                   pl.BlockSpec((h, tm, HEAD_PAD), lambda i: (0, i, 0)),
                   _vt_spec(h, tm)],
        out_shape=[jax.ShapeDtypeStruct((h, s, HEAD_PAD), BF16),
                   jax.ShapeDtypeStruct((h, s, HEAD_PAD), BF16),
                   jax.ShapeDtypeStruct((h, s // ATT_TILE, ATT_VROWS, ATT_TILE), BF16)],
        compiler_params=_cparams(("parallel",)),
        name="mla_prep",
    )(pa, pa, pa, wuq_p, wukv, g_cq, g_ckv, g_q, g_k, g_kr, *tabs)


def _fox_cumsum_kernel(misc_ref, bf_ref, f_ref, ft_ref, carry_ref):
    blk = misc_ref.shape[0]

    @pl.when(pl.program_id(0) == 0)
    def _():
        carry_ref[...] = jnp.zeros(carry_ref.shape, F32)

    r = lax.broadcasted_iota(jnp.int32, (blk, blk), 0)
    c = lax.broadcasted_iota(jnp.int32, (blk, blk), 1)
    tri = jnp.where(c <= r, 1.0, 0.0).astype(BF16)
    x1, x2, x3 = _split3(_log_sigmoid(misc_ref[...] + bf_ref[...]))
    cs = (jnp.dot(tri, x1, preferred_element_type=F32)
          + jnp.dot(tri, x2, preferred_element_type=F32)
          + jnp.dot(tri, x3, preferred_element_type=F32)) + carry_ref[...]
    f_ref[...] = cs
    ft_ref[...] = cs.T
    carry_ref[...] = cs[blk - 1:blk, :]


def _fox_cumsum(pa, bf128, blk=512):
    s = pa.shape[0]
    return pl.pallas_call(
        _fox_cumsum_kernel,
        grid=(s // blk,),
        in_specs=[pl.BlockSpec((blk, LANES), lambda i: (i, A_MISC // LANES)),
                  pl.BlockSpec((1, LANES), lambda i: (0, 0))],
        out_specs=[pl.BlockSpec((blk, LANES), lambda i: (i, 0)),
                   pl.BlockSpec((LANES, blk), lambda i: (0, i))],
        out_shape=[jax.ShapeDtypeStruct((s, LANES), F32), jax.ShapeDtypeStruct((LANES, s), F32)],
        scratch_shapes=[pltpu.VMEM((1, LANES), F32)],
        compiler_params=_cparams(("arbitrary",)),
        name="fox_cumsum",
    )(pa, bf128)


def _fox_prep_kernel(fq_ref, fk_ref, fv_ref, f_ref, gq_ref, gk_ref, q_out, k_out, vt_out):
    tm = fq_ref.shape[0]
    lane = lax.broadcasted_iota(jnp.int32, (tm, LANES), 1)
    ones_tail = jnp.where(lane < 3, 1.0, 0.0).astype(BF16)
    f_all = f_ref[...]
    scale = FOX_DH ** -0.5 * LOG2E
    gq, gk = gq_ref[...], gk_ref[...]
    for h in range(FOX_HEADS):
        sl = slice(h * FOX_DH, (h + 1) * FOX_DH)
        q = fq_ref[:, sl].astype(F32)
        k = fk_ref[:, sl].astype(F32)
        qn = q * lax.rsqrt(jnp.mean(q * q, axis=-1, keepdims=True) + EPS) * gq
        kn = k * lax.rsqrt(jnp.mean(k * k, axis=-1, keepdims=True) + EPS) * gk
        q_out[h, :, 0:LANES] = (qn * scale).astype(BF16)
        q_out[h, :, LANES:HEAD_PAD] = ones_tail
        k_out[h, :, 0:LANES] = kn.astype(BF16)
        nf = f_all[:, h:h + 1] * (-LOG2E)
        n1, n2, n3 = _split3(nf)
        tail = jnp.where(lane == 0, n1.astype(F32),
                         jnp.where(lane == 1, n2.astype(F32),
                                   jnp.where(lane == 2, n3.astype(F32), 0.0)))
        k_out[h, :, LANES:HEAD_PAD] = tail.astype(BF16)
        v_t, ones_rows = _value_rows_t(fv_ref[:, sl].astype(F32))
        vt_out[h, 0, 0:LANES, :] = v_t
        vt_out[h, 0, LANES:ATT_VROWS, :] = ones_rows


def _fox_prep(pb, f_cum, g_q, g_k, tm=256):
    s = pb.shape[0]
    h = FOX_HEADS
    w = FOX_HEADS * FOX_DH
    row = pl.BlockSpec((1, FOX_DH), lambda i: (0, 0))
    return pl.pallas_call(
        _fox_prep_kernel,
        grid=(s // tm,),
        in_specs=[pl.BlockSpec((tm, w), lambda i: (i, B_FQ // w)),
                  pl.BlockSpec((tm, w), lambda i: (i, B_FK // w)),
                  pl.BlockSpec((tm, w), lambda i: (i, B_FV // w)),
                  pl.BlockSpec((tm, LANES), lambda i: (i, 0)),
                  row, row],
        out_specs=[pl.BlockSpec((h, tm, HEAD_PAD), lambda i: (0, i, 0)),
                   pl.BlockSpec((h, tm, HEAD_PAD), lambda i: (0, i, 0)),
                   _vt_spec(h, tm)],
        out_shape=[jax.ShapeDtypeStruct((h, s, HEAD_PAD), BF16),
                   jax.ShapeDtypeStruct((h, s, HEAD_PAD), BF16),
                   jax.ShapeDtypeStruct((h, s // ATT_TILE, ATT_VROWS, ATT_TILE), BF16)],
        compiler_params=_cparams(("parallel",)),
        name="fox_prep",
    )(pb, pb, pb, f_cum, g_q, g_k)


def _attn_kernel(*refs, tq, use_f):
    if use_f:
        q_ref, k_ref, vt_ref, f_ref, o_ref, m_sc, acc_sc, s_a, s_b = refs
    else:
        q_ref, k_ref, vt_ref, o_ref, m_sc, acc_sc, s_a, s_b = refs
    h = pl.program_id(0)
    qi = pl.program_id(1)
    ns = tq // ATT_SUB
    if use_f:
        ft = [f_ref[pl.ds(h, 1), j * ATT_SUB:(j + 1) * ATT_SUB] * LOG2E for j in range(ns)]
    m_sc[...] = jnp.full(m_sc.shape, NEG_INF, F32)
    acc_sc[...] = jnp.zeros(acc_sc.shape, F32)

    def logits(dst, kb):
        dst[...] = lax.dot_general(k_ref[0, pl.ds(_aligned(kb * tq, tq), tq), :], q_ref[0],
                                   (((1,), (1,)), ((), ())), preferred_element_type=F32)

    def update(src, j, kb, size, mask):
        s = src[0:size, j * ATT_SUB:(j + 1) * ATT_SUB]
        if mask:
            key = lax.broadcasted_iota(jnp.int32, (size, ATT_SUB), 0)
            qry = lax.broadcasted_iota(jnp.int32, (size, ATT_SUB), 1)
            s = jnp.where(key <= qry + j * ATT_SUB, s, NEG_INF)
        m_old = m_sc[j]
        m_cur = jnp.max(s, axis=0, keepdims=True)
        if use_f:
            m_new = jnp.maximum(m_old, m_cur + ft[j])
            shift = m_new - ft[j]
        else:
            m_new = jnp.maximum(m_old, m_cur)
            shift = m_new
        p = jnp.exp2(s - shift).astype(BF16)
        alpha = jnp.exp2(m_old - m_new)
        acc_sc[j] = alpha * acc_sc[j] + jnp.dot(vt_ref[0, kb, :, 0:size], p, preferred_element_type=F32)
        m_sc[j] = m_new

    def full_tile(src, kb):
        for j in range(ns):
            update(src, j, kb, tq, False)

    def diagonal_tile(src):
        for j in range(ns):
            update(src, j, qi, (j + 1) * ATT_SUB, True)
        for j in range(ns):
            acc = acc_sc[j]
            o_t = acc[0:LANES, :] / acc[LANES:LANES + 1, :]
            o_ref[j * ATT_SUB:(j + 1) * ATT_SUB, :] = o_t.T.astype(o_ref.dtype)

    logits(s_a, 0)

    def pair(u, carry):
        t = 2 * u
        logits(s_b, t + 1)
        full_tile(s_a, t)
        logits(s_a, t + 2)
        full_tile(s_b, t + 1)
        return carry

    lax.fori_loop(0, qi // 2, pair, 0)

    @pl.when(qi % 2 == 1)
    def _():
        logits(s_b, qi)
        full_tile(s_a, qi - 1)
        diagonal_tile(s_b)

    @pl.when(qi % 2 == 0)
    def _():
        diagonal_tile(s_a)


def _attention(q, k, vt, f_t=None):
    hh, s, dk = q.shape
    _, nt, vrows, tq = vt.shape
    use_f = f_t is not None
    in_specs = [pl.BlockSpec((1, tq, dk), lambda h, i: (h, i, 0)),
                pl.BlockSpec((1, s, dk), lambda h, i: (h, 0, 0)),
                pl.BlockSpec((1, nt, vrows, tq), lambda h, i: (h, 0, 0, 0))]
    args = [q, k, vt]
    if use_f:
        in_specs.append(pl.BlockSpec((FOX_HEADS, tq), lambda h, i: (0, i)))
        args.append(f_t)
    ns = tq // ATT_SUB
    return pl.pallas_call(
        functools.partial(_attn_kernel, tq=tq, use_f=use_f),
        grid=(hh, s // tq),
        in_specs=in_specs,
        out_specs=pl.BlockSpec((tq, LANES), lambda h, i: (i, h)),
        out_shape=jax.ShapeDtypeStruct((s, hh * LANES), BF16),
        scratch_shapes=[pltpu.VMEM((ns, 1, ATT_SUB), F32), pltpu.VMEM((ns, vrows, ATT_SUB), F32),
                        pltpu.VMEM((tq, tq), F32), pltpu.VMEM((tq, tq), F32)],
        compiler_params=_cparams(("parallel", "arbitrary")),
        name="fox_attention" if use_f else "mla_attention",
    )(*args)


def _gla_kernel(q_ref, k_ref, v_ref, r_ref, misc_ref, wa_ref, ba_ref, go_ref, o_ref, st_ref, la_ref):
    tb = q_ref.shape[0]
    nchunk = tb // GLA_CHUNK
    nsub = GLA_CHUNK // GLA_SUB
    C = GLA_CHUNK

    @pl.when(pl.program_id(1) == 0)
    def _():
        st_ref[...] = jnp.zeros(st_ref.shape, F32)

    m1, m2, _ = _split3(misc_ref[...])
    w1, w2, _ = _split3(wa_ref[...])
    z = (jnp.dot(m1, w1, preferred_element_type=F32) + jnp.dot(m1, w2, preferred_element_type=F32)
         + jnp.dot(m2, w1, preferred_element_type=F32)) + ba_ref[...]
    la_ref[...] = _log_sigmoid(z) * (1.0 / GLA_TAU)

    r_i = lax.broadcasted_iota(jnp.int32, (C, C), 0)
    c_i = lax.broadcasted_iota(jnp.int32, (C, C), 1)
    tri = jnp.where(c_i <= r_i, 1.0, 0.0).astype(BF16)
    sub_row = lax.broadcasted_iota(jnp.int32, (GLA_SUB, GLA_DK), 0)
    col_sub = lax.broadcasted_iota(jnp.int32, (GLA_SUB, C), 1)
    scale = GLA_DK ** -0.5
    g_o = go_ref[...]

    def chunk(ci, carry):
        rows = pl.ds(pl.multiple_of(ci * C, C), C)
        l1, l2, l3 = _split3(la_ref[rows, :])
        b = (jnp.dot(tri, l1, preferred_element_type=F32) + jnp.dot(tri, l2, preferred_element_type=F32)
             + jnp.dot(tri, l3, preferred_element_type=F32))
        qc = q_ref[rows, :].astype(F32) * scale
        kc = k_ref[rows, :].astype(F32)
        vc = v_ref[rows, :]
        vcf = vc.astype(F32)
        st = st_ref[...]

        o = lax.dot_general((qc * jnp.exp(b)).astype(BF16), st.astype(BF16),
                            (((1,), (1,)), ((), ())), preferred_element_type=F32)

        a_rows = [jnp.zeros((GLA_SUB, C), F32)]
        for si in range(1, nsub):
            lo = si * GLA_SUB
            ref_row = b[lo - 1:lo, :]
            q_i = qc[lo:lo + GLA_SUB] * jnp.exp(b[lo:lo + GLA_SUB] - ref_row)
            k_j = kc * jnp.exp(jnp.minimum(ref_row - b, 0.0))
            a = lax.dot_general(q_i.astype(BF16), k_j.astype(BF16), (((1,), (1,)), ((), ())),
                                preferred_element_type=F32)
            a_rows.append(jnp.where(col_sub < lo, a, 0.0))
        a_off = jnp.concatenate(a_rows, axis=0)
        o = o + jnp.dot(a_off.astype(BF16), vc, preferred_element_type=F32)

        o_diag = []
        for si in range(nsub):
            lo = si * GLA_SUB
            qs, ks, bs, vs = qc[lo:lo + GLA_SUB], kc[lo:lo + GLA_SUB], b[lo:lo + GLA_SUB], vcf[lo:lo + GLA_SUB]
            od = jnp.zeros((GLA_SUB, GLA_DV), F32)
            for j in range(GLA_SUB):
                e = jnp.exp(jnp.where(sub_row >= j, bs - bs[j:j + 1, :], NEG_INF))
                w = jnp.sum(qs * (ks[j:j + 1, :] * e), axis=-1, keepdims=True)
                od = od + w * vs[j:j + 1, :]
            o_diag.append(od)
        o = o + jnp.concatenate(o_diag, axis=0)

        b_last = b[C - 1:C, :]
        ke = (kc * jnp.exp(b_last - b)).astype(BF16)
        st_ref[...] = st * jnp.exp(b_last) + lax.dot_general(
            vc, ke, (((0,), (0,)), ((), ())), preferred_element_type=F32)

        on = o * lax.rsqrt(jnp.mean(o * o, axis=-1, keepdims=True) + EPS) * g_o
        r = r_ref[rows, :].astype(F32)
        o_ref[rows, :] = (on * (r * jax.nn.sigmoid(r))).astype(o_ref.dtype)
        return carry

    lax.fori_loop(0, nchunk, chunk, 0)


def _gla(pb, pa, wa_p, b_a, g_o, tb=GLA_BLOCK):
    s = pb.shape[0]
    return pl.pallas_call(
        _gla_kernel,
        grid=(GLA_HEADS, s // tb),
        in_specs=[pl.BlockSpec((tb, GLA_DK), lambda h, i: (i, B_GQ // GLA_DK + h)),
                  pl.BlockSpec((tb, GLA_DK), lambda h, i: (i, B_GK // GLA_DK + h)),
                  pl.BlockSpec((tb, GLA_DV), lambda h, i: (i, B_GV // GLA_DV + h)),
                  pl.BlockSpec((tb, GLA_DV), lambda h, i: (i, B_GR // GLA_DV + h)),
                  pl.BlockSpec((tb, LANES), lambda h, i: (i, A_MISC // LANES)),
                  pl.BlockSpec((LANES, GLA_DK), lambda h, i: (0, h)),
                  pl.BlockSpec((1, GLA_DK), lambda h, i: (0, h)),
                  pl.BlockSpec((1, GLA_DV), lambda h, i: (0, 0))],
        out_specs=pl.BlockSpec((tb, GLA_DV), lambda h, i: (i, h)),
        out_shape=jax.ShapeDtypeStruct((s, GLA_HEADS * GLA_DV), BF16),
        scratch_shapes=[pltpu.VMEM((GLA_DV, GLA_DK), F32), pltpu.VMEM((tb, GLA_DK), F32)],
        compiler_params=_cparams(("parallel", "arbitrary")),
        name="gla",
    )(pb, pb, pb, pb, pa, wa_p, b_a, g_o)


def _group_w_in(w):
    d = w.shape[0]
    off = [0]
    for sz in (512, 512, 64, 512, 512, 1024, 16, 1024, 1024, 1024, 1024, 8, N_BRANCH * D_MODEL):
        off.append(off[-1] + sz)
    cq, ckv, kr, gq, gk, gv, ga, gr, fq, fk, fv, fl, gates = [w[:, off[i]:off[i + 1]] for i in range(13)]
    z = lambda n: jnp.zeros((d, n), w.dtype)
    wa = jnp.concatenate([cq, ckv, kr, z(LANES - MLA_ROPE), fl, ga, z(LANES - FOX_HEADS - GLA_GATE_RANK)], axis=1)
    wb = jnp.concatenate([gq, gk, gv, gr, fq, fk, fv], axis=1)
    return wa.astype(BF16), wb.astype(BF16), gates.astype(BF16)


def _pad_lanes(v, n):
    return jnp.pad(v, (0, n - v.shape[0])).reshape(1, n)


def _layer(x, tabs, g_mix, w_in, g_cq, w_uq, g_ckv, w_ukv, g_mla_q, g_mla_k, w_a2, b_a, g_gla_o,
           g_fox_q, g_fox_k, b_f, w_branch, w_out, g_ffn, w_gu, w_down):
    wa, wb, wg = _group_w_in(w_in)
    h = _rmsnorm(x, g_mix)
    pa = _matmul(h, wa, F32, 1024, A_WIDTH, name="in_proj_a")
    pb = _matmul(h, wb, BF16, 1024, 512, name="in_proj_b")
    gsig = _matmul(h, wg, BF16, 1024, 512, act="sigmoid", name="in_proj_gates")

    wuq_p = jnp.pad(w_uq.reshape(MLA_RANK, MLA_HEADS, MLA_QK),
                    ((0, 0), (0, 0), (0, HEAD_PAD - MLA_QK))).reshape(MLA_RANK, MLA_HEADS * HEAD_PAD).astype(BF16)
    q_a, k_a, vt_a = _mla_prep(pa, wuq_p, w_ukv.astype(BF16), g_cq.reshape(1, -1), g_ckv.reshape(1, -1),
                               _pad_lanes(g_mla_q, HEAD_PAD), g_mla_k[:MLA_NOPE].reshape(1, -1),
                               _pad_lanes(g_mla_k[MLA_NOPE:], LANES), tabs)
    o_a = _attention(q_a, k_a, vt_a)

    wa_p = jnp.zeros((LANES, GLA_HEADS * GLA_DK), F32).at[MISC_GA:MISC_GA + GLA_GATE_RANK].set(w_a2)
    o_b = _gla(pb, pa, wa_p, b_a.reshape(1, -1), g_gla_o.reshape(1, -1))

    f_cum, f_cum_t = _fox_cumsum(pa, _pad_lanes(b_f, LANES))
    q_c, k_c, vt_c = _fox_prep(pb, f_cum, g_fox_q.reshape(1, -1), g_fox_k.reshape(1, -1))
    o_c = _attention(q_c, k_c, vt_c, f_cum_t)

    merged = _merge(o_a, o_b, o_c, w_branch.reshape(N_BRANCH * BRANCH_WIDTH, D_MODEL).astype(BF16), gsig, 512, 512)
    x1 = _matmul_residual(merged, w_out.astype(BF16), x, 512, 512, name="out_proj")

    h2 = _rmsnorm(x1, g_ffn)
    act = _swiglu_up(h2, w_gu.astype(BF16), 1024, 512)
    return _matmul_residual(act, w_down.astype(BF16), x1, 512, 512, name="ffn_down")


def kernel(x, positions, g_mix, w_in, g_cq, w_uq, g_ckv, w_ukv, g_mla_q, g_mla_k, w_a2, b_a, g_gla_o,
           g_fox_q, g_fox_k, b_f, w_branch, w_out, g_ffn, w_gu, w_down):
    bsz, s, d = x.shape
    assert bsz == 1 and d == D_MODEL and s % ATT_TILE == 0
    half = MLA_ROPE // 2
    inv = ROPE_THETA ** (-jnp.arange(half, dtype=F32) / half)
    inv128 = jnp.concatenate([inv, inv, jnp.zeros((LANES - MLA_ROPE,), F32)]).reshape(1, LANES)
    tabs = _rope_tables(positions.reshape(s, 1), inv128)
    hcur = x.reshape(s, d)
    for l in range(w_in.shape[0]):
        hcur = _layer(hcur, tabs, g_mix[l], w_in[l], g_cq[l], w_uq[l], g_ckv[l], w_ukv[l], g_mla_q[l], g_mla_k[l],
                      w_a2[l], b_a[l], g_gla_o[l], g_fox_q[l], g_fox_k[l], b_f[l], w_branch[l], w_out[l],
                      g_ffn[l], w_gu[l], w_down[l])
    return hcur.reshape(bsz, s, d)
```

```python
import functools

import jax
import jax.numpy as jnp
from jax import lax
from jax.experimental import pallas as pl
from jax.experimental.pallas import tpu as pltpu

F32 = jnp.float32
BF16 = jnp.bfloat16

D_MODEL = 2048
BRANCH_WIDTH = D_MODEL // 2
N_BRANCH = 3
MLA_NOPE = 128
MLA_ROPE = 64
MLA_V = 128
MLA_HEADS = BRANCH_WIDTH // MLA_V
MLA_QK = MLA_NOPE + MLA_ROPE
MLA_RANK = 512
ROPE_THETA = 10000.0
GLA_HEADS = 4
GLA_DV = BRANCH_WIDTH // GLA_HEADS
GLA_DK = GLA_DV // 2
GLA_GATE_RANK = 16
GLA_TAU = 16.0
FOX_DH = 128
FOX_HEADS = BRANCH_WIDTH // FOX_DH
FFN_HIDDEN = ((8 * D_MODEL + 3 * 256 - 1) // (3 * 256)) * 256
EPS = 1e-6
NEG_INF = -1e30

LANES = 128
BF16_SUBLANES = 16
HEAD_PAD = 256
VMEM_LIMIT = 48 * 1024 * 1024
VMEM_LIMIT_BIG = 56 * 1024 * 1024

A_CQ, A_CKV, A_KR, A_MISC = 0, 512, 1024, 1152
A_WIDTH = 1280
MISC_FL = 0
MISC_GA = FOX_HEADS
B_GQ, B_GK, B_GV, B_GR, B_FQ, B_FK, B_FV = 0, 512, 1024, 2048, 3072, 4096, 5120
B_WIDTH = 6144

ATT_TILE = 1024
ATT_SUB = 256
ATT_VROWS = LANES + BF16_SUBLANES
LOG2E = 1.4426950408889634

GLA_CHUNK = 64
GLA_SUB = 16
GLA_BLOCK = 512


def _aligned(start, m):
    return start if isinstance(start, int) else pl.multiple_of(start, m)


def _cparams(sem):
    return pltpu.CompilerParams(dimension_semantics=sem, vmem_limit_bytes=VMEM_LIMIT)


def _split3(x):
    x1 = x.astype(BF16)
    r1 = x - x1.astype(F32)
    x2 = r1.astype(BF16)
    x3 = (r1 - x2.astype(F32)).astype(BF16)
    return x1, x2, x3


def _log_sigmoid(z):
    return -(jnp.maximum(-z, 0.0) + jnp.log1p(jnp.exp(-jnp.abs(z))))


def _value_rows_t(v_f32):
    rows = v_f32.shape[0]
    r = lax.broadcasted_iota(jnp.int32, (BF16_SUBLANES, rows), 0)
    return v_f32.T.astype(BF16), jnp.where(r == 0, 1.0, 0.0).astype(BF16)


def _rmsnorm_kernel(x_ref, g_ref, o_ref):
    x = x_ref[...]
    ms = jnp.mean(x * x, axis=-1, keepdims=True)
    o_ref[...] = (x * lax.rsqrt(ms + EPS) * g_ref[...]).astype(o_ref.dtype)


def _rmsnorm(x, g, tm=512):
    m, d = x.shape
    return pl.pallas_call(
        _rmsnorm_kernel,
        grid=(m // tm,),
        in_specs=[pl.BlockSpec((tm, d), lambda i: (i, 0)),
                  pl.BlockSpec((1, d), lambda i: (0, 0))],
        out_specs=pl.BlockSpec((tm, d), lambda i: (i, 0)),
        out_shape=jax.ShapeDtypeStruct((m, d), BF16),
        compiler_params=_cparams(("parallel",)),
        name="rmsnorm",
    )(x, g.reshape(1, d))


def _mm_kernel(a_ref, w_ref, o_ref, *, act):
    y = jnp.dot(a_ref[...], w_ref[...], preferred_element_type=F32)
    if act == "sigmoid":
        y = jax.nn.sigmoid(y)
    o_ref[...] = y.astype(o_ref.dtype)


def _matmul(a, w, out_dtype, tm, tn, act=None, name="matmul"):
    m, k = a.shape
    n = w.shape[1]
    return pl.pallas_call(
        functools.partial(_mm_kernel, act=act),
        grid=(m // tm, n // tn),
        in_specs=[pl.BlockSpec((tm, k), lambda i, j: (i, 0)),
                  pl.BlockSpec((k, tn), lambda i, j: (0, j))],
        out_specs=pl.BlockSpec((tm, tn), lambda i, j: (i, j)),
        out_shape=jax.ShapeDtypeStruct((m, n), out_dtype),
        compiler_params=_cparams(("parallel", "arbitrary")),
        name=name,
    )(a, w)


def _mm_res_kernel(*refs, with_norm):
    if with_norm:
        a_ref, w_ref, r_ref, g_ref, o_ref, h_ref = refs
    else:
        a_ref, w_ref, r_ref, o_ref = refs
    y = r_ref[...] + jnp.dot(a_ref[...], w_ref[...], preferred_element_type=F32)
    o_ref[...] = y
    if with_norm:
        ms = jnp.mean(y * y, axis=-1, keepdims=True)
        h_ref[...] = (y * lax.rsqrt(ms + EPS) * g_ref[...]).astype(h_ref.dtype)


def _matmul_residual(a, w, res, tm, name, g_next=None):
    m, k = a.shape
    n = w.shape[1]
    with_norm = g_next is not None
    row = pl.BlockSpec((tm, n), lambda i: (i, 0))
    in_specs = [pl.BlockSpec((tm, k), lambda i: (i, 0)),
                pl.BlockSpec((k, n), lambda i: (0, 0), pipeline_mode=pl.Buffered(1)),
                row]
    args = [a, w, res]
    out_specs, out_shape = row, jax.ShapeDtypeStruct((m, n), F32)
    if with_norm:
        in_specs.append(pl.BlockSpec((1, n), lambda i: (0, 0)))
        args.append(g_next.reshape(1, n))
        out_specs, out_shape = [row, row], [out_shape, jax.ShapeDtypeStruct((m, n), BF16)]
    return pl.pallas_call(
        functools.partial(_mm_res_kernel, with_norm=with_norm),
        grid=(m // tm,),
        in_specs=in_specs,
        out_specs=out_specs,
        out_shape=out_shape,
        compiler_params=pltpu.CompilerParams(dimension_semantics=("parallel",), vmem_limit_bytes=VMEM_LIMIT_BIG),
        name=name,
    )(*args)


def _swiglu_kernel(a_ref, wg_ref, wu_ref, o_ref):
    a = a_ref[...]
    g = jnp.dot(a, wg_ref[...], preferred_element_type=F32)
    u = jnp.dot(a, wu_ref[...], preferred_element_type=F32)
    o_ref[...] = (g * jax.nn.sigmoid(g) * u).astype(o_ref.dtype)


def _swiglu_up(a, w_gu, tm, tn):
    m, k = a.shape
    hidden = w_gu.shape[1] // 2
    nj = hidden // tn
    return pl.pallas_call(
        _swiglu_kernel,
        grid=(m // tm, nj),
        in_specs=[pl.BlockSpec((tm, k), lambda i, j: (i, 0)),
                  pl.BlockSpec((k, tn), lambda i, j: (0, j)),
                  pl.BlockSpec((k, tn), lambda i, j: (0, j + nj))],
        out_specs=pl.BlockSpec((tm, tn), lambda i, j: (i, j)),
        out_shape=jax.ShapeDtypeStruct((m, hidden), BF16),
        compiler_params=_cparams(("parallel", "arbitrary")),
        name="ffn_up",
    )(a, w_gu, w_gu)


def _merge_kernel(oa_ref, ob_ref, oc_ref, wa_ref, wb_ref, wc_ref, ga_ref, gb_ref, gc_ref, o_ref):
    acc = jnp.dot(oa_ref[...], wa_ref[...], preferred_element_type=F32) * ga_ref[...].astype(F32)
    acc += jnp.dot(ob_ref[...], wb_ref[...], preferred_element_type=F32) * gb_ref[...].astype(F32)
    acc += jnp.dot(oc_ref[...], wc_ref[...], preferred_element_type=F32) * gc_ref[...].astype(F32)
    o_ref[...] = acc.astype(o_ref.dtype)


def _merge(o_a, o_b, o_c, w_branch2d, gsig, tm, tn):
    m, bw = o_a.shape
    d = w_branch2d.shape[1]
    nj = d // tn
    o_spec = pl.BlockSpec((tm, bw), lambda i, j: (i, 0))

    def w_spec(n):
        return pl.BlockSpec((bw, tn), lambda i, j: (n, j))

    def g_spec(n):
        return pl.BlockSpec((tm, tn), lambda i, j: (i, n * nj + j))

    return pl.pallas_call(
        _merge_kernel,
        grid=(m // tm, nj),
        in_specs=[o_spec, o_spec, o_spec, w_spec(0), w_spec(1), w_spec(2), g_spec(0), g_spec(1), g_spec(2)],
        out_specs=pl.BlockSpec((tm, tn), lambda i, j: (i, j)),
        out_shape=jax.ShapeDtypeStruct((m, d), BF16),
        compiler_params=_cparams(("parallel", "arbitrary")),
        name="branch_merge",
    )(o_a, o_b, o_c, w_branch2d, w_branch2d, w_branch2d, gsig, gsig, gsig)


def _rope_table_kernel(pos_ref, inv_ref, c_ref, s1_ref, s2_ref):
    ang = pos_ref[...].astype(F32) * inv_ref[...]
    c = jnp.cos(ang)
    s = jnp.sin(ang)
    lane = lax.broadcasted_iota(jnp.int32, ang.shape, 1)
    half = MLA_ROPE // 2
    c_ref[...] = jnp.where(lane < MLA_ROPE, c, 0.0)
    s1_ref[...] = jnp.where((lane >= half) & (lane < MLA_ROPE), s, 0.0)
    s2_ref[...] = jnp.where(lane < half, -s, 0.0)


def _rope_tables(pos_col, inv128, tm=512):
    s = pos_col.shape[0]
    spec = pl.BlockSpec((tm, LANES), lambda i: (i, 0))
    shp = jax.ShapeDtypeStruct((s, LANES), F32)
    return pl.pallas_call(
        _rope_table_kernel,
        grid=(s // tm,),
        in_specs=[pl.BlockSpec((tm, 1), lambda i: (i, 0)),
                  pl.BlockSpec((1, LANES), lambda i: (0, 0))],
        out_specs=[spec, spec, spec],
        out_shape=[shp, shp, shp],
        compiler_params=_cparams(("parallel",)),
        name="rope_tables",
    )(pos_col, inv128)


def _rope128(x, c, s1, s2):
    half = MLA_ROPE // 2
    return x * c + pltpu.roll(x, half, 1) * s1 + pltpu.roll(x, LANES - half, 1) * s2


def _mla_prep_kernel(cq_ref, ckv_ref, kr_ref, wuq_ref, wukv_ref, gcq_ref, gckv_ref, gq_ref, gk_ref, gkr_ref,
                     c_ref, s1_ref, s2_ref, q_out, k_out, vt_out):
    def norm(x, g):
        return x * lax.rsqrt(jnp.mean(x * x, axis=-1, keepdims=True) + EPS) * g

    cqn = norm(cq_ref[...], gcq_ref[...]).astype(BF16)
    ckvn = norm(ckv_ref[...], gckv_ref[...]).astype(BF16)
    qa = jnp.dot(cqn, wuq_ref[...], preferred_element_type=F32)
    kva = jnp.dot(ckvn, wukv_ref[...], preferred_element_type=F32)
    krp = kr_ref[...]
    kr_ss = jnp.sum(krp * krp, axis=-1, keepdims=True)
    c, s1, s2 = c_ref[...], s1_ref[...], s2_ref[...]
    gq, gk, gkr = gq_ref[...], gk_ref[...], gkr_ref[...]
    scale = MLA_QK ** -0.5 * LOG2E
    for h in range(MLA_HEADS):
        qh = qa[:, h * HEAD_PAD:(h + 1) * HEAD_PAD]
        rq = lax.rsqrt(jnp.sum(qh * qh, axis=-1, keepdims=True) * (1.0 / MLA_QK) + EPS)
        qn = qh * rq * gq
        q_out[h, :, 0:LANES] = (qn[:, 0:LANES] * scale).astype(BF16)
        q_out[h, :, LANES:HEAD_PAD] = (_rope128(qn[:, LANES:HEAD_PAD], c, s1, s2) * scale).astype(BF16)
        kn = kva[:, h * HEAD_PAD:h * HEAD_PAD + MLA_NOPE]
        rk = lax.rsqrt((jnp.sum(kn * kn, axis=-1, keepdims=True) + kr_ss) * (1.0 / MLA_QK) + EPS)
        k_out[h, :, 0:LANES] = (kn * rk * gk).astype(BF16)
        k_out[h, :, LANES:HEAD_PAD] = _rope128(krp * rk * gkr, c, s1, s2).astype(BF16)
        v_t, ones_rows = _value_rows_t(kva[:, h * HEAD_PAD + MLA_NOPE:(h + 1) * HEAD_PAD])
        vt_out[h, 0, 0:LANES, :] = v_t
        vt_out[h, 0, LANES:ATT_VROWS, :] = ones_rows


def _vt_spec(heads, tm):
    per_tile = ATT_TILE // tm
    return pl.BlockSpec((heads, 1, ATT_VROWS, tm), lambda i: (0, i // per_tile, 0, i % per_tile))


def _mla_prep(pa, wuq_p, wukv, g_cq, g_ckv, g_q, g_k, g_kr, tabs, tm=256):
    s = pa.shape[0]
    h = MLA_HEADS
    row = lambda w: pl.BlockSpec((1, w), lambda i: (0, 0))
    tab = pl.BlockSpec((tm, LANES), lambda i: (i, 0))
    return pl.pallas_call(
        _mla_prep_kernel,
        grid=(s // tm,),
        in_specs=[pl.BlockSpec((tm, MLA_RANK), lambda i: (i, A_CQ // MLA_RANK)),
                  pl.BlockSpec((tm, MLA_RANK), lambda i: (i, A_CKV // MLA_RANK)),
                  pl.BlockSpec((tm, LANES), lambda i: (i, A_KR // LANES)),
                  pl.BlockSpec(wuq_p.shape, lambda i: (0, 0)),
                  pl.BlockSpec(wukv.shape, lambda i: (0, 0)),
                  row(MLA_RANK), row(MLA_RANK), row(HEAD_PAD), row(LANES), row(LANES),
                  tab, tab, tab],
        out_specs=[pl.BlockSpec((h, tm, HEAD_PAD), lambda i: (0, i, 0)),
                   pl.BlockSpec((h, tm, HEAD_PAD), lambda i: (0, i, 0)),
                   _vt_spec(h, tm)],
        out_shape=[jax.ShapeDtypeStruct((h, s, HEAD_PAD), BF16),
                   jax.ShapeDtypeStruct((h, s, HEAD_PAD), BF16),
                   jax.ShapeDtypeStruct((h, s // ATT_TILE, ATT_VROWS, ATT_TILE), BF16)],
        compiler_params=_cparams(("parallel",)),
        name="mla_prep",
    )(pa, pa, pa, wuq_p, wukv, g_cq, g_ckv, g_q, g_k, g_kr, *tabs)


def _fox_cumsum_kernel(misc_ref, bf_ref, f_ref, ft_ref, carry_ref):
    blk = misc_ref.shape[0]

    @pl.when(pl.program_id(0) == 0)
    def _():
        carry_ref[...] = jnp.zeros(carry_ref.shape, F32)

    r = lax.broadcasted_iota(jnp.int32, (blk, blk), 0)
    c = lax.broadcasted_iota(jnp.int32, (blk, blk), 1)
    tri = jnp.where(c <= r, 1.0, 0.0).astype(BF16)
    x1, x2, x3 = _split3(_log_sigmoid(misc_ref[...] + bf_ref[...]))
    cs = (jnp.dot(tri, x1, preferred_element_type=F32)
          + jnp.dot(tri, x2, preferred_element_type=F32)
          + jnp.dot(tri, x3, preferred_element_type=F32)) + carry_ref[...]
    f_ref[...] = cs
    ft_ref[...] = cs.T
    carry_ref[...] = cs[blk - 1:blk, :]


def _fox_cumsum(pa, bf128, blk=512):
    s = pa.shape[0]
    return pl.pallas_call(
        _fox_cumsum_kernel,
        grid=(s // blk,),
        in_specs=[pl.BlockSpec((blk, LANES), lambda i: (i, A_MISC // LANES)),
                  pl.BlockSpec((1, LANES), lambda i: (0, 0))],
        out_specs=[pl.BlockSpec((blk, LANES), lambda i: (i, 0)),
                   pl.BlockSpec((LANES, blk), lambda i: (0, i))],
        out_shape=[jax.ShapeDtypeStruct((s, LANES), F32), jax.ShapeDtypeStruct((LANES, s), F32)],
        scratch_shapes=[pltpu.VMEM((1, LANES), F32)],
        compiler_params=_cparams(("arbitrary",)),
        name="fox_cumsum",
    )(pa, bf128)


def _fox_prep_kernel(fq_ref, fk_ref, fv_ref, f_ref, gq_ref, gk_ref, q_out, k_out, vt_out):
    tm = fq_ref.shape[0]
    lane = lax.broadcasted_iota(jnp.int32, (tm, LANES), 1)
    ones_tail = jnp.where(lane < 3, 1.0, 0.0).astype(BF16)
    f_all = f_ref[...]
    scale = FOX_DH ** -0.5 * LOG2E
    gq, gk = gq_ref[...], gk_ref[...]
    for h in range(FOX_HEADS):
        sl = slice(h * FOX_DH, (h + 1) * FOX_DH)
        q = fq_ref[:, sl].astype(F32)
        k = fk_ref[:, sl].astype(F32)
        qn = q * lax.rsqrt(jnp.mean(q * q, axis=-1, keepdims=True) + EPS) * gq
        kn = k * lax.rsqrt(jnp.mean(k * k, axis=-1, keepdims=True) + EPS) * gk
        q_out[h, :, 0:LANES] = (qn * scale).astype(BF16)
        q_out[h, :, LANES:HEAD_PAD] = ones_tail
        k_out[h, :, 0:LANES] = kn.astype(BF16)
        nf = f_all[:, h:h + 1] * (-LOG2E)
        n1, n2, n3 = _split3(nf)
        tail = jnp.where(lane == 0, n1.astype(F32),
                         jnp.where(lane == 1, n2.astype(F32),
                                   jnp.where(lane == 2, n3.astype(F32), 0.0)))
        k_out[h, :, LANES:HEAD_PAD] = tail.astype(BF16)
        v_t, ones_rows = _value_rows_t(fv_ref[:, sl].astype(F32))
        vt_out[h, 0, 0:LANES, :] = v_t
        vt_out[h, 0, LANES:ATT_VROWS, :] = ones_rows


def _fox_prep(pb, f_cum, g_q, g_k, tm=256):
    s = pb.shape[0]
    h = FOX_HEADS
    w = FOX_HEADS * FOX_DH
    row = pl.BlockSpec((1, FOX_DH), lambda i: (0, 0))
    return pl.pallas_call(
        _fox_prep_kernel,
        grid=(s // tm,),
        in_specs=[pl.BlockSpec((tm, w), lambda i: (i, B_FQ // w)),
                  pl.BlockSpec((tm, w), lambda i: (i, B_FK // w)),
                  pl.BlockSpec((tm, w), lambda i: (i, B_FV // w)),
                  pl.BlockSpec((tm, LANES), lambda i: (i, 0)),
                  row, row],
        out_specs=[pl.BlockSpec((h, tm, HEAD_PAD), lambda i: (0, i, 0)),
                   pl.BlockSpec((h, tm, HEAD_PAD), lambda i: (0, i, 0)),
                   _vt_spec(h, tm)],
        out_shape=[jax.ShapeDtypeStruct((h, s, HEAD_PAD), BF16),
                   jax.ShapeDtypeStruct((h, s, HEAD_PAD), BF16),
                   jax.ShapeDtypeStruct((h, s // ATT_TILE, ATT_VROWS, ATT_TILE), BF16)],
        compiler_params=_cparams(("parallel",)),
        name="fox_prep",
    )(pb, pb, pb, f_cum, g_q, g_k)


def _attn_kernel(*refs, tq, use_f):
    if use_f:
        q_ref, k_ref, vt_ref, f_ref, o_ref, m_sc, acc_sc, s_a, s_b = refs
    else:
        q_ref, k_ref, vt_ref, o_ref, m_sc, acc_sc, s_a, s_b = refs
    h = pl.program_id(0)
    qi = pl.program_id(1)
    ns = tq // ATT_SUB
    if use_f:
        ft = [f_ref[pl.ds(h, 1), j * ATT_SUB:(j + 1) * ATT_SUB] * LOG2E for j in range(ns)]
    m_sc[...] = jnp.full(m_sc.shape, NEG_INF, F32)
    acc_sc[...] = jnp.zeros(acc_sc.shape, F32)

    def logits(dst, kb):
        dst[...] = lax.dot_general(k_ref[0, pl.ds(_aligned(kb * tq, tq), tq), :], q_ref[0],
                                   (((1,), (1,)), ((), ())), preferred_element_type=F32)

    def update(src, j, kb, size, mask):
        s = src[0:size, j * ATT_SUB:(j + 1) * ATT_SUB]
        if mask:
            key = lax.broadcasted_iota(jnp.int32, (size, ATT_SUB), 0)
            qry = lax.broadcasted_iota(jnp.int32, (size, ATT_SUB), 1)
            s = jnp.where(key <= qry + j * ATT_SUB, s, NEG_INF)
        m_old = m_sc[j]
        m_cur = jnp.max(s, axis=0, keepdims=True)
        if use_f:
            m_new = jnp.maximum(m_old, m_cur + ft[j])
            shift = m_new - ft[j]
        else:
            m_new = jnp.maximum(m_old, m_cur)
            shift = m_new
        p = jnp.exp2(s - shift).astype(BF16)
        alpha = jnp.exp2(m_old - m_new)
        acc_sc[j] = alpha * acc_sc[j] + jnp.dot(vt_ref[0, kb, :, 0:size], p, preferred_element_type=F32)
        m_sc[j] = m_new

    def full_tile(src, kb):
        for j in range(ns):
            update(src, j, kb, tq, False)

    def diagonal_tile(src):
        for j in range(ns):
            update(src, j, qi, (j + 1) * ATT_SUB, True)
        for j in range(ns):
            acc = acc_sc[j]
            o_t = acc[0:LANES, :] / acc[LANES:LANES + 1, :]
            o_ref[j * ATT_SUB:(j + 1) * ATT_SUB, :] = o_t.T.astype(o_ref.dtype)

    logits(s_a, 0)

    def pair(u, carry):
        t = 2 * u
        logits(s_b, t + 1)
        full_tile(s_a, t)
        logits(s_a, t + 2)
        full_tile(s_b, t + 1)
        return carry

    lax.fori_loop(0, qi // 2, pair, 0)

    @pl.when(qi % 2 == 1)
    def _():
        logits(s_b, qi)
        full_tile(s_a, qi - 1)
        diagonal_tile(s_b)

    @pl.when(qi % 2 == 0)
    def _():
        diagonal_tile(s_a)


def _attention(q, k, vt, f_t=None):
    hh, s, dk = q.shape
    _, nt, vrows, tq = vt.shape
    use_f = f_t is not None
    in_specs = [pl.BlockSpec((1, tq, dk), lambda h, i: (h, i, 0)),
                pl.BlockSpec((1, s, dk), lambda h, i: (h, 0, 0)),
                pl.BlockSpec((1, nt, vrows, tq), lambda h, i: (h, 0, 0, 0))]
    args = [q, k, vt]
    if use_f:
        in_specs.append(pl.BlockSpec((FOX_HEADS, tq), lambda h, i: (0, i)))
        args.append(f_t)
    ns = tq // ATT_SUB
    return pl.pallas_call(
        functools.partial(_attn_kernel, tq=tq, use_f=use_f),
        grid=(hh, s // tq),
        in_specs=in_specs,
        out_specs=pl.BlockSpec((tq, LANES), lambda h, i: (i, h)),
        out_shape=jax.ShapeDtypeStruct((s, hh * LANES), BF16),
        scratch_shapes=[pltpu.VMEM((ns, 1, ATT_SUB), F32), pltpu.VMEM((ns, vrows, ATT_SUB), F32),
                        pltpu.VMEM((tq, tq), F32), pltpu.VMEM((tq, tq), F32)],
        compiler_params=_cparams(("parallel", "arbitrary")),
        name="fox_attention" if use_f else "mla_attention",
    )(*args)


def _gla_kernel(q_ref, k_ref, v_ref, r_ref, misc_ref, wa_ref, ba_ref, go_ref, o_ref, st_ref, la_ref):
    tb = q_ref.shape[0]
    nchunk = tb // GLA_CHUNK
    nsub = GLA_CHUNK // GLA_SUB
    C = GLA_CHUNK

    @pl.when(pl.program_id(1) == 0)
    def _():
        st_ref[...] = jnp.zeros(st_ref.shape, F32)

    m1, m2, _ = _split3(misc_ref[...])
    w1, w2, _ = _split3(wa_ref[...])
    z = (jnp.dot(m1, w1, preferred_element_type=F32) + jnp.dot(m1, w2, preferred_element_type=F32)
         + jnp.dot(m2, w1, preferred_element_type=F32)) + ba_ref[...]
    la_ref[...] = _log_sigmoid(z) * (1.0 / GLA_TAU)

    r_i = lax.broadcasted_iota(jnp.int32, (C, C), 0)
    c_i = lax.broadcasted_iota(jnp.int32, (C, C), 1)
    tri = jnp.where(c_i <= r_i, 1.0, 0.0).astype(BF16)
    sub_row = lax.broadcasted_iota(jnp.int32, (GLA_SUB, GLA_DK), 0)
    col_sub = lax.broadcasted_iota(jnp.int32, (GLA_SUB, C), 1)
    scale = GLA_DK ** -0.5
    g_o = go_ref[...]

    def chunk(ci, carry):
        rows = pl.ds(pl.multiple_of(ci * C, C), C)
        l1, l2, l3 = _split3(la_ref[rows, :])
        b = (jnp.dot(tri, l1, preferred_element_type=F32) + jnp.dot(tri, l2, preferred_element_type=F32)
             + jnp.dot(tri, l3, preferred_element_type=F32))
        qc = q_ref[rows, :].astype(F32) * scale
        kc = k_ref[rows, :].astype(F32)
        vc = v_ref[rows, :]
        vcf = vc.astype(F32)
        st = st_ref[...]

        o = lax.dot_general((qc * jnp.exp(b)).astype(BF16), st.astype(BF16),
                            (((1,), (1,)), ((), ())), preferred_element_type=F32)

        a_rows = [jnp.zeros((GLA_SUB, C), F32)]
        for si in range(1, nsub):
            lo = si * GLA_SUB
            ref_row = b[lo - 1:lo, :]
            q_i = qc[lo:lo + GLA_SUB] * jnp.exp(b[lo:lo + GLA_SUB] - ref_row)
            k_j = kc * jnp.exp(jnp.minimum(ref_row - b, 0.0))
            a = lax.dot_general(q_i.astype(BF16), k_j.astype(BF16), (((1,), (1,)), ((), ())),
                                preferred_element_type=F32)
            a_rows.append(jnp.where(col_sub < lo, a, 0.0))
        a_off = jnp.concatenate(a_rows, axis=0)
        o = o + jnp.dot(a_off.astype(BF16), vc, preferred_element_type=F32)

        o_diag = []
        for si in range(nsub):
            lo = si * GLA_SUB
            qs, ks, bs, vs = qc[lo:lo + GLA_SUB], kc[lo:lo + GLA_SUB], b[lo:lo + GLA_SUB], vcf[lo:lo + GLA_SUB]
            od = jnp.zeros((GLA_SUB, GLA_DV), F32)
            for j in range(GLA_SUB):
                e = jnp.exp(jnp.where(sub_row >= j, bs - bs[j:j + 1, :], NEG_INF))
                w = jnp.sum(qs * (ks[j:j + 1, :] * e), axis=-1, keepdims=True)
                od = od + w * vs[j:j + 1, :]
            o_diag.append(od)
        o = o + jnp.concatenate(o_diag, axis=0)

        b_last = b[C - 1:C, :]
        ke = (kc * jnp.exp(b_last - b)).astype(BF16)
        st_ref[...] = st * jnp.exp(b_last) + lax.dot_general(
            vc, ke, (((0,), (0,)), ((), ())), preferred_element_type=F32)

        on = o * lax.rsqrt(jnp.mean(o * o, axis=-1, keepdims=True) + EPS) * g_o
        r = r_ref[rows, :].astype(F32)
        o_ref[rows, :] = (on * (r * jax.nn.sigmoid(r))).astype(o_ref.dtype)
        return carry

    lax.fori_loop(0, nchunk, chunk, 0, unroll=4)


def _gla(pb, pa, wa_p, b_a, g_o, tb=GLA_BLOCK):
    s = pb.shape[0]
    return pl.pallas_call(
        _gla_kernel,
        grid=(GLA_HEADS, s // tb),
        in_specs=[pl.BlockSpec((tb, GLA_DK), lambda h, i: (i, B_GQ // GLA_DK + h)),
                  pl.BlockSpec((tb, GLA_DK), lambda h, i: (i, B_GK // GLA_DK + h)),
                  pl.BlockSpec((tb, GLA_DV), lambda h, i: (i, B_GV // GLA_DV + h)),
                  pl.BlockSpec((tb, GLA_DV), lambda h, i: (i, B_GR // GLA_DV + h)),
                  pl.BlockSpec((tb, LANES), lambda h, i: (i, A_MISC // LANES)),
                  pl.BlockSpec((LANES, GLA_DK), lambda h, i: (0, h)),
                  pl.BlockSpec((1, GLA_DK), lambda h, i: (0, h)),
                  pl.BlockSpec((1, GLA_DV), lambda h, i: (0, 0))],
        out_specs=pl.BlockSpec((tb, GLA_DV), lambda h, i: (i, h)),
        out_shape=jax.ShapeDtypeStruct((s, GLA_HEADS * GLA_DV), BF16),
        scratch_shapes=[pltpu.VMEM((GLA_DV, GLA_DK), F32), pltpu.VMEM((tb, GLA_DK), F32)],
        compiler_params=_cparams(("parallel", "arbitrary")),
        name="gla",
    )(pb, pb, pb, pb, pa, wa_p, b_a, g_o)


def _group_w_in(w):
    d = w.shape[0]
    off = [0]
    for sz in (512, 512, 64, 512, 512, 1024, 16, 1024, 1024, 1024, 1024, 8, N_BRANCH * D_MODEL):
        off.append(off[-1] + sz)
    cq, ckv, kr, gq, gk, gv, ga, gr, fq, fk, fv, fl, gates = [w[:, off[i]:off[i + 1]] for i in range(13)]
    z = lambda n: jnp.zeros((d, n), w.dtype)
    wa = jnp.concatenate([cq, ckv, kr, z(LANES - MLA_ROPE), fl, ga, z(LANES - FOX_HEADS - GLA_GATE_RANK)], axis=1)
    wb = jnp.concatenate([gq, gk, gv, gr, fq, fk, fv], axis=1)
    return wa.astype(BF16), wb.astype(BF16), gates.astype(BF16)


def _pad_lanes(v, n):
    return jnp.pad(v, (0, n - v.shape[0])).reshape(1, n)


def _layer(x, h, tabs, g_next, w_in, g_cq, w_uq, g_ckv, w_ukv, g_mla_q, g_mla_k, w_a2, b_a, g_gla_o,
           g_fox_q, g_fox_k, b_f, w_branch, w_out, g_ffn, w_gu, w_down):
    wa, wb, wg = _group_w_in(w_in)
    pa = _matmul(h, wa, F32, 1024, A_WIDTH, name="in_proj_a")
    pb = _matmul(h, wb, BF16, 1024, 1024, name="in_proj_b")
    gsig = _matmul(h, wg, BF16, 1024, 1024, act="sigmoid", name="in_proj_gates")

    wuq_p = jnp.pad(w_uq.reshape(MLA_RANK, MLA_HEADS, MLA_QK),
                    ((0, 0), (0, 0), (0, HEAD_PAD - MLA_QK))).reshape(MLA_RANK, MLA_HEADS * HEAD_PAD).astype(BF16)
    q_a, k_a, vt_a = _mla_prep(pa, wuq_p, w_ukv.astype(BF16), g_cq.reshape(1, -1), g_ckv.reshape(1, -1),
                               _pad_lanes(g_mla_q, HEAD_PAD), g_mla_k[:MLA_NOPE].reshape(1, -1),
                               _pad_lanes(g_mla_k[MLA_NOPE:], LANES), tabs)
    o_a = _attention(q_a, k_a, vt_a)

    wa_p = jnp.zeros((LANES, GLA_HEADS * GLA_DK), F32).at[MISC_GA:MISC_GA + GLA_GATE_RANK].set(w_a2)
    o_b = _gla(pb, pa, wa_p, b_a.reshape(1, -1), g_gla_o.reshape(1, -1))

    f_cum, f_cum_t = _fox_cumsum(pa, _pad_lanes(b_f, LANES))
    q_c, k_c, vt_c = _fox_prep(pb, f_cum, g_fox_q.reshape(1, -1), g_fox_k.reshape(1, -1))
    o_c = _attention(q_c, k_c, vt_c, f_cum_t)

    merged = _merge(o_a, o_b, o_c, w_branch.reshape(N_BRANCH * BRANCH_WIDTH, D_MODEL).astype(BF16), gsig, 1024, 512)
    x1, h2 = _matmul_residual(merged, w_out.astype(BF16), x, 512, "out_proj", g_next=g_ffn)
    act = _swiglu_up(h2, w_gu.astype(BF16), 1024, 512)
    if g_next is None:
        return _matmul_residual(act, w_down.astype(BF16), x1, 256, "ffn_down"), None
    return _matmul_residual(act, w_down.astype(BF16), x1, 256, "ffn_down", g_next=g_next)


def kernel(x, positions, g_mix, w_in, g_cq, w_uq, g_ckv, w_ukv, g_mla_q, g_mla_k, w_a2, b_a, g_gla_o,
           g_fox_q, g_fox_k, b_f, w_branch, w_out, g_ffn, w_gu, w_down):
    bsz, s, d = x.shape
    assert bsz == 1 and d == D_MODEL and s % ATT_TILE == 0
    half = MLA_ROPE // 2
    inv = ROPE_THETA ** (-jnp.arange(half, dtype=F32) / half)
    inv128 = jnp.concatenate([inv, inv, jnp.zeros((LANES - MLA_ROPE,), F32)]).reshape(1, LANES)
    tabs = _rope_tables(positions.reshape(s, 1), inv128)
    depth = w_in.shape[0]
    xcur = x.reshape(s, d)
    hcur = _rmsnorm(xcur, g_mix[0])
    for l in range(depth):
        g_next = g_mix[l + 1] if l + 1 < depth else None
        xcur, hcur = _layer(xcur, hcur, tabs, g_next, w_in[l], g_cq[l], w_uq[l], g_ckv[l], w_ukv[l], g_mla_q[l],
                            g_mla_k[l], w_a2[l], b_a[l], g_gla_o[l], g_fox_q[l], g_fox_k[l], b_f[l], w_branch[l],
                            w_out[l], g_ffn[l], w_gu[l], w_down[l])
    return xcur.reshape(bsz, s, d)
```

```python
import functools

import jax
import jax.numpy as jnp
from jax import lax
from jax.experimental import pallas as pl
from jax.experimental.pallas import tpu as pltpu

F32 = jnp.float32
BF16 = jnp.bfloat16

D_MODEL = 2048
BRANCH_WIDTH = D_MODEL // 2
N_BRANCH = 3
MLA_NOPE = 128
MLA_ROPE = 64
MLA_V = 128
MLA_HEADS = BRANCH_WIDTH // MLA_V
MLA_QK = MLA_NOPE + MLA_ROPE
MLA_RANK = 512
ROPE_THETA = 10000.0
GLA_HEADS = 4
GLA_DV = BRANCH_WIDTH // GLA_HEADS
GLA_DK = GLA_DV // 2
GLA_GATE_RANK = 16
GLA_TAU = 16.0
FOX_DH = 128
FOX_HEADS = BRANCH_WIDTH // FOX_DH
FFN_HIDDEN = ((8 * D_MODEL + 3 * 256 - 1) // (3 * 256)) * 256
EPS = 1e-6
NEG_INF = -1e30

LANES = 128
BF16_SUBLANES = 16
HEAD_PAD = 256
VMEM_LIMIT = 48 * 1024 * 1024
VMEM_LIMIT_BIG = 56 * 1024 * 1024

IN_SPLIT = (MLA_RANK, MLA_RANK, MLA_ROPE, GLA_HEADS * GLA_DK, GLA_HEADS * GLA_DK, GLA_HEADS * GLA_DV, GLA_GATE_RANK,
            GLA_HEADS * GLA_DV, FOX_HEADS * FOX_DH, FOX_HEADS * FOX_DH, FOX_HEADS * FOX_DH, FOX_HEADS,
            N_BRANCH * D_MODEL)
IN_OFF = tuple(sum(IN_SPLIT[:i]) for i in range(len(IN_SPLIT) + 1))
FOX_V_COL = IN_OFF[10]

A_CQ, A_CKV, A_KR, A_MISC = 0, 512, 1024, 1152
A_WIDTH = 1280
MISC_FL = 0
MISC_GA = FOX_HEADS
B_GQ, B_GK, B_GV, B_GR, B_FQ, B_FK = 0, 512, 1024, 2048, 3072, 4096
B_WIDTH = 5120

ATT_TILE = 1024
ATT_SUB = 256
ATT_VROWS = LANES + BF16_SUBLANES
LOG2E = 1.4426950408889634

GLA_CHUNK = 64
GLA_SUB = 8
GLA_BLOCK = 512


def _aligned(start, m):
    return start if isinstance(start, int) else pl.multiple_of(start, m)


def _cparams(sem):
    return pltpu.CompilerParams(dimension_semantics=sem, vmem_limit_bytes=VMEM_LIMIT)


def _split3(x):
    x1 = x.astype(BF16)
    r1 = x - x1.astype(F32)
    x2 = r1.astype(BF16)
    x3 = (r1 - x2.astype(F32)).astype(BF16)
    return x1, x2, x3


def _log_sigmoid(z):
    return -(jnp.maximum(-z, 0.0) + jnp.log1p(jnp.exp(-jnp.abs(z))))


def _store_values_t(vt_out, vt):
    rows = vt.shape[1]
    r = lax.broadcasted_iota(jnp.int32, (BF16_SUBLANES, rows), 0)
    ones_rows = jnp.where(r == 0, 1.0, 0.0).astype(BF16)
    for h in range(vt.shape[0] // LANES):
        vt_out[h, 0, 0:LANES, :] = vt[h * LANES:(h + 1) * LANES, :].astype(BF16)
        vt_out[h, 0, LANES:ATT_VROWS, :] = ones_rows


def _rmsnorm_kernel(x_ref, g_ref, o_ref):
    x = x_ref[...]
    ms = jnp.mean(x * x, axis=-1, keepdims=True)
    o_ref[...] = (x * lax.rsqrt(ms + EPS) * g_ref[...]).astype(o_ref.dtype)


def _rmsnorm(x, g, tm=512):
    m, d = x.shape
    return pl.pallas_call(
        _rmsnorm_kernel,
        grid=(m // tm,),
        in_specs=[pl.BlockSpec((tm, d), lambda i: (i, 0)),
                  pl.BlockSpec((1, d), lambda i: (0, 0))],
        out_specs=pl.BlockSpec((tm, d), lambda i: (i, 0)),
        out_shape=jax.ShapeDtypeStruct((m, d), BF16),
        compiler_params=_cparams(("parallel",)),
        name="rmsnorm",
    )(x, g.reshape(1, d))


def _mm_kernel(a_ref, w_ref, o_ref, *, act):
    y = jnp.dot(a_ref[...], w_ref[...], preferred_element_type=F32)
    if act == "sigmoid":
        y = jax.nn.sigmoid(y)
    o_ref[...] = y.astype(o_ref.dtype)


def _matmul(a, w, out_dtype, tm, tn, act=None, name="matmul"):
    m, k = a.shape
    n = w.shape[1]
    return pl.pallas_call(
        functools.partial(_mm_kernel, act=act),
        grid=(m // tm, n // tn),
        in_specs=[pl.BlockSpec((tm, k), lambda i, j: (i, 0)),
                  pl.BlockSpec((k, tn), lambda i, j: (0, j))],
        out_specs=pl.BlockSpec((tm, tn), lambda i, j: (i, j)),
        out_shape=jax.ShapeDtypeStruct((m, n), out_dtype),
        compiler_params=_cparams(("parallel", "arbitrary")),
        name=name,
    )(a, w)


def _mm_res_kernel(*refs, with_norm):
    if with_norm:
        a_ref, w_ref, r_ref, g_ref, o_ref, h_ref = refs
    else:
        a_ref, w_ref, r_ref, o_ref = refs
    y = r_ref[...] + jnp.dot(a_ref[...], w_ref[...], preferred_element_type=F32)
    o_ref[...] = y
    if with_norm:
        ms = jnp.mean(y * y, axis=-1, keepdims=True)
        h_ref[...] = (y * lax.rsqrt(ms + EPS) * g_ref[...]).astype(h_ref.dtype)


def _matmul_residual(a, w, res, tm, name, g_next=None):
    m, k = a.shape
    n = w.shape[1]
    with_norm = g_next is not None
    row = pl.BlockSpec((tm, n), lambda i: (i, 0))
    in_specs = [pl.BlockSpec((tm, k), lambda i: (i, 0)),
                pl.BlockSpec((k, n), lambda i: (0, 0), pipeline_mode=pl.Buffered(1)),
                row]
    args = [a, w, res]
    out_specs, out_shape = row, jax.ShapeDtypeStruct((m, n), F32)
    if with_norm:
        in_specs.append(pl.BlockSpec((1, n), lambda i: (0, 0)))
        args.append(g_next.reshape(1, n))
        out_specs, out_shape = [row, row], [out_shape, jax.ShapeDtypeStruct((m, n), BF16)]
    return pl.pallas_call(
        functools.partial(_mm_res_kernel, with_norm=with_norm),
        grid=(m // tm,),
        in_specs=in_specs,
        out_specs=out_specs,
        out_shape=out_shape,
        compiler_params=pltpu.CompilerParams(dimension_semantics=("parallel",), vmem_limit_bytes=VMEM_LIMIT_BIG),
        name=name,
    )(*args)


def _swiglu_kernel(a_ref, wg_ref, wu_ref, o_ref):
    a = a_ref[...]
    g = jnp.dot(a, wg_ref[...], preferred_element_type=F32)
    u = jnp.dot(a, wu_ref[...], preferred_element_type=F32)
    o_ref[...] = (g * jax.nn.sigmoid(g) * u).astype(o_ref.dtype)


def _swiglu_up(a, w_gu, tm, tn):
    m, k = a.shape
    hidden = w_gu.shape[1] // 2
    nj = hidden // tn
    return pl.pallas_call(
        _swiglu_kernel,
        grid=(m // tm, nj),
        in_specs=[pl.BlockSpec((tm, k), lambda i, j: (i, 0)),
                  pl.BlockSpec((k, tn), lambda i, j: (0, j)),
                  pl.BlockSpec((k, tn), lambda i, j: (0, j + nj))],
        out_specs=pl.BlockSpec((tm, tn), lambda i, j: (i, j)),
        out_shape=jax.ShapeDtypeStruct((m, hidden), BF16),
        compiler_params=_cparams(("parallel", "arbitrary")),
        name="ffn_up",
    )(a, w_gu, w_gu)


def _merge_kernel(oa_ref, ob_ref, oc_ref, wa_ref, wb_ref, wc_ref, ga_ref, gb_ref, gc_ref, o_ref):
    acc = jnp.dot(oa_ref[...], wa_ref[...], preferred_element_type=F32) * ga_ref[...].astype(F32)
    acc += jnp.dot(ob_ref[...], wb_ref[...], preferred_element_type=F32) * gb_ref[...].astype(F32)
    acc += jnp.dot(oc_ref[...], wc_ref[...], preferred_element_type=F32) * gc_ref[...].astype(F32)
    o_ref[...] = acc.astype(o_ref.dtype)


def _merge(o_a, o_b, o_c, w_branch2d, gsig, tm, tn):
    m, bw = o_a.shape
    d = w_branch2d.shape[1]
    nj = d // tn
    o_spec = pl.BlockSpec((tm, bw), lambda i, j: (i, 0))

    def w_spec(n):
        return pl.BlockSpec((bw, tn), lambda i, j: (n, j))

    def g_spec(n):
        return pl.BlockSpec((tm, tn), lambda i, j: (i, n * nj + j))

    return pl.pallas_call(
        _merge_kernel,
        grid=(m // tm, nj),
        in_specs=[o_spec, o_spec, o_spec, w_spec(0), w_spec(1), w_spec(2), g_spec(0), g_spec(1), g_spec(2)],
        out_specs=pl.BlockSpec((tm, tn), lambda i, j: (i, j)),
        out_shape=jax.ShapeDtypeStruct((m, d), BF16),
        compiler_params=_cparams(("parallel", "arbitrary")),
        name="branch_merge",
    )(o_a, o_b, o_c, w_branch2d, w_branch2d, w_branch2d, gsig, gsig, gsig)


def _rope_table_kernel(pos_ref, inv_ref, c_ref, s1_ref, s2_ref):
    ang = pos_ref[...].astype(F32) * inv_ref[...]
    c = jnp.cos(ang)
    s = jnp.sin(ang)
    lane = lax.broadcasted_iota(jnp.int32, ang.shape, 1)
    half = MLA_ROPE // 2
    c_ref[...] = jnp.where(lane < MLA_ROPE, c, 0.0)
    s1_ref[...] = jnp.where((lane >= half) & (lane < MLA_ROPE), s, 0.0)
    s2_ref[...] = jnp.where(lane < half, -s, 0.0)


def _rope_tables(pos_col, inv128, tm=512):
    s = pos_col.shape[0]
    spec = pl.BlockSpec((tm, LANES), lambda i: (i, 0))
    shp = jax.ShapeDtypeStruct((s, LANES), F32)
    return pl.pallas_call(
        _rope_table_kernel,
        grid=(s // tm,),
        in_specs=[pl.BlockSpec((tm, 1), lambda i: (i, 0)),
                  pl.BlockSpec((1, LANES), lambda i: (0, 0))],
        out_specs=[spec, spec, spec],
        out_shape=[shp, shp, shp],
        compiler_params=_cparams(("parallel",)),
        name="rope_tables",
    )(pos_col, inv128)


def _rope128(x, c, s1, s2):
    half = MLA_ROPE // 2
    return x * c + pltpu.roll(x, half, 1) * s1 + pltpu.roll(x, LANES - half, 1) * s2


def _mla_prep_kernel(cq_ref, ckv_ref, kr_ref, wuq_ref, wuk_ref, wuvt_ref, gcq_ref, gckv_ref, gq_ref, gk_ref, gkr_ref,
                     c_ref, s1_ref, s2_ref, q_out, k_out, vt_out):
    def norm(x, g):
        return x * lax.rsqrt(jnp.mean(x * x, axis=-1, keepdims=True) + EPS) * g

    cqn = norm(cq_ref[...], gcq_ref[...]).astype(BF16)
    ckvn = norm(ckv_ref[...], gckv_ref[...]).astype(BF16)
    qa = jnp.dot(cqn, wuq_ref[...], preferred_element_type=F32)
    ka = jnp.dot(ckvn, wuk_ref[...], preferred_element_type=F32)
    _store_values_t(vt_out, lax.dot_general(wuvt_ref[...], ckvn, (((1,), (1,)), ((), ())),
                                            preferred_element_type=F32))
    krp = kr_ref[...]
    kr_ss = jnp.sum(krp * krp, axis=-1, keepdims=True)
    c, s1, s2 = c_ref[...], s1_ref[...], s2_ref[...]
    gq, gk, gkr = gq_ref[...], gk_ref[...], gkr_ref[...]
    scale = MLA_QK ** -0.5 * LOG2E
    for h in range(MLA_HEADS):
        qh = qa[:, h * HEAD_PAD:(h + 1) * HEAD_PAD]
        rq = lax.rsqrt(jnp.sum(qh * qh, axis=-1, keepdims=True) * (1.0 / MLA_QK) + EPS)
        qn = qh * rq * gq
        q_out[h, :, 0:LANES] = (qn[:, 0:LANES] * scale).astype(BF16)
        q_out[h, :, LANES:HEAD_PAD] = (_rope128(qn[:, LANES:HEAD_PAD], c, s1, s2) * scale).astype(BF16)
        kn = ka[:, h * MLA_NOPE:(h + 1) * MLA_NOPE]
        rk = lax.rsqrt((jnp.sum(kn * kn, axis=-1, keepdims=True) + kr_ss) * (1.0 / MLA_QK) + EPS)
        k_out[h, :, 0:LANES] = (kn * rk * gk).astype(BF16)
        k_out[h, :, LANES:HEAD_PAD] = _rope128(krp * rk * gkr, c, s1, s2).astype(BF16)


def _vt_spec(heads, tm):
    per_tile = ATT_TILE // tm
    return pl.BlockSpec((heads, 1, ATT_VROWS, tm), lambda i: (0, i // per_tile, 0, i % per_tile))


def _mla_prep(pa, wuq_p, wuk, wuv_t, g_cq, g_ckv, g_q, g_k, g_kr, tabs, tm=256):
    s = pa.shape[0]
    h = MLA_HEADS
    row = lambda w: pl.BlockSpec((1, w), lambda i: (0, 0))
    tab = pl.BlockSpec((tm, LANES), lambda i: (i, 0))
    return pl.pallas_call(
        _mla_prep_kernel,
        grid=(s // tm,),
        in_specs=[pl.BlockSpec((tm, MLA_RANK), lambda i: (i, A_CQ // MLA_RANK)),
                  pl.BlockSpec((tm, MLA_RANK), lambda i: (i, A_CKV // MLA_RANK)),
                  pl.BlockSpec((tm, LANES), lambda i: (i, A_KR // LANES)),
                  pl.BlockSpec(wuq_p.shape, lambda i: (0, 0)),
                  pl.BlockSpec(wuk.shape, lambda i: (0, 0)),
                  pl.BlockSpec(wuv_t.shape, lambda i: (0, 0)),
                  row(MLA_RANK), row(MLA_RANK), row(HEAD_PAD), row(LANES), row(LANES),
                  tab, tab, tab],
        out_specs=[pl.BlockSpec((h, tm, HEAD_PAD), lambda i: (0, i, 0)),
                   pl.BlockSpec((h, tm, HEAD_PAD), lambda i: (0, i, 0)),
                   _vt_spec(h, tm)],
        out_shape=[jax.ShapeDtypeStruct((h, s, HEAD_PAD), BF16),
                   jax.ShapeDtypeStruct((h, s, HEAD_PAD), BF16),
                   jax.ShapeDtypeStruct((h, s // ATT_TILE, ATT_VROWS, ATT_TILE), BF16)],
        compiler_params=_cparams(("parallel",)),
        name="mla_prep",
    )(pa, pa, pa, wuq_p, wuk, wuv_t, g_cq, g_ckv, g_q, g_k, g_kr, *tabs)


def _fox_cumsum_kernel(misc_ref, bf_ref, f_ref, ft_ref, carry_ref):
    blk = misc_ref.shape[0]

    @pl.when(pl.program_id(0) == 0)
    def _():
        carry_ref[...] = jnp.zeros(carry_ref.shape, F32)

    r = lax.broadcasted_iota(jnp.int32, (blk, blk), 0)
    c = lax.broadcasted_iota(jnp.int32, (blk, blk), 1)
    tri = jnp.where(c <= r, 1.0, 0.0).astype(BF16)
    x1, x2, x3 = _split3(_log_sigmoid(misc_ref[...] + bf_ref[...]))
    cs = (jnp.dot(tri, x1, preferred_element_type=F32)
          + jnp.dot(tri, x2, preferred_element_type=F32)
          + jnp.dot(tri, x3, preferred_element_type=F32)) + carry_ref[...]
    f_ref[...] = cs
    ft_ref[...] = cs.T
    carry_ref[...] = cs[blk - 1:blk, :]


def _fox_cumsum(pa, bf128, blk=512):
    s = pa.shape[0]
    return pl.pallas_call(
        _fox_cumsum_kernel,
        grid=(s // blk,),
        in_specs=[pl.BlockSpec((blk, LANES), lambda i: (i, A_MISC // LANES)),
                  pl.BlockSpec((1, LANES), lambda i: (0, 0))],
        out_specs=[pl.BlockSpec((blk, LANES), lambda i: (i, 0)),
                   pl.BlockSpec((LANES, blk), lambda i: (0, i))],
        out_shape=[jax.ShapeDtypeStruct((s, LANES), F32), jax.ShapeDtypeStruct((LANES, s), F32)],
        scratch_shapes=[pltpu.VMEM((1, LANES), F32)],
        compiler_params=_cparams(("arbitrary",)),
        name="fox_cumsum",
    )(pa, bf128)


def _fox_prep_kernel(fq_ref, fk_ref, f_ref, gq_ref, gk_ref, q_out, k_out):
    tm = fq_ref.shape[0]
    lane = lax.broadcasted_iota(jnp.int32, (tm, LANES), 1)
    ones_tail = jnp.where(lane < 3, 1.0, 0.0).astype(BF16)
    f_all = f_ref[...]
    scale = FOX_DH ** -0.5 * LOG2E
    gq, gk = gq_ref[...], gk_ref[...]
    for h in range(FOX_HEADS):
        sl = slice(h * FOX_DH, (h + 1) * FOX_DH)
        q = fq_ref[:, sl].astype(F32)
        k = fk_ref[:, sl].astype(F32)
        qn = q * lax.rsqrt(jnp.mean(q * q, axis=-1, keepdims=True) + EPS) * gq
        kn = k * lax.rsqrt(jnp.mean(k * k, axis=-1, keepdims=True) + EPS) * gk
        q_out[h, :, 0:LANES] = (qn * scale).astype(BF16)
        q_out[h, :, LANES:HEAD_PAD] = ones_tail
        k_out[h, :, 0:LANES] = kn.astype(BF16)
        nf = f_all[:, h:h + 1] * (-LOG2E)
        n1, n2, n3 = _split3(nf)
        tail = jnp.where(lane == 0, n1.astype(F32),
                         jnp.where(lane == 1, n2.astype(F32),
                                   jnp.where(lane == 2, n3.astype(F32), 0.0)))
        k_out[h, :, LANES:HEAD_PAD] = tail.astype(BF16)


def _fox_prep(pb, f_cum, g_q, g_k, tm=256):
    s = pb.shape[0]
    h = FOX_HEADS
    w = FOX_HEADS * FOX_DH
    row = pl.BlockSpec((1, FOX_DH), lambda i: (0, 0))
    return pl.pallas_call(
        _fox_prep_kernel,
        grid=(s // tm,),
        in_specs=[pl.BlockSpec((tm, w), lambda i: (i, B_FQ // w)),
                  pl.BlockSpec((tm, w), lambda i: (i, B_FK // w)),
                  pl.BlockSpec((tm, LANES), lambda i: (i, 0)),
                  row, row],
        out_specs=[pl.BlockSpec((h, tm, HEAD_PAD), lambda i: (0, i, 0)),
                   pl.BlockSpec((h, tm, HEAD_PAD), lambda i: (0, i, 0))],
        out_shape=[jax.ShapeDtypeStruct((h, s, HEAD_PAD), BF16),
                   jax.ShapeDtypeStruct((h, s, HEAD_PAD), BF16)],
        compiler_params=_cparams(("parallel",)),
        name="fox_prep",
    )(pb, pb, f_cum, g_q, g_k)


def _value_t_kernel(h_ref, w_ref, vt_out):
    vt = lax.dot_general(w_ref[...], h_ref[...], (((1,), (1,)), ((), ())), preferred_element_type=F32)
    _store_values_t(vt_out, vt)


def _value_t_proj(h, w_t, tm=512):
    s, d = h.shape
    heads = w_t.shape[0] // LANES
    return pl.pallas_call(
        _value_t_kernel,
        grid=(s // tm,),
        in_specs=[pl.BlockSpec((tm, d), lambda i: (i, 0)),
                  pl.BlockSpec(w_t.shape, lambda i: (0, 0))],
        out_specs=_vt_spec(heads, tm),
        out_shape=jax.ShapeDtypeStruct((heads, s // ATT_TILE, ATT_VROWS, ATT_TILE), BF16),
        compiler_params=_cparams(("parallel",)),
        name="fox_value_t",
    )(h, w_t)


def _attn_kernel(*refs, tq, use_f):
    if use_f:
        q_ref, k_ref, vt_ref, f_ref, o_ref, m_sc, acc_sc, s_a, s_b = refs
    else:
        q_ref, k_ref, vt_ref, o_ref, m_sc, acc_sc, s_a, s_b = refs
    h = pl.program_id(0)
    qi = pl.program_id(1)
    ns = tq // ATT_SUB
    if use_f:
        ft = [f_ref[pl.ds(h, 1), j * ATT_SUB:(j + 1) * ATT_SUB] * LOG2E for j in range(ns)]
    m_sc[...] = jnp.full(m_sc.shape, NEG_INF, F32)
    acc_sc[...] = jnp.zeros(acc_sc.shape, F32)

    def logits(dst, kb):
        dst[...] = lax.dot_general(k_ref[0, pl.ds(_aligned(kb * tq, tq), tq), :], q_ref[0],
                                   (((1,), (1,)), ((), ())), preferred_element_type=F32)

    def update(src, j, kb, k0, size, causal):
        s = src[k0:k0 + size, j * ATT_SUB:(j + 1) * ATT_SUB]
        if causal:
            key = lax.broadcasted_iota(jnp.int32, (size, ATT_SUB), 0)
            qry = lax.broadcasted_iota(jnp.int32, (size, ATT_SUB), 1)
            s = jnp.where(key <= qry, s, NEG_INF)
        m_old = m_sc[j]
        m_cur = jnp.max(s, axis=0, keepdims=True)
        if use_f:
            m_new = jnp.maximum(m_old, m_cur + ft[j])
            shift = m_new - ft[j]
        else:
            m_new = jnp.maximum(m_old, m_cur)
            shift = m_new
        p = jnp.exp2(s - shift).astype(BF16)
        alpha = jnp.exp2(m_old - m_new)
        acc_sc[j] = alpha * acc_sc[j] + jnp.dot(vt_ref[0, kb, :, k0:k0 + size], p, preferred_element_type=F32)
        m_sc[j] = m_new

    def full_tile(src, kb):
        for j in range(ns):
            update(src, j, kb, 0, tq, False)

    def diagonal_tile(src):
        for j in range(ns):
            if j > 0:
                update(src, j, qi, 0, j * ATT_SUB, False)
            update(src, j, qi, j * ATT_SUB, ATT_SUB, True)
        for j in range(ns):
            acc = acc_sc[j]
            o_t = acc[0:LANES, :] / acc[LANES:LANES + 1, :]
            o_ref[j * ATT_SUB:(j + 1) * ATT_SUB, :] = o_t.T.astype(o_ref.dtype)

    logits(s_a, 0)

    def pair(u, carry):
        t = 2 * u
        logits(s_b, t + 1)
        full_tile(s_a, t)
        logits(s_a, t + 2)
        full_tile(s_b, t + 1)
        return carry

    lax.fori_loop(0, qi // 2, pair, 0)

    @pl.when(qi % 2 == 1)
    def _():
        logits(s_b, qi)
        full_tile(s_a, qi - 1)
        diagonal_tile(s_b)

    @pl.when(qi % 2 == 0)
    def _():
        diagonal_tile(s_a)


def _attention(q, k, vt, f_t=None):
    hh, s, dk = q.shape
    _, nt, vrows, tq = vt.shape
    use_f = f_t is not None
    in_specs = [pl.BlockSpec((1, tq, dk), lambda h, i: (h, i, 0)),
                pl.BlockSpec((1, s, dk), lambda h, i: (h, 0, 0)),
                pl.BlockSpec((1, nt, vrows, tq), lambda h, i: (h, 0, 0, 0))]
    args = [q, k, vt]
    if use_f:
        in_specs.append(pl.BlockSpec((FOX_HEADS, tq), lambda h, i: (0, i)))
        args.append(f_t)
    ns = tq // ATT_SUB
    return pl.pallas_call(
        functools.partial(_attn_kernel, tq=tq, use_f=use_f),
        grid=(hh, s // tq),
        in_specs=in_specs,
        out_specs=pl.BlockSpec((tq, LANES), lambda h, i: (i, h)),
        out_shape=jax.ShapeDtypeStruct((s, hh * LANES), BF16),
        scratch_shapes=[pltpu.VMEM((ns, 1, ATT_SUB), F32), pltpu.VMEM((ns, vrows, ATT_SUB), F32),
                        pltpu.VMEM((tq, tq), F32), pltpu.VMEM((tq, tq), F32)],
        compiler_params=_cparams(("parallel", "arbitrary")),
        name="fox_attention" if use_f else "mla_attention",
    )(*args)


def _gla_kernel(q_ref, k_ref, v_ref, r_ref, misc_ref, wa_ref, ba_ref, go_ref, o_ref,
                st_ref, la_ref, b_ref, qe_ref, a_ref, u_ref, dl_ref, stb_ref):
    tb = q_ref.shape[0]
    nchunk = tb // GLA_CHUNK
    nsub = GLA_CHUNK // GLA_SUB
    C = GLA_CHUNK

    @pl.when(pl.program_id(1) == 0)
    def _():
        st_ref[...] = jnp.zeros(st_ref.shape, F32)

    m1, m2, _ = _split3(misc_ref[...])
    w1, w2, _ = _split3(wa_ref[...])
    z = (jnp.dot(m1, w1, preferred_element_type=F32) + jnp.dot(m1, w2, preferred_element_type=F32)
         + jnp.dot(m2, w1, preferred_element_type=F32)) + ba_ref[...]
    la_ref[...] = _log_sigmoid(z) * (LOG2E / GLA_TAU)

    r_i = lax.broadcasted_iota(jnp.int32, (C, C), 0)
    c_i = lax.broadcasted_iota(jnp.int32, (C, C), 1)
    tri = jnp.where(c_i <= r_i, 1.0, 0.0).astype(BF16)
    sub_row = lax.broadcasted_iota(jnp.int32, (GLA_SUB, GLA_DK), 0)
    col_sub = lax.broadcasted_iota(jnp.int32, (GLA_SUB, C), 1)
    scale = GLA_DK ** -0.5
    g_o = go_ref[...]

    chunk_rows = [slice(ci * C, (ci + 1) * C) for ci in range(nchunk)]

    for rows in chunk_rows:
        l1, l2, l3 = _split3(la_ref[rows, :])
        b_ref[rows, :] = (jnp.dot(tri, l1, preferred_element_type=F32) + jnp.dot(tri, l2, preferred_element_type=F32)
                          + jnp.dot(tri, l3, preferred_element_type=F32))

    for ci, rows in enumerate(chunk_rows):
        b = b_ref[rows, :]
        qc = q_ref[rows, :].astype(F32) * scale
        kc = k_ref[rows, :].astype(F32)
        b_last = b[C - 1:C, :]
        qe_ref[rows, :] = (qc * jnp.exp2(b)).astype(BF16)
        ke = (kc * jnp.exp2(b_last - b)).astype(BF16)
        u_ref[ci] = lax.dot_general(v_ref[rows, :], ke, (((0,), (0,)), ((), ())), preferred_element_type=F32)
        dl_ref[ci] = jnp.exp2(b_last)

        a_rows = []
        for si in range(nsub):
            lo = si * GLA_SUB
            qs, ks, bs = qc[lo:lo + GLA_SUB], kc[lo:lo + GLA_SUB], b[lo:lo + GLA_SUB]
            a = jnp.zeros((GLA_SUB, C), F32)
            for j in range(GLA_SUB):
                e = jnp.exp2(jnp.where(sub_row >= j, bs - bs[j:j + 1, :], NEG_INF))
                w = jnp.sum(qs * (ks[j:j + 1, :] * e), axis=-1, keepdims=True)
                a = jnp.where(col_sub == lo + j, w, a)
            if si > 0:
                ref_row = b[lo - 1:lo, :]
                q_i = qs * jnp.exp2(bs - ref_row)
                k_j = jnp.concatenate([kc[0:lo] * jnp.exp2(ref_row - b[0:lo]),
                                       jnp.zeros((C - lo, GLA_DK), F32)], axis=0)
                a = a + lax.dot_general(q_i.astype(BF16), k_j.astype(BF16), (((1,), (1,)), ((), ())),
                                        preferred_element_type=F32)
            a_rows.append(a)
        a_ref[rows, :] = jnp.concatenate(a_rows, axis=0).astype(BF16)

    st = st_ref[...]
    for ci in range(nchunk):
        stb_ref[ci] = st.astype(BF16)
        st = st * dl_ref[ci] + u_ref[ci]
    st_ref[...] = st

    for ci, rows in enumerate(chunk_rows):
        o = lax.dot_general(qe_ref[rows, :], stb_ref[ci], (((1,), (1,)), ((), ())), preferred_element_type=F32)
        o = o + jnp.dot(a_ref[rows, :], v_ref[rows, :], preferred_element_type=F32)
        on = o * lax.rsqrt(jnp.mean(o * o, axis=-1, keepdims=True) + EPS) * g_o
        r = r_ref[rows, :].astype(F32)
        o_ref[rows, :] = (on * (r * jax.nn.sigmoid(r))).astype(o_ref.dtype)


def _gla(pb, pa, wa_p, b_a, g_o, tb=GLA_BLOCK):
    s = pb.shape[0]
    nchunk = tb // GLA_CHUNK
    return pl.pallas_call(
        _gla_kernel,
        grid=(GLA_HEADS, s // tb),
        in_specs=[pl.BlockSpec((tb, GLA_DK), lambda h, i: (i, B_GQ // GLA_DK + h)),
                  pl.BlockSpec((tb, GLA_DK), lambda h, i: (i, B_GK // GLA_DK + h)),
                  pl.BlockSpec((tb, GLA_DV), lambda h, i: (i, B_GV // GLA_DV + h)),
                  pl.BlockSpec((tb, GLA_DV), lambda h, i: (i, B_GR // GLA_DV + h)),
                  pl.BlockSpec((tb, LANES), lambda h, i: (i, A_MISC // LANES)),
                  pl.BlockSpec((LANES, GLA_DK), lambda h, i: (0, h)),
                  pl.BlockSpec((1, GLA_DK), lambda h, i: (0, h)),
                  pl.BlockSpec((1, GLA_DV), lambda h, i: (0, 0))],
        out_specs=pl.BlockSpec((tb, GLA_DV), lambda h, i: (i, h)),
        out_shape=jax.ShapeDtypeStruct((s, GLA_HEADS * GLA_DV), BF16),
        scratch_shapes=[pltpu.VMEM((GLA_DV, GLA_DK), F32),
                        pltpu.VMEM((tb, GLA_DK), F32),
                        pltpu.VMEM((tb, GLA_DK), F32),
                        pltpu.VMEM((tb, GLA_DK), BF16),
                        pltpu.VMEM((tb, GLA_CHUNK), BF16),
                        pltpu.VMEM((nchunk, GLA_DV, GLA_DK), F32),
                        pltpu.VMEM((nchunk, 1, GLA_DK), F32),
                        pltpu.VMEM((nchunk, GLA_DV, GLA_DK), BF16)],
        compiler_params=_cparams(("parallel", "arbitrary")),
        name="gla",
    )(pb, pb, pb, pb, pa, wa_p, b_a, g_o)


def _group_w_in_kernel(w_ref, wa_ref, wb_ref, wg_ref):
    off = IN_OFF
    col = lambda i: w_ref[:, off[i]:off[i + 1]].astype(BF16)
    rows = w_ref.shape[0]
    wa_ref[:, A_CQ:A_KR] = w_ref[:, off[0]:off[2]].astype(BF16)
    wa_ref[:, A_KR:A_MISC] = jnp.concatenate([col(2), jnp.zeros((rows, LANES - MLA_ROPE), BF16)], axis=1)
    wa_ref[:, A_MISC:A_WIDTH] = jnp.concatenate(
        [col(11), col(6), jnp.zeros((rows, LANES - FOX_HEADS - GLA_GATE_RANK), BF16)], axis=1)
    wb_ref[:, B_GQ:B_GR] = w_ref[:, off[3]:off[6]].astype(BF16)
    wb_ref[:, B_GR:B_WIDTH] = w_ref[:, off[7]:off[10]].astype(BF16)
    wg_ref[...] = w_ref[:, off[12]:off[13]].astype(BF16)


def _group_w_in(w, tk=64):
    d, n = w.shape
    ng = N_BRANCH * D_MODEL
    return pl.pallas_call(
        _group_w_in_kernel,
        grid=(d // tk,),
        in_specs=[pl.BlockSpec((tk, n), lambda i: (i, 0))],
        out_specs=[pl.BlockSpec((tk, A_WIDTH), lambda i: (i, 0)),
                   pl.BlockSpec((tk, B_WIDTH), lambda i: (i, 0)),
                   pl.BlockSpec((tk, ng), lambda i: (i, 0))],
        out_shape=[jax.ShapeDtypeStruct((d, A_WIDTH), BF16),
                   jax.ShapeDtypeStruct((d, B_WIDTH), BF16),
                   jax.ShapeDtypeStruct((d, ng), BF16)],
        compiler_params=_cparams(("parallel",)),
        name="group_w_in",
    )(w)


def _pad_lanes(v, n):
    return jnp.pad(v, (0, n - v.shape[0])).reshape(1, n)


def _layer(x, h, tabs, g_next, w_in, g_cq, w_uq, g_ckv, w_ukv, g_mla_q, g_mla_k, w_a2, b_a, g_gla_o,
           g_fox_q, g_fox_k, b_f, w_branch, w_out, g_ffn, w_gu, w_down):
    wa, wb, wg = _group_w_in(w_in)
    pa = _matmul(h, wa, F32, 1024, A_WIDTH, name="in_proj_a")
    pb = _matmul(h, wb, BF16, 1024, 1024, name="in_proj_b")
    gsig = _matmul(h, wg, BF16, 1024, 1024, act="sigmoid", name="in_proj_gates")

    wuq_p = jnp.pad(w_uq.reshape(MLA_RANK, MLA_HEADS, MLA_QK),
                    ((0, 0), (0, 0), (0, HEAD_PAD - MLA_QK))).reshape(MLA_RANK, MLA_HEADS * HEAD_PAD).astype(BF16)
    w_ukv3 = w_ukv.reshape(MLA_RANK, MLA_HEADS, MLA_NOPE + MLA_V)
    wuk = w_ukv3[:, :, :MLA_NOPE].reshape(MLA_RANK, MLA_HEADS * MLA_NOPE).astype(BF16)
    wuv_t = w_ukv3[:, :, MLA_NOPE:].reshape(MLA_RANK, MLA_HEADS * MLA_V).T.astype(BF16)
    q_a, k_a, vt_a = _mla_prep(pa, wuq_p, wuk, wuv_t, g_cq.reshape(1, -1), g_ckv.reshape(1, -1),
                               _pad_lanes(g_mla_q, HEAD_PAD), g_mla_k[:MLA_NOPE].reshape(1, -1),
                               _pad_lanes(g_mla_k[MLA_NOPE:], LANES), tabs)
    o_a = _attention(q_a, k_a, vt_a)

    wa_p = jnp.zeros((LANES, GLA_HEADS * GLA_DK), F32).at[MISC_GA:MISC_GA + GLA_GATE_RANK].set(w_a2)
    o_b = _gla(pb, pa, wa_p, b_a.reshape(1, -1), g_gla_o.reshape(1, -1))

    f_cum, f_cum_t = _fox_cumsum(pa, _pad_lanes(b_f, LANES))
    q_c, k_c = _fox_prep(pb, f_cum, g_fox_q.reshape(1, -1), g_fox_k.reshape(1, -1))
    vt_c = _value_t_proj(h, w_in[:, FOX_V_COL:FOX_V_COL + FOX_HEADS * FOX_DH].T.astype(BF16))
    o_c = _attention(q_c, k_c, vt_c, f_cum_t)

    merged = _merge(o_a, o_b, o_c, w_branch.reshape(N_BRANCH * BRANCH_WIDTH, D_MODEL).astype(BF16), gsig, 1024, 512)
    x1, h2 = _matmul_residual(merged, w_out.astype(BF16), x, 512, "out_proj", g_next=g_ffn)
    act = _swiglu_up(h2, w_gu.astype(BF16), 1024, 512)
    if g_next is None:
        return _matmul_residual(act, w_down.astype(BF16), x1, 256, "ffn_down"), None
    return _matmul_residual(act, w_down.astype(BF16), x1, 256, "ffn_down", g_next=g_next)


def kernel(x, positions, g_mix, w_in, g_cq, w_uq, g_ckv, w_ukv, g_mla_q, g_mla_k, w_a2, b_a, g_gla_o,
           g_fox_q, g_fox_k, b_f, w_branch, w_out, g_ffn, w_gu, w_down):
    bsz, s, d = x.shape
    assert bsz == 1 and d == D_MODEL and s % ATT_TILE == 0
    half = MLA_ROPE // 2
    inv = ROPE_THETA ** (-jnp.arange(half, dtype=F32) / half)
    inv128 = jnp.concatenate([inv, inv, jnp.zeros((LANES - MLA_ROPE,), F32)]).reshape(1, LANES)
    tabs = _rope_tables(positions.reshape(s, 1), inv128)
    depth = w_in.shape[0]
    xcur = x.reshape(s, d)
    hcur = _rmsnorm(xcur, g_mix[0])
    for l in range(depth):
        g_next = g_mix[l + 1] if l + 1 < depth else None
        xcur, hcur = _layer(xcur, hcur, tabs, g_next, w_in[l], g_cq[l], w_uq[l], g_ckv[l], w_ukv[l], g_mla_q[l],
                            g_mla_k[l], w_a2[l], b_a[l], g_gla_o[l], g_fox_q[l], g_fox_k[l], b_f[l], w_branch[l],
                            w_out[l], g_ffn[l], w_gu[l], w_down[l])
    return xcur.reshape(bsz, s, d)
```

```python
import functools

import jax
import jax.numpy as jnp
from jax import lax
from jax.experimental import pallas as pl
from jax.experimental.pallas import tpu as pltpu

F32 = jnp.float32
BF16 = jnp.bfloat16

D_MODEL = 2048
BRANCH_WIDTH = D_MODEL // 2
N_BRANCH = 3
MLA_NOPE = 128
MLA_ROPE = 64
MLA_V = 128
MLA_HEADS = BRANCH_WIDTH // MLA_V
MLA_QK = MLA_NOPE + MLA_ROPE
MLA_RANK = 512
ROPE_THETA = 10000.0
GLA_HEADS = 4
GLA_DV = BRANCH_WIDTH // GLA_HEADS
GLA_DK = GLA_DV // 2
GLA_GATE_RANK = 16
GLA_TAU = 16.0
FOX_DH = 128
FOX_HEADS = BRANCH_WIDTH // FOX_DH
FFN_HIDDEN = ((8 * D_MODEL + 3 * 256 - 1) // (3 * 256)) * 256
EPS = 1e-6
NEG_INF = -1e30

LANES = 128
BF16_SUBLANES = 16
HEAD_PAD = 256
VMEM_LIMIT = 48 * 1024 * 1024
VMEM_LIMIT_BIG = 56 * 1024 * 1024

IN_SPLIT = (MLA_RANK, MLA_RANK, MLA_ROPE, GLA_HEADS * GLA_DK, GLA_HEADS * GLA_DK, GLA_HEADS * GLA_DV, GLA_GATE_RANK,
            GLA_HEADS * GLA_DV, FOX_HEADS * FOX_DH, FOX_HEADS * FOX_DH, FOX_HEADS * FOX_DH, FOX_HEADS,
            N_BRANCH * D_MODEL)
IN_OFF = tuple(sum(IN_SPLIT[:i]) for i in range(len(IN_SPLIT) + 1))

A_CQ, A_CKV, A_KR, A_MISC = 0, 512, 1024, 1152
A_WIDTH = 1280
MISC_FL = 0
MISC_GA = FOX_HEADS
B_GQ, B_GK, B_GV, B_GR, B_FQ, B_FK = 0, 512, 1024, 2048, 3072, 4096
B_WIDTH = 5120

ATT_TILE = 1024
ATT_SUB = 256
ATT_VROWS = LANES + BF16_SUBLANES
LOG2E = 1.4426950408889634

GLA_CHUNK = 64
GLA_SUB = 8
GLA_BLOCK = 512


def _aligned(start, m):
    return start if isinstance(start, int) else pl.multiple_of(start, m)


def _cparams(sem):
    return pltpu.CompilerParams(dimension_semantics=sem, vmem_limit_bytes=VMEM_LIMIT)


def _split3(x):
    x1 = x.astype(BF16)
    r1 = x - x1.astype(F32)
    x2 = r1.astype(BF16)
    x3 = (r1 - x2.astype(F32)).astype(BF16)
    return x1, x2, x3


def _log_sigmoid(z):
    return -(jnp.maximum(-z, 0.0) + jnp.log1p(jnp.exp(-jnp.abs(z))))


def _store_values_t(vt_out, vt):
    rows = vt.shape[1]
    r = lax.broadcasted_iota(jnp.int32, (BF16_SUBLANES, rows), 0)
    ones_rows = jnp.where(r == 0, 1.0, 0.0).astype(BF16)
    for h in range(vt.shape[0] // LANES):
        vt_out[h, 0, 0:LANES, :] = vt[h * LANES:(h + 1) * LANES, :].astype(BF16)
        vt_out[h, 0, LANES:ATT_VROWS, :] = ones_rows


def _rmsnorm_kernel(x_ref, g_ref, o_ref):
    x = x_ref[...]
    ms = jnp.mean(x * x, axis=-1, keepdims=True)
    o_ref[...] = (x * lax.rsqrt(ms + EPS) * g_ref[...]).astype(o_ref.dtype)


def _rmsnorm(x, g, tm=512):
    m, d = x.shape
    return pl.pallas_call(
        _rmsnorm_kernel,
        grid=(m // tm,),
        in_specs=[pl.BlockSpec((tm, d), lambda i: (i, 0)),
                  pl.BlockSpec((1, d), lambda i: (0, 0))],
        out_specs=pl.BlockSpec((tm, d), lambda i: (i, 0)),
        out_shape=jax.ShapeDtypeStruct((m, d), BF16),
        compiler_params=_cparams(("parallel",)),
        name="rmsnorm",
    )(x, g.reshape(1, d))


def _mm_kernel(a_ref, w_ref, o_ref, *, act):
    y = jnp.dot(a_ref[...], w_ref[...], preferred_element_type=F32)
    if act == "sigmoid":
        y = jax.nn.sigmoid(y)
    o_ref[...] = y.astype(o_ref.dtype)


def _matmul(a, w, out_dtype, tm, tn, act=None, name="matmul"):
    m, k = a.shape
    n = w.shape[1]
    return pl.pallas_call(
        functools.partial(_mm_kernel, act=act),
        grid=(m // tm, n // tn),
        in_specs=[pl.BlockSpec((tm, k), lambda i, j: (i, 0)),
                  pl.BlockSpec((k, tn), lambda i, j: (0, j))],
        out_specs=pl.BlockSpec((tm, tn), lambda i, j: (i, j)),
        out_shape=jax.ShapeDtypeStruct((m, n), out_dtype),
        compiler_params=_cparams(("parallel", "arbitrary")),
        name=name,
    )(a, w)


def _mm_res_kernel(*refs, with_norm):
    if with_norm:
        a_ref, w_ref, r_ref, g_ref, o_ref, h_ref = refs
    else:
        a_ref, w_ref, r_ref, o_ref = refs
    y = r_ref[...] + jnp.dot(a_ref[...], w_ref[...], preferred_element_type=F32)
    o_ref[...] = y
    if with_norm:
        ms = jnp.mean(y * y, axis=-1, keepdims=True)
        h_ref[...] = (y * lax.rsqrt(ms + EPS) * g_ref[...]).astype(h_ref.dtype)


def _matmul_residual(a, w, res, tm, name, g_next=None):
    m, k = a.shape
    n = w.shape[1]
    with_norm = g_next is not None
    row = pl.BlockSpec((tm, n), lambda i: (i, 0))
    in_specs = [pl.BlockSpec((tm, k), lambda i: (i, 0)),
                pl.BlockSpec((k, n), lambda i: (0, 0), pipeline_mode=pl.Buffered(1)),
                row]
    args = [a, w, res]
    out_specs, out_shape = row, jax.ShapeDtypeStruct((m, n), F32)
    if with_norm:
        in_specs.append(pl.BlockSpec((1, n), lambda i: (0, 0)))
        args.append(g_next.reshape(1, n))
        out_specs, out_shape = [row, row], [out_shape, jax.ShapeDtypeStruct((m, n), BF16)]
    return pl.pallas_call(
        functools.partial(_mm_res_kernel, with_norm=with_norm),
        grid=(m // tm,),
        in_specs=in_specs,
        out_specs=out_specs,
        out_shape=out_shape,
        compiler_params=pltpu.CompilerParams(dimension_semantics=("parallel",), vmem_limit_bytes=VMEM_LIMIT_BIG),
        name=name,
    )(*args)


def _swiglu_kernel(a_ref, wg_ref, wu_ref, o_ref, wg_sc, wu_sc):
    @pl.when(pl.program_id(1) == 0)
    def _():
        wg_sc[...] = wg_ref[...].astype(BF16)
        wu_sc[...] = wu_ref[...].astype(BF16)

    a = a_ref[...]
    g = jnp.dot(a, wg_sc[...], preferred_element_type=F32)
    u = jnp.dot(a, wu_sc[...], preferred_element_type=F32)
    o_ref[...] = (g * jax.nn.sigmoid(g) * u).astype(o_ref.dtype)


def _swiglu_up(a, w_gu_all, layer, tm, tn):
    m, k = a.shape
    hidden = w_gu_all.shape[2] // 2
    nj = hidden // tn
    return pl.pallas_call(
        _swiglu_kernel,
        grid=(nj, m // tm),
        in_specs=[pl.BlockSpec((tm, k), lambda j, i: (i, 0)),
                  pl.BlockSpec((None, k, tn), lambda j, i: (layer, 0, j)),
                  pl.BlockSpec((None, k, tn), lambda j, i: (layer, 0, j + nj))],
        out_specs=pl.BlockSpec((tm, tn), lambda j, i: (i, j)),
        out_shape=jax.ShapeDtypeStruct((m, hidden), BF16),
        scratch_shapes=[pltpu.VMEM((k, tn), BF16), pltpu.VMEM((k, tn), BF16)],
        compiler_params=_cparams(("parallel", "arbitrary")),
        name="ffn_up",
    )(a, w_gu_all, w_gu_all)


def _merge_kernel(oa_ref, ob_ref, oc_ref, wa_ref, wb_ref, wc_ref, ga_ref, gb_ref, gc_ref, o_ref):
    acc = jnp.dot(oa_ref[...], wa_ref[...], preferred_element_type=F32) * ga_ref[...].astype(F32)
    acc += jnp.dot(ob_ref[...], wb_ref[...], preferred_element_type=F32) * gb_ref[...].astype(F32)
    acc += jnp.dot(oc_ref[...], wc_ref[...], preferred_element_type=F32) * gc_ref[...].astype(F32)
    o_ref[...] = acc.astype(o_ref.dtype)


def _merge(o_a, o_b, o_c, w_branch2d, gsig, tm, tn):
    m, bw = o_a.shape
    d = w_branch2d.shape[1]
    nj = d // tn
    o_spec = pl.BlockSpec((tm, bw), lambda i, j: (i, 0))

    def w_spec(n):
        return pl.BlockSpec((bw, tn), lambda i, j: (n, j))

    def g_spec(n):
        return pl.BlockSpec((tm, tn), lambda i, j: (i, n * nj + j))

    return pl.pallas_call(
        _merge_kernel,
        grid=(m // tm, nj),
        in_specs=[o_spec, o_spec, o_spec, w_spec(0), w_spec(1), w_spec(2), g_spec(0), g_spec(1), g_spec(2)],
        out_specs=pl.BlockSpec((tm, tn), lambda i, j: (i, j)),
        out_shape=jax.ShapeDtypeStruct((m, d), BF16),
        compiler_params=_cparams(("parallel", "arbitrary")),
        name="branch_merge",
    )(o_a, o_b, o_c, w_branch2d, w_branch2d, w_branch2d, gsig, gsig, gsig)


def _rope_table_kernel(pos_ref, inv_ref, c_ref, s1_ref, s2_ref):
    ang = pos_ref[...].astype(F32) * inv_ref[...]
    c = jnp.cos(ang)
    s = jnp.sin(ang)
    lane = lax.broadcasted_iota(jnp.int32, ang.shape, 1)
    half = MLA_ROPE // 2
    c_ref[...] = jnp.where(lane < MLA_ROPE, c, 0.0)
    s1_ref[...] = jnp.where((lane >= half) & (lane < MLA_ROPE), s, 0.0)
    s2_ref[...] = jnp.where(lane < half, -s, 0.0)


def _rope_tables(pos_col, inv128, tm=512):
    s = pos_col.shape[0]
    spec = pl.BlockSpec((tm, LANES), lambda i: (i, 0))
    shp = jax.ShapeDtypeStruct((s, LANES), F32)
    return pl.pallas_call(
        _rope_table_kernel,
        grid=(s // tm,),
        in_specs=[pl.BlockSpec((tm, 1), lambda i: (i, 0)),
                  pl.BlockSpec((1, LANES), lambda i: (0, 0))],
        out_specs=[spec, spec, spec],
        out_shape=[shp, shp, shp],
        compiler_params=_cparams(("parallel",)),
        name="rope_tables",
    )(pos_col, inv128)


def _rope128(x, c, s1, s2):
    half = MLA_ROPE // 2
    return x * c + pltpu.roll(x, half, 1) * s1 + pltpu.roll(x, LANES - half, 1) * s2


def _mla_prep_kernel(cq_ref, ckv_ref, kr_ref, wuq_ref, wuk_ref, wuvt_ref, gcq_ref, gckv_ref, gq_ref, gk_ref, gkr_ref,
                     c_ref, s1_ref, s2_ref, q_out, k_out, vt_out):
    def norm(x, g):
        return x * lax.rsqrt(jnp.mean(x * x, axis=-1, keepdims=True) + EPS) * g

    cqn = norm(cq_ref[...], gcq_ref[...]).astype(BF16)
    ckvn = norm(ckv_ref[...], gckv_ref[...]).astype(BF16)
    qa = jnp.dot(cqn, wuq_ref[...], preferred_element_type=F32)
    ka = jnp.dot(ckvn, wuk_ref[...], preferred_element_type=F32)
    _store_values_t(vt_out, lax.dot_general(wuvt_ref[...], ckvn, (((1,), (1,)), ((), ())),
                                            preferred_element_type=F32))
    krp = kr_ref[...]
    kr_ss = jnp.sum(krp * krp, axis=-1, keepdims=True)
    c, s1, s2 = c_ref[...], s1_ref[...], s2_ref[...]
    gq, gk, gkr = gq_ref[...], gk_ref[...], gkr_ref[...]
    scale = MLA_QK ** -0.5 * LOG2E
    for h in range(MLA_HEADS):
        qh = qa[:, h * HEAD_PAD:(h + 1) * HEAD_PAD]
        rq = lax.rsqrt(jnp.sum(qh * qh, axis=-1, keepdims=True) * (1.0 / MLA_QK) + EPS)
        qn = qh * rq * gq
        q_out[h, :, 0:LANES] = (qn[:, 0:LANES] * scale).astype(BF16)
        q_out[h, :, LANES:HEAD_PAD] = (_rope128(qn[:, LANES:HEAD_PAD], c, s1, s2) * scale).astype(BF16)
        kn = ka[:, h * MLA_NOPE:(h + 1) * MLA_NOPE]
        rk = lax.rsqrt((jnp.sum(kn * kn, axis=-1, keepdims=True) + kr_ss) * (1.0 / MLA_QK) + EPS)
        k_out[h, :, 0:LANES] = (kn * rk * gk).astype(BF16)
        k_out[h, :, LANES:HEAD_PAD] = _rope128(krp * rk * gkr, c, s1, s2).astype(BF16)


def _vt_spec(heads, tm):
    per_tile = ATT_TILE // tm
    return pl.BlockSpec((heads, 1, ATT_VROWS, tm), lambda i: (0, i // per_tile, 0, i % per_tile))


def _mla_prep(pa, wuq_p, wuk, wuv_t, g_cq, g_ckv, g_q, g_k, g_kr, tabs, tm=256):
    s = pa.shape[0]
    h = MLA_HEADS
    row = lambda w: pl.BlockSpec((1, w), lambda i: (0, 0))
    tab = pl.BlockSpec((tm, LANES), lambda i: (i, 0))
    return pl.pallas_call(
        _mla_prep_kernel,
        grid=(s // tm,),
        in_specs=[pl.BlockSpec((tm, MLA_RANK), lambda i: (i, A_CQ // MLA_RANK)),
                  pl.BlockSpec((tm, MLA_RANK), lambda i: (i, A_CKV // MLA_RANK)),
                  pl.BlockSpec((tm, LANES), lambda i: (i, A_KR // LANES)),
                  pl.BlockSpec(wuq_p.shape, lambda i: (0, 0)),
                  pl.BlockSpec(wuk.shape, lambda i: (0, 0)),
                  pl.BlockSpec(wuv_t.shape, lambda i: (0, 0)),
                  row(MLA_RANK), row(MLA_RANK), row(HEAD_PAD), row(LANES), row(LANES),
                  tab, tab, tab],
        out_specs=[pl.BlockSpec((h, tm, HEAD_PAD), lambda i: (0, i, 0)),
                   pl.BlockSpec((h, tm, HEAD_PAD), lambda i: (0, i, 0)),
                   _vt_spec(h, tm)],
        out_shape=[jax.ShapeDtypeStruct((h, s, HEAD_PAD), BF16),
                   jax.ShapeDtypeStruct((h, s, HEAD_PAD), BF16),
                   jax.ShapeDtypeStruct((h, s // ATT_TILE, ATT_VROWS, ATT_TILE), BF16)],
        compiler_params=_cparams(("parallel",)),
        name="mla_prep",
    )(pa, pa, pa, wuq_p, wuk, wuv_t, g_cq, g_ckv, g_q, g_k, g_kr, *tabs)


def _fox_cumsum_kernel(misc_ref, bf_ref, f_ref, ft_ref, carry_ref):
    blk = misc_ref.shape[0]

    @pl.when(pl.program_id(0) == 0)
    def _():
        carry_ref[...] = jnp.zeros(carry_ref.shape, F32)

    r = lax.broadcasted_iota(jnp.int32, (blk, blk), 0)
    c = lax.broadcasted_iota(jnp.int32, (blk, blk), 1)
    tri = jnp.where(c <= r, 1.0, 0.0).astype(BF16)
    x1, x2, x3 = _split3(_log_sigmoid(misc_ref[...] + bf_ref[...]))
    cs = (jnp.dot(tri, x1, preferred_element_type=F32)
          + jnp.dot(tri, x2, preferred_element_type=F32)
          + jnp.dot(tri, x3, preferred_element_type=F32)) + carry_ref[...]
    f_ref[...] = cs
    ft_ref[...] = cs.T
    carry_ref[...] = cs[blk - 1:blk, :]


def _fox_cumsum(pa, bf128, blk=512):
    s = pa.shape[0]
    return pl.pallas_call(
        _fox_cumsum_kernel,
        grid=(s // blk,),
        in_specs=[pl.BlockSpec((blk, LANES), lambda i: (i, A_MISC // LANES)),
                  pl.BlockSpec((1, LANES), lambda i: (0, 0))],
        out_specs=[pl.BlockSpec((blk, LANES), lambda i: (i, 0)),
                   pl.BlockSpec((LANES, blk), lambda i: (0, i))],
        out_shape=[jax.ShapeDtypeStruct((s, LANES), F32), jax.ShapeDtypeStruct((LANES, s), F32)],
        scratch_shapes=[pltpu.VMEM((1, LANES), F32)],
        compiler_params=_cparams(("arbitrary",)),
        name="fox_cumsum",
    )(pa, bf128)


def _fox_prep_kernel(fq_ref, fk_ref, f_ref, gq_ref, gk_ref, q_out, k_out):
    tm = fq_ref.shape[0]
    lane = lax.broadcasted_iota(jnp.int32, (tm, LANES), 1)
    ones_tail = jnp.where(lane < 3, 1.0, 0.0).astype(BF16)
    f_all = f_ref[...]
    scale = FOX_DH ** -0.5 * LOG2E
    gq, gk = gq_ref[...], gk_ref[...]
    for h in range(FOX_HEADS):
        sl = slice(h * FOX_DH, (h + 1) * FOX_DH)
        q = fq_ref[:, sl].astype(F32)
        k = fk_ref[:, sl].astype(F32)
        qn = q * lax.rsqrt(jnp.mean(q * q, axis=-1, keepdims=True) + EPS) * gq
        kn = k * lax.rsqrt(jnp.mean(k * k, axis=-1, keepdims=True) + EPS) * gk
        q_out[h, :, 0:LANES] = (qn * scale).astype(BF16)
        q_out[h, :, LANES:HEAD_PAD] = ones_tail
        k_out[h, :, 0:LANES] = kn.astype(BF16)
        nf = f_all[:, h:h + 1] * (-LOG2E)
        n1, n2, n3 = _split3(nf)
        tail = jnp.where(lane == 0, n1.astype(F32),
                         jnp.where(lane == 1, n2.astype(F32),
                                   jnp.where(lane == 2, n3.astype(F32), 0.0)))
        k_out[h, :, LANES:HEAD_PAD] = tail.astype(BF16)


def _fox_prep(pb, f_cum, g_q, g_k, tm=256):
    s = pb.shape[0]
    h = FOX_HEADS
    w = FOX_HEADS * FOX_DH
    row = pl.BlockSpec((1, FOX_DH), lambda i: (0, 0))
    return pl.pallas_call(
        _fox_prep_kernel,
        grid=(s // tm,),
        in_specs=[pl.BlockSpec((tm, w), lambda i: (i, B_FQ // w)),
                  pl.BlockSpec((tm, w), lambda i: (i, B_FK // w)),
                  pl.BlockSpec((tm, LANES), lambda i: (i, 0)),
                  row, row],
        out_specs=[pl.BlockSpec((h, tm, HEAD_PAD), lambda i: (0, i, 0)),
                   pl.BlockSpec((h, tm, HEAD_PAD), lambda i: (0, i, 0))],
        out_shape=[jax.ShapeDtypeStruct((h, s, HEAD_PAD), BF16),
                   jax.ShapeDtypeStruct((h, s, HEAD_PAD), BF16)],
        compiler_params=_cparams(("parallel",)),
        name="fox_prep",
    )(pb, pb, f_cum, g_q, g_k)


def _value_t_kernel(h_ref, w_ref, vt_out):
    vt = lax.dot_general(w_ref[...], h_ref[...], (((1,), (1,)), ((), ())), preferred_element_type=F32)
    _store_values_t(vt_out, vt)


def _value_t_proj(h, w_t, tm=512):
    s, d = h.shape
    heads = w_t.shape[0] // LANES
    return pl.pallas_call(
        _value_t_kernel,
        grid=(s // tm,),
        in_specs=[pl.BlockSpec((tm, d), lambda i: (i, 0)),
                  pl.BlockSpec(w_t.shape, lambda i: (0, 0))],
        out_specs=_vt_spec(heads, tm),
        out_shape=jax.ShapeDtypeStruct((heads, s // ATT_TILE, ATT_VROWS, ATT_TILE), BF16),
        compiler_params=_cparams(("parallel",)),
        name="fox_value_t",
    )(h, w_t)


def _attn_kernel(*refs, tq, use_f):
    if use_f:
        q_ref, k_ref, vt_ref, f_ref, o_ref, m_sc, acc_sc, s_a, s_b = refs
    else:
        q_ref, k_ref, vt_ref, o_ref, m_sc, acc_sc, s_a, s_b = refs
    h = pl.program_id(0)
    qi = pl.program_id(1)
    ns = tq // ATT_SUB
    if use_f:
        ft = [f_ref[pl.ds(h, 1), j * ATT_SUB:(j + 1) * ATT_SUB] * LOG2E for j in range(ns)]
    m_sc[...] = jnp.full(m_sc.shape, NEG_INF, F32)
    acc_sc[...] = jnp.zeros(acc_sc.shape, F32)

    def logits(dst, kb):
        dst[...] = lax.dot_general(k_ref[0, pl.ds(_aligned(kb * tq, tq), tq), :], q_ref[0],
                                   (((1,), (1,)), ((), ())), preferred_element_type=F32)

    def update(src, j, kb, k0, size, causal):
        s = src[k0:k0 + size, j * ATT_SUB:(j + 1) * ATT_SUB]
        if causal:
            key = lax.broadcasted_iota(jnp.int32, (size, ATT_SUB), 0)
            qry = lax.broadcasted_iota(jnp.int32, (size, ATT_SUB), 1)
            s = jnp.where(key <= qry, s, NEG_INF)
        m_old = m_sc[j]
        m_cur = jnp.max(s, axis=0, keepdims=True)
        if use_f:
            m_new = jnp.maximum(m_old, m_cur + ft[j])
            shift = m_new - ft[j]
        else:
            m_new = jnp.maximum(m_old, m_cur)
            shift = m_new
        p = jnp.exp2(s - shift).astype(BF16)
        alpha = jnp.exp2(m_old - m_new)
        acc_sc[j] = alpha * acc_sc[j] + jnp.dot(vt_ref[0, kb, :, k0:k0 + size], p, preferred_element_type=F32)
        m_sc[j] = m_new

    def full_tile(src, kb):
        for j in range(ns):
            update(src, j, kb, 0, tq, False)

    def diagonal_tile(src):
        for j in range(ns):
            if j > 0:
                update(src, j, qi, 0, j * ATT_SUB, False)
            update(src, j, qi, j * ATT_SUB, ATT_SUB, True)
        for j in range(ns):
            acc = acc_sc[j]
            o_t = acc[0:LANES, :] / acc[LANES:LANES + 1, :]
            o_ref[j * ATT_SUB:(j + 1) * ATT_SUB, :] = o_t.T.astype(o_ref.dtype)

    logits(s_a, 0)

    def pair(u, carry):
        t = 2 * u
        logits(s_b, t + 1)
        full_tile(s_a, t)
        logits(s_a, t + 2)
        full_tile(s_b, t + 1)
        return carry

    lax.fori_loop(0, qi // 2, pair, 0)

    @pl.when(qi % 2 == 1)
    def _():
        logits(s_b, qi)
        full_tile(s_a, qi - 1)
        diagonal_tile(s_b)

    @pl.when(qi % 2 == 0)
    def _():
        diagonal_tile(s_a)


def _attention(q, k, vt, f_t=None):
    hh, s, dk = q.shape
    _, nt, vrows, tq = vt.shape
    use_f = f_t is not None
    in_specs = [pl.BlockSpec((1, tq, dk), lambda h, i: (h, i, 0)),
                pl.BlockSpec((1, s, dk), lambda h, i: (h, 0, 0)),
                pl.BlockSpec((1, nt, vrows, tq), lambda h, i: (h, 0, 0, 0))]
    args = [q, k, vt]
    if use_f:
        in_specs.append(pl.BlockSpec((FOX_HEADS, tq), lambda h, i: (0, i)))
        args.append(f_t)
    ns = tq // ATT_SUB
    return pl.pallas_call(
        functools.partial(_attn_kernel, tq=tq, use_f=use_f),
        grid=(hh, s // tq),
        in_specs=in_specs,
        out_specs=pl.BlockSpec((tq, LANES), lambda h, i: (i, h)),
        out_shape=jax.ShapeDtypeStruct((s, hh * LANES), BF16),
        scratch_shapes=[pltpu.VMEM((ns, 1, ATT_SUB), F32), pltpu.VMEM((ns, vrows, ATT_SUB), F32),
                        pltpu.VMEM((tq, tq), F32), pltpu.VMEM((tq, tq), F32)],
        compiler_params=_cparams(("parallel", "arbitrary")),
        name="fox_attention" if use_f else "mla_attention",
    )(*args)


def _gla_kernel(q_ref, k_ref, v_ref, r_ref, misc_ref, wa_ref, ba_ref, go_ref, o_ref,
                st_ref, la_ref, b_ref, qe_ref, a_ref, u_ref, dl_ref, stb_ref):
    tb = q_ref.shape[0]
    nchunk = tb // GLA_CHUNK
    nsub = GLA_CHUNK // GLA_SUB
    C = GLA_CHUNK

    @pl.when(pl.program_id(1) == 0)
    def _():
        st_ref[...] = jnp.zeros(st_ref.shape, F32)

    m1, m2, _ = _split3(misc_ref[...])
    w1, w2, _ = _split3(wa_ref[...])
    z = (jnp.dot(m1, w1, preferred_element_type=F32) + jnp.dot(m1, w2, preferred_element_type=F32)
         + jnp.dot(m2, w1, preferred_element_type=F32)) + ba_ref[...]
    la_ref[...] = _log_sigmoid(z) * (LOG2E / GLA_TAU)

    r_i = lax.broadcasted_iota(jnp.int32, (C, C), 0)
    c_i = lax.broadcasted_iota(jnp.int32, (C, C), 1)
    tri = jnp.where(c_i <= r_i, 1.0, 0.0).astype(BF16)
    sub_row = lax.broadcasted_iota(jnp.int32, (GLA_SUB, GLA_DK), 0)
    col_sub = lax.broadcasted_iota(jnp.int32, (GLA_SUB, C), 1)
    scale = GLA_DK ** -0.5
    g_o = go_ref[...]

    chunk_rows = [slice(ci * C, (ci + 1) * C) for ci in range(nchunk)]

    for rows in chunk_rows:
        l1, l2, l3 = _split3(la_ref[rows, :])
        b_ref[rows, :] = (jnp.dot(tri, l1, preferred_element_type=F32) + jnp.dot(tri, l2, preferred_element_type=F32)
                          + jnp.dot(tri, l3, preferred_element_type=F32))

    for ci, rows in enumerate(chunk_rows):
        b = b_ref[rows, :]
        qc = q_ref[rows, :].astype(F32) * scale
        kc = k_ref[rows, :].astype(F32)
        b_last = b[C - 1:C, :]
        qe_ref[rows, :] = (qc * jnp.exp2(b)).astype(BF16)
        ke = (kc * jnp.exp2(b_last - b)).astype(BF16)
        u_ref[ci] = lax.dot_general(v_ref[rows, :], ke, (((0,), (0,)), ((), ())), preferred_element_type=F32)
        dl_ref[ci] = jnp.exp2(b_last)

        a_rows = []
        for si in range(nsub):
            lo = si * GLA_SUB
            qs, ks, bs = qc[lo:lo + GLA_SUB], kc[lo:lo + GLA_SUB], b[lo:lo + GLA_SUB]
            a = jnp.zeros((GLA_SUB, C), F32)
            for j in range(GLA_SUB):
                e = jnp.exp2(jnp.where(sub_row >= j, bs - bs[j:j + 1, :], NEG_INF))
                w = jnp.sum(qs * (ks[j:j + 1, :] * e), axis=-1, keepdims=True)
                a = jnp.where(col_sub == lo + j, w, a)
            if si > 0:
                ref_row = b[lo - 1:lo, :]
                q_i = qs * jnp.exp2(bs - ref_row)
                k_j = jnp.concatenate([kc[0:lo] * jnp.exp2(ref_row - b[0:lo]),
                                       jnp.zeros((C - lo, GLA_DK), F32)], axis=0)
                a = a + lax.dot_general(q_i.astype(BF16), k_j.astype(BF16), (((1,), (1,)), ((), ())),
                                        preferred_element_type=F32)
            a_rows.append(a)
        a_ref[rows, :] = jnp.concatenate(a_rows, axis=0).astype(BF16)

    st = st_ref[...]
    for ci in range(nchunk):
        stb_ref[ci] = st.astype(BF16)
        st = st * dl_ref[ci] + u_ref[ci]
    st_ref[...] = st

    for ci, rows in enumerate(chunk_rows):
        o = lax.dot_general(qe_ref[rows, :], stb_ref[ci], (((1,), (1,)), ((), ())), preferred_element_type=F32)
        o = o + jnp.dot(a_ref[rows, :], v_ref[rows, :], preferred_element_type=F32)
        on = o * lax.rsqrt(jnp.mean(o * o, axis=-1, keepdims=True) + EPS) * g_o
        r = r_ref[rows, :].astype(F32)
        o_ref[rows, :] = (on * (r * jax.nn.sigmoid(r))).astype(o_ref.dtype)


def _gla(pb, pa, wa_p, b_a, g_o, tb=GLA_BLOCK):
    s = pb.shape[0]
    nchunk = tb // GLA_CHUNK
    return pl.pallas_call(
        _gla_kernel,
        grid=(GLA_HEADS, s // tb),
        in_specs=[pl.BlockSpec((tb, GLA_DK), lambda h, i: (i, B_GQ // GLA_DK + h)),
                  pl.BlockSpec((tb, GLA_DK), lambda h, i: (i, B_GK // GLA_DK + h)),
                  pl.BlockSpec((tb, GLA_DV), lambda h, i: (i, B_GV // GLA_DV + h)),
                  pl.BlockSpec((tb, GLA_DV), lambda h, i: (i, B_GR // GLA_DV + h)),
                  pl.BlockSpec((tb, LANES), lambda h, i: (i, A_MISC // LANES)),
                  pl.BlockSpec((LANES, GLA_DK), lambda h, i: (0, h)),
                  pl.BlockSpec((1, GLA_DK), lambda h, i: (0, h)),
                  pl.BlockSpec((1, GLA_DV), lambda h, i: (0, 0))],
        out_specs=pl.BlockSpec((tb, GLA_DV), lambda h, i: (i, h)),
        out_shape=jax.ShapeDtypeStruct((s, GLA_HEADS * GLA_DV), BF16),
        scratch_shapes=[pltpu.VMEM((GLA_DV, GLA_DK), F32),
                        pltpu.VMEM((tb, GLA_DK), F32),
                        pltpu.VMEM((tb, GLA_DK), F32),
                        pltpu.VMEM((tb, GLA_DK), BF16),
                        pltpu.VMEM((tb, GLA_CHUNK), BF16),
                        pltpu.VMEM((nchunk, GLA_DV, GLA_DK), F32),
                        pltpu.VMEM((nchunk, 1, GLA_DK), F32),
                        pltpu.VMEM((nchunk, GLA_DV, GLA_DK), BF16)],
        compiler_params=_cparams(("parallel", "arbitrary")),
        name="gla",
    )(pb, pb, pb, pb, pa, wa_p, b_a, g_o)


def _group_w_in_kernel(w_ref, wa_ref, wb_ref, wg_ref, wfvt_ref):
    off = IN_OFF
    col = lambda i: w_ref[:, off[i]:off[i + 1]].astype(BF16)
    rows = w_ref.shape[0]
    wa_ref[:, A_CQ:A_KR] = w_ref[:, off[0]:off[2]].astype(BF16)
    wa_ref[:, A_KR:A_MISC] = jnp.concatenate([col(2), jnp.zeros((rows, LANES - MLA_ROPE), BF16)], axis=1)
    wa_ref[:, A_MISC:A_WIDTH] = jnp.concatenate(
        [col(11), col(6), jnp.zeros((rows, LANES - FOX_HEADS - GLA_GATE_RANK), BF16)], axis=1)
    wb_ref[:, B_GQ:B_GR] = w_ref[:, off[3]:off[6]].astype(BF16)
    wb_ref[:, B_GR:B_WIDTH] = w_ref[:, off[7]:off[10]].astype(BF16)
    wg_ref[...] = w_ref[:, off[12]:off[13]].astype(BF16)
    wfvt_ref[...] = w_ref[:, off[10]:off[11]].T.astype(BF16)


def _group_w_in(w_all, layer, tk=LANES):
    _, d, n = w_all.shape
    ng = N_BRANCH * D_MODEL
    nv = FOX_HEADS * FOX_DH
    return pl.pallas_call(
        _group_w_in_kernel,
        grid=(d // tk,),
        in_specs=[pl.BlockSpec((None, tk, n), lambda i: (layer, i, 0))],
        out_specs=[pl.BlockSpec((tk, A_WIDTH), lambda i: (i, 0)),
                   pl.BlockSpec((tk, B_WIDTH), lambda i: (i, 0)),
                   pl.BlockSpec((tk, ng), lambda i: (i, 0)),
                   pl.BlockSpec((nv, tk), lambda i: (0, i))],
        out_shape=[jax.ShapeDtypeStruct((d, A_WIDTH), BF16),
                   jax.ShapeDtypeStruct((d, B_WIDTH), BF16),
                   jax.ShapeDtypeStruct((d, ng), BF16),
                   jax.ShapeDtypeStruct((nv, d), BF16)],
        compiler_params=_cparams(("parallel",)),
        name="group_w_in",
    )(w_all)


def _pad_lanes(v, n):
    return jnp.pad(v, (0, n - v.shape[0])).reshape(1, n)


def _layer(x, h, tabs, g_next, w_in_all, layer, g_cq, w_uq, g_ckv, w_ukv, g_mla_q, g_mla_k, w_a2, b_a, g_gla_o,
           g_fox_q, g_fox_k, b_f, w_branch, w_out, g_ffn, w_gu_all, w_down):
    wa, wb, wg, wfv_t = _group_w_in(w_in_all, layer)
    pa = _matmul(h, wa, F32, 1024, A_WIDTH, name="in_proj_a")
    pb = _matmul(h, wb, BF16, 1024, 1024, name="in_proj_b")
    gsig = _matmul(h, wg, BF16, 1024, 1024, act="sigmoid", name="in_proj_gates")

    wuq_p = jnp.pad(w_uq.reshape(MLA_RANK, MLA_HEADS, MLA_QK),
                    ((0, 0), (0, 0), (0, HEAD_PAD - MLA_QK))).reshape(MLA_RANK, MLA_HEADS * HEAD_PAD).astype(BF16)
    w_ukv3 = w_ukv.reshape(MLA_RANK, MLA_HEADS, MLA_NOPE + MLA_V)
    wuk = w_ukv3[:, :, :MLA_NOPE].reshape(MLA_RANK, MLA_HEADS * MLA_NOPE).astype(BF16)
    wuv_t = w_ukv3[:, :, MLA_NOPE:].reshape(MLA_RANK, MLA_HEADS * MLA_V).T.astype(BF16)
    q_a, k_a, vt_a = _mla_prep(pa, wuq_p, wuk, wuv_t, g_cq.reshape(1, -1), g_ckv.reshape(1, -1),
                               _pad_lanes(g_mla_q, HEAD_PAD), g_mla_k[:MLA_NOPE].reshape(1, -1),
                               _pad_lanes(g_mla_k[MLA_NOPE:], LANES), tabs)
    o_a = _attention(q_a, k_a, vt_a)

    wa_p = jnp.zeros((LANES, GLA_HEADS * GLA_DK), F32).at[MISC_GA:MISC_GA + GLA_GATE_RANK].set(w_a2)
    o_b = _gla(pb, pa, wa_p, b_a.reshape(1, -1), g_gla_o.reshape(1, -1))

    f_cum, f_cum_t = _fox_cumsum(pa, _pad_lanes(b_f, LANES))
    q_c, k_c = _fox_prep(pb, f_cum, g_fox_q.reshape(1, -1), g_fox_k.reshape(1, -1))
    vt_c = _value_t_proj(h, wfv_t)
    o_c = _attention(q_c, k_c, vt_c, f_cum_t)

    merged = _merge(o_a, o_b, o_c, w_branch.reshape(N_BRANCH * BRANCH_WIDTH, D_MODEL).astype(BF16), gsig, 1024, 512)
    x1, h2 = _matmul_residual(merged, w_out.astype(BF16), x, 512, "out_proj", g_next=g_ffn)
    act = _swiglu_up(h2, w_gu_all, layer, 1024, 512)
    if g_next is None:
        return _matmul_residual(act, w_down.astype(BF16), x1, 256, "ffn_down"), None
    return _matmul_residual(act, w_down.astype(BF16), x1, 256, "ffn_down", g_next=g_next)


def kernel(x, positions, g_mix, w_in, g_cq, w_uq, g_ckv, w_ukv, g_mla_q, g_mla_k, w_a2, b_a, g_gla_o,
           g_fox_q, g_fox_k, b_f, w_branch, w_out, g_ffn, w_gu, w_down):
    bsz, s, d = x.shape
    assert bsz == 1 and d == D_MODEL and s % ATT_TILE == 0
    half = MLA_ROPE // 2
    inv = ROPE_THETA ** (-jnp.arange(half, dtype=F32) / half)
    inv128 = jnp.concatenate([inv, inv, jnp.zeros((LANES - MLA_ROPE,), F32)]).reshape(1, LANES)
    tabs = _rope_tables(positions.reshape(s, 1), inv128)
    depth = w_in.shape[0]
    xcur = x.reshape(s, d)
    hcur = _rmsnorm(xcur, g_mix[0])
    for l in range(depth):
        g_next = g_mix[l + 1] if l + 1 < depth else None
        xcur, hcur = _layer(xcur, hcur, tabs, g_next, w_in, l, g_cq[l], w_uq[l], g_ckv[l], w_ukv[l], g_mla_q[l],
                            g_mla_k[l], w_a2[l], b_a[l], g_gla_o[l], g_fox_q[l], g_fox_k[l], b_f[l], w_branch[l],
                            w_out[l], g_ffn[l], w_gu, w_down[l])
    return xcur.reshape(bsz, s, d)
```

```python
import functools

import jax
import jax.numpy as jnp
from jax import lax
from jax.experimental import pallas as pl
from jax.experimental.pallas import tpu as pltpu

F32 = jnp.float32
BF16 = jnp.bfloat16

D_MODEL = 2048
BRANCH_WIDTH = D_MODEL // 2
N_BRANCH = 3
MLA_NOPE = 128
MLA_ROPE = 64
MLA_V = 128
MLA_HEADS = BRANCH_WIDTH // MLA_V
MLA_QK = MLA_NOPE + MLA_ROPE
MLA_RANK = 512
ROPE_THETA = 10000.0
GLA_HEADS = 4
GLA_DV = BRANCH_WIDTH // GLA_HEADS
GLA_DK = GLA_DV // 2
GLA_GATE_RANK = 16
GLA_TAU = 16.0
FOX_DH = 128
FOX_HEADS = BRANCH_WIDTH // FOX_DH
FFN_HIDDEN = ((8 * D_MODEL + 3 * 256 - 1) // (3 * 256)) * 256
EPS = 1e-6
NEG_INF = -1e30

LANES = 128
F32_SUBLANES = 8
BF16_SUBLANES = 16
HEAD_PAD = 256
VMEM_LIMIT = 48 * 1024 * 1024
VMEM_LIMIT_BIG = 56 * 1024 * 1024

IN_SPLIT = (MLA_RANK, MLA_RANK, MLA_ROPE, GLA_HEADS * GLA_DK, GLA_HEADS * GLA_DK, GLA_HEADS * GLA_DV, GLA_GATE_RANK,
            GLA_HEADS * GLA_DV, FOX_HEADS * FOX_DH, FOX_HEADS * FOX_DH, FOX_HEADS * FOX_DH, FOX_HEADS,
            N_BRANCH * D_MODEL)
IN_OFF = tuple(sum(IN_SPLIT[:i]) for i in range(len(IN_SPLIT) + 1))

A_CQ, A_CKV = 0, 512
A_LATENT = 1024
A_KR, A_MISC = 0, 128
A_SMALL = 256
MISC_FL = 0
MISC_GA = FOX_HEADS
B_GQ, B_GK, B_GV, B_GR, B_FQ, B_FK = 0, 512, 1024, 2048, 3072, 4096
B_WIDTH = 5120

ATT_TILE = 1024
ATT_SUB = 256
ATT_VROWS = LANES + BF16_SUBLANES
LOG2E = 1.4426950408889634

GLA_CHUNK = 64
GLA_SUB = 8
GLA_BLOCK = 512


def _aligned(start, m):
    return start if isinstance(start, int) else pl.multiple_of(start, m)


def _cparams(sem):
    return pltpu.CompilerParams(dimension_semantics=sem, vmem_limit_bytes=VMEM_LIMIT)


def _split3(x):
    x1 = x.astype(BF16)
    r1 = x - x1.astype(F32)
    x2 = r1.astype(BF16)
    x3 = (r1 - x2.astype(F32)).astype(BF16)
    return x1, x2, x3


def _log_sigmoid(z):
    return -(jnp.maximum(-z, 0.0) + jnp.log1p(jnp.exp(-jnp.abs(z))))


def _store_values_t(vt_out, vt):
    rows = vt.shape[1]
    r = lax.broadcasted_iota(jnp.int32, (BF16_SUBLANES, rows), 0)
    ones_rows = jnp.where(r == 0, 1.0, 0.0).astype(BF16)
    for h in range(vt.shape[0] // LANES):
        vt_out[h, 0, 0:LANES, :] = vt[h * LANES:(h + 1) * LANES, :].astype(BF16)
        vt_out[h, 0, LANES:ATT_VROWS, :] = ones_rows


def _rmsnorm_kernel(x_ref, g_ref, o_ref):
    x = x_ref[...]
    ms = jnp.mean(x * x, axis=-1, keepdims=True)
    o_ref[...] = (x * lax.rsqrt(ms + EPS) * g_ref[...]).astype(o_ref.dtype)


def _rmsnorm(x, g, tm=512):
    m, d = x.shape
    return pl.pallas_call(
        _rmsnorm_kernel,
        grid=(m // tm,),
        in_specs=[pl.BlockSpec((tm, d), lambda i: (i, 0)),
                  pl.BlockSpec((1, d), lambda i: (0, 0))],
        out_specs=pl.BlockSpec((tm, d), lambda i: (i, 0)),
        out_shape=jax.ShapeDtypeStruct((m, d), BF16),
        compiler_params=_cparams(("parallel",)),
        name="rmsnorm",
    )(x, g.reshape(1, d))


def _mm_nt_kernel(a_ref, wt_ref, o_ref, *, act):
    y = lax.dot_general(a_ref[...], wt_ref[...], (((1,), (1,)), ((), ())), preferred_element_type=F32)
    if act == "sigmoid":
        y = jax.nn.sigmoid(y)
    o_ref[...] = y.astype(o_ref.dtype)


def _matmul_nt(a, w_t, out_dtype, tm, tn, act=None, name="matmul"):
    m, k = a.shape
    n = w_t.shape[0]
    return pl.pallas_call(
        functools.partial(_mm_nt_kernel, act=act),
        grid=(m // tm, n // tn),
        in_specs=[pl.BlockSpec((tm, k), lambda i, j: (i, 0)),
                  pl.BlockSpec((tn, k), lambda i, j: (j, 0))],
        out_specs=pl.BlockSpec((tm, tn), lambda i, j: (i, j)),
        out_shape=jax.ShapeDtypeStruct((m, n), out_dtype),
        compiler_params=_cparams(("parallel", "arbitrary")),
        name=name,
    )(a, w_t)


def _mm_res_kernel(*refs, with_norm):
    if with_norm:
        a_ref, w_ref, r_ref, g_ref, o_ref, h_ref = refs
    else:
        a_ref, w_ref, r_ref, o_ref = refs
    y = r_ref[...] + jnp.dot(a_ref[...], w_ref[...], preferred_element_type=F32)
    o_ref[...] = y
    if with_norm:
        ms = jnp.mean(y * y, axis=-1, keepdims=True)
        h_ref[...] = (y * lax.rsqrt(ms + EPS) * g_ref[...]).astype(h_ref.dtype)


def _matmul_residual(a, w_all, layer, res, tm, name, g_next=None):
    m, k = a.shape
    n = w_all.shape[2]
    with_norm = g_next is not None
    row = pl.BlockSpec((tm, n), lambda i: (i, 0))
    in_specs = [pl.BlockSpec((tm, k), lambda i: (i, 0)),
                pl.BlockSpec((None, k, n), lambda i: (layer, 0, 0), pipeline_mode=pl.Buffered(1)),
                row]
    args = [a, w_all, res]
    out_specs, out_shape = row, jax.ShapeDtypeStruct((m, n), F32)
    if with_norm:
        in_specs.append(pl.BlockSpec((1, n), lambda i: (0, 0)))
        args.append(g_next.reshape(1, n))
        out_specs, out_shape = [row, row], [out_shape, jax.ShapeDtypeStruct((m, n), BF16)]
    return pl.pallas_call(
        functools.partial(_mm_res_kernel, with_norm=with_norm),
        grid=(m // tm,),
        in_specs=in_specs,
        out_specs=out_specs,
        out_shape=out_shape,
        compiler_params=pltpu.CompilerParams(dimension_semantics=("parallel",), vmem_limit_bytes=VMEM_LIMIT_BIG),
        name=name,
    )(*args)


def _swiglu_kernel(a_ref, wg_ref, wu_ref, o_ref, wg_sc, wu_sc):
    @pl.when(pl.program_id(1) == 0)
    def _():
        wg_sc[...] = wg_ref[...].astype(BF16)
        wu_sc[...] = wu_ref[...].astype(BF16)

    a = a_ref[...]
    g = jnp.dot(a, wg_sc[...], preferred_element_type=F32)
    u = jnp.dot(a, wu_sc[...], preferred_element_type=F32)
    o_ref[...] = (g * jax.nn.sigmoid(g) * u).astype(o_ref.dtype)


def _swiglu_up(a, w_gu_all, layer, tm, tn):
    m, k = a.shape
    hidden = w_gu_all.shape[2] // 2
    nj = hidden // tn
    return pl.pallas_call(
        _swiglu_kernel,
        grid=(nj, m // tm),
        in_specs=[pl.BlockSpec((tm, k), lambda j, i: (i, 0)),
                  pl.BlockSpec((None, k, tn), lambda j, i: (layer, 0, j)),
                  pl.BlockSpec((None, k, tn), lambda j, i: (layer, 0, j + nj))],
        out_specs=pl.BlockSpec((tm, tn), lambda j, i: (i, j)),
        out_shape=jax.ShapeDtypeStruct((m, hidden), BF16),
        scratch_shapes=[pltpu.VMEM((k, tn), BF16), pltpu.VMEM((k, tn), BF16)],
        compiler_params=_cparams(("parallel", "arbitrary")),
        name="ffn_up",
    )(a, w_gu_all, w_gu_all)


def _merge_kernel(oa_ref, ob_ref, oc_ref, wa_ref, wb_ref, wc_ref, ga_ref, gb_ref, gc_ref, o_ref):
    acc = jnp.dot(oa_ref[...], wa_ref[...], preferred_element_type=F32) * ga_ref[...].astype(F32)
    acc += jnp.dot(ob_ref[...], wb_ref[...], preferred_element_type=F32) * gb_ref[...].astype(F32)
    acc += jnp.dot(oc_ref[...], wc_ref[...], preferred_element_type=F32) * gc_ref[...].astype(F32)
    o_ref[...] = acc.astype(o_ref.dtype)


def _merge(o_a, o_b, o_c, w_branch_all, layer, gsig, tm, tn):
    m, bw = o_a.shape
    d = w_branch_all.shape[2]
    nj = d // tn
    o_spec = pl.BlockSpec((tm, bw), lambda i, j: (i, 0))

    def w_spec(n):
        return pl.BlockSpec((None, bw, tn), lambda i, j: (layer, n, j))

    def g_spec(n):
        return pl.BlockSpec((tm, tn), lambda i, j: (i, n * nj + j))

    return pl.pallas_call(
        _merge_kernel,
        grid=(m // tm, nj),
        in_specs=[o_spec, o_spec, o_spec, w_spec(0), w_spec(1), w_spec(2), g_spec(0), g_spec(1), g_spec(2)],
        out_specs=pl.BlockSpec((tm, tn), lambda i, j: (i, j)),
        out_shape=jax.ShapeDtypeStruct((m, d), BF16),
        compiler_params=_cparams(("parallel", "arbitrary")),
        name="branch_merge",
    )(o_a, o_b, o_c, w_branch_all, w_branch_all, w_branch_all, gsig, gsig, gsig)


def _rope_table_kernel(pos_ref, inv_ref, c_ref, s1_ref, s2_ref):
    ang = pos_ref[...].astype(F32) * inv_ref[...]
    c = jnp.cos(ang)
    s = jnp.sin(ang)
    lane = lax.broadcasted_iota(jnp.int32, ang.shape, 1)
    half = MLA_ROPE // 2
    c_ref[...] = jnp.where(lane < MLA_ROPE, c, 0.0)
    s1_ref[...] = jnp.where((lane >= half) & (lane < MLA_ROPE), s, 0.0)
    s2_ref[...] = jnp.where(lane < half, -s, 0.0)


def _rope_tables(pos_col, inv128, tm=512):
    s = pos_col.shape[0]
    spec = pl.BlockSpec((tm, LANES), lambda i: (i, 0))
    shp = jax.ShapeDtypeStruct((s, LANES), F32)
    return pl.pallas_call(
        _rope_table_kernel,
        grid=(s // tm,),
        in_specs=[pl.BlockSpec((tm, 1), lambda i: (i, 0)),
                  pl.BlockSpec((1, LANES), lambda i: (0, 0))],
        out_specs=[spec, spec, spec],
        out_shape=[shp, shp, shp],
        compiler_params=_cparams(("parallel",)),
        name="rope_tables",
    )(pos_col, inv128)


def _rope128(x, c, s1, s2):
    half = MLA_ROPE // 2
    return x * c + pltpu.roll(x, half, 1) * s1 + pltpu.roll(x, LANES - half, 1) * s2


def _mla_prep_kernel(cq_ref, ckv_ref, kr_ref, wuq_ref, wuk_ref, wuvt_ref, gcq_ref, gckv_ref, gq_ref, gk_ref, gkr_ref,
                     c_ref, s1_ref, s2_ref, q_out, k_out, vt_out):
    def norm(x, g):
        return x * lax.rsqrt(jnp.mean(x * x, axis=-1, keepdims=True) + EPS) * g

    cqn = norm(cq_ref[...], gcq_ref[...]).astype(BF16)
    ckvn = norm(ckv_ref[...], gckv_ref[...]).astype(BF16)
    qa = jnp.dot(cqn, wuq_ref[...], preferred_element_type=F32)
    ka = jnp.dot(ckvn, wuk_ref[...], preferred_element_type=F32)
    _store_values_t(vt_out, lax.dot_general(wuvt_ref[...], ckvn, (((1,), (1,)), ((), ())),
                                            preferred_element_type=F32))
    krp = kr_ref[...]
    kr_ss = jnp.sum(krp * krp, axis=-1, keepdims=True)
    c, s1, s2 = c_ref[...], s1_ref[...], s2_ref[...]
    gq, gk, gkr = gq_ref[...], gk_ref[...], gkr_ref[...]
    scale = MLA_QK ** -0.5 * LOG2E
    kr_roped = _rope128(krp * gkr, c, s1, s2)
    for h in range(MLA_HEADS):
        qh = qa[:, h * HEAD_PAD:(h + 1) * HEAD_PAD]
        rq = lax.rsqrt(jnp.sum(qh * qh, axis=-1, keepdims=True) * (1.0 / MLA_QK) + EPS)
        qn = qh * rq * gq
        q_out[h, :, 0:LANES] = (qn[:, 0:LANES] * scale).astype(BF16)
        q_out[h, :, LANES:HEAD_PAD] = (_rope128(qn[:, LANES:HEAD_PAD], c, s1, s2) * scale).astype(BF16)
        kn = ka[:, h * MLA_NOPE:(h + 1) * MLA_NOPE]
        rk = lax.rsqrt((jnp.sum(kn * kn, axis=-1, keepdims=True) + kr_ss) * (1.0 / MLA_QK) + EPS)
        k_out[h, :, 0:LANES] = (kn * rk * gk).astype(BF16)
        k_out[h, :, LANES:HEAD_PAD] = (kr_roped * rk).astype(BF16)


def _vt_spec(heads, tm):
    per_tile = ATT_TILE // tm
    return pl.BlockSpec((heads, 1, ATT_VROWS, tm), lambda i: (0, i // per_tile, 0, i % per_tile))


def _mla_prep(p_latent, p_small, wuq_p, wuk, wuv_t, g_cq, g_ckv, g_q, g_k, g_kr, tabs, tm=256):
    s = p_latent.shape[0]
    h = MLA_HEADS
    row = lambda w: pl.BlockSpec((1, w), lambda i: (0, 0))
    tab = pl.BlockSpec((tm, LANES), lambda i: (i, 0))
    return pl.pallas_call(
        _mla_prep_kernel,
        grid=(s // tm,),
        in_specs=[pl.BlockSpec((tm, MLA_RANK), lambda i: (i, A_CQ // MLA_RANK)),
                  pl.BlockSpec((tm, MLA_RANK), lambda i: (i, A_CKV // MLA_RANK)),
                  pl.BlockSpec((tm, LANES), lambda i: (i, A_KR // LANES)),
                  pl.BlockSpec(wuq_p.shape, lambda i: (0, 0)),
                  pl.BlockSpec(wuk.shape, lambda i: (0, 0)),
                  pl.BlockSpec(wuv_t.shape, lambda i: (0, 0)),
                  row(MLA_RANK), row(MLA_RANK), row(HEAD_PAD), row(LANES), row(LANES),
                  tab, tab, tab],
        out_specs=[pl.BlockSpec((h, tm, HEAD_PAD), lambda i: (0, i, 0)),
                   pl.BlockSpec((h, tm, HEAD_PAD), lambda i: (0, i, 0)),
                   _vt_spec(h, tm)],
        out_shape=[jax.ShapeDtypeStruct((h, s, HEAD_PAD), BF16),
                   jax.ShapeDtypeStruct((h, s, HEAD_PAD), BF16),
                   jax.ShapeDtypeStruct((h, s // ATT_TILE, ATT_VROWS, ATT_TILE), BF16)],
        compiler_params=_cparams(("parallel",)),
        name="mla_prep",
    )(p_latent, p_latent, p_small, wuq_p, wuk, wuv_t, g_cq, g_ckv, g_q, g_k, g_kr, *tabs)


def _fox_cumsum_kernel(misc_ref, bf_ref, f_ref, ft_ref, carry_ref):
    blk = misc_ref.shape[0]

    @pl.when(pl.program_id(0) == 0)
    def _():
        carry_ref[...] = jnp.zeros(carry_ref.shape, F32)

    r = lax.broadcasted_iota(jnp.int32, (blk, blk), 0)
    c = lax.broadcasted_iota(jnp.int32, (blk, blk), 1)
    tri = jnp.where(c <= r, 1.0, 0.0).astype(BF16)
    x1, x2, x3 = _split3(_log_sigmoid(misc_ref[...] + bf_ref[...]))
    cs = (jnp.dot(tri, x1, preferred_element_type=F32)
          + jnp.dot(tri, x2, preferred_element_type=F32)
          + jnp.dot(tri, x3, preferred_element_type=F32)) + carry_ref[...]
    f_ref[...] = cs
    ft_ref[...] = cs.T
    carry_ref[...] = cs[blk - 1:blk, :]


def _fox_cumsum(p_small, bf128, blk=512):
    s = p_small.shape[0]
    return pl.pallas_call(
        _fox_cumsum_kernel,
        grid=(s // blk,),
        in_specs=[pl.BlockSpec((blk, LANES), lambda i: (i, A_MISC // LANES)),
                  pl.BlockSpec((1, LANES), lambda i: (0, 0))],
        out_specs=[pl.BlockSpec((blk, LANES), lambda i: (i, 0)),
                   pl.BlockSpec((LANES, blk), lambda i: (0, i))],
        out_shape=[jax.ShapeDtypeStruct((s, LANES), F32), jax.ShapeDtypeStruct((LANES, s), F32)],
        scratch_shapes=[pltpu.VMEM((1, LANES), F32)],
        compiler_params=_cparams(("arbitrary",)),
        name="fox_cumsum",
    )(p_small, bf128)


def _fox_prep_kernel(fq_ref, fk_ref, f_ref, gq_ref, gk_ref, sel_ref, q_out, k_out):
    tm = fq_ref.shape[0]
    lane = lax.broadcasted_iota(jnp.int32, (tm, LANES), 1)
    ones_tail = jnp.where(lane < 3, 1.0, 0.0).astype(BF16)
    pieces = jnp.concatenate(_split3(f_ref[...] * (-LOG2E)), axis=1)
    scale = FOX_DH ** -0.5 * LOG2E
    gq, gk = gq_ref[...], gk_ref[...]
    for h in range(FOX_HEADS):
        sl = slice(h * FOX_DH, (h + 1) * FOX_DH)
        q = fq_ref[:, sl].astype(F32)
        k = fk_ref[:, sl].astype(F32)
        qn = q * lax.rsqrt(jnp.mean(q * q, axis=-1, keepdims=True) + EPS) * gq
        kn = k * lax.rsqrt(jnp.mean(k * k, axis=-1, keepdims=True) + EPS) * gk
        q_out[h, :, 0:LANES] = (qn * scale).astype(BF16)
        q_out[h, :, LANES:HEAD_PAD] = ones_tail
        k_out[h, :, 0:LANES] = kn.astype(BF16)
        k_out[h, :, LANES:HEAD_PAD] = jnp.dot(pieces, sel_ref[h], preferred_element_type=F32).astype(BF16)


def _fox_prep(pb, f_cum, g_q, g_k, tm=256):
    s = pb.shape[0]
    h = FOX_HEADS
    w = FOX_HEADS * FOX_DH
    row = pl.BlockSpec((1, FOX_DH), lambda i: (0, 0))
    hh, pp = jnp.meshgrid(jnp.arange(h), jnp.arange(3), indexing="ij")
    sel = jnp.zeros((h, 3 * LANES, LANES), BF16).at[hh, pp * LANES + hh, pp].set(1.0)
    return pl.pallas_call(
        _fox_prep_kernel,
        grid=(s // tm,),
        in_specs=[pl.BlockSpec((tm, w), lambda i: (i, B_FQ // w)),
                  pl.BlockSpec((tm, w), lambda i: (i, B_FK // w)),
                  pl.BlockSpec((tm, LANES), lambda i: (i, 0)),
                  row, row,
                  pl.BlockSpec(sel.shape, lambda i: (0, 0, 0))],
        out_specs=[pl.BlockSpec((h, tm, HEAD_PAD), lambda i: (0, i, 0)),
                   pl.BlockSpec((h, tm, HEAD_PAD), lambda i: (0, i, 0))],
        out_shape=[jax.ShapeDtypeStruct((h, s, HEAD_PAD), BF16),
                   jax.ShapeDtypeStruct((h, s, HEAD_PAD), BF16)],
        compiler_params=_cparams(("parallel",)),
        name="fox_prep",
    )(pb, pb, f_cum, g_q, g_k, sel)


def _value_t_kernel(h_ref, w_ref, vt_out, w_sc):
    @pl.when(pl.program_id(0) == 0)
    def _():
        w_sc[...] = w_ref[0].astype(BF16)

    vt = lax.dot_general(w_sc[...], h_ref[...], (((1,), (1,)), ((), ())), preferred_element_type=F32)
    _store_values_t(vt_out, vt)


def _value_t_proj(h, w_all_t, layer, row_start, heads, tm=512):
    s, d = h.shape
    nv = heads * LANES
    return pl.pallas_call(
        _value_t_kernel,
        grid=(s // tm,),
        in_specs=[pl.BlockSpec((tm, d), lambda i: (i, 0)),
                  pl.BlockSpec((pl.Element(1), pl.Element(nv), pl.Element(d)), lambda i: (layer, row_start, 0))],
        out_specs=_vt_spec(heads, tm),
        out_shape=jax.ShapeDtypeStruct((heads, s // ATT_TILE, ATT_VROWS, ATT_TILE), BF16),
        scratch_shapes=[pltpu.VMEM((nv, d), BF16)],
        compiler_params=_cparams(("arbitrary",)),
        name="fox_value_t",
    )(h, w_all_t)


def _attn_kernel(*refs, tq, use_f):
    if use_f:
        q_ref, k_ref, vt_ref, f_ref, o_ref, m_sc, acc_sc, s_a, s_b = refs
    else:
        q_ref, k_ref, vt_ref, o_ref, m_sc, acc_sc, s_a, s_b = refs
    h = pl.program_id(0)
    qi = pl.program_id(1)
    ns = tq // ATT_SUB
    if use_f:
        ft = [f_ref[pl.ds(h, 1), j * ATT_SUB:(j + 1) * ATT_SUB] * LOG2E for j in range(ns)]
    m_sc[...] = jnp.full(m_sc.shape, NEG_INF, F32)
    acc_sc[...] = jnp.zeros(acc_sc.shape, F32)

    def logits(dst, kb):
        dst[...] = lax.dot_general(k_ref[0, pl.ds(_aligned(kb * tq, tq), tq), :], q_ref[0],
                                   (((1,), (1,)), ((), ())), preferred_element_type=F32)

    def update(src, j, kb, k0, size, causal):
        s = src[k0:k0 + size, j * ATT_SUB:(j + 1) * ATT_SUB]
        if causal:
            key = lax.broadcasted_iota(jnp.int32, (size, ATT_SUB), 0)
            qry = lax.broadcasted_iota(jnp.int32, (size, ATT_SUB), 1)
            s = jnp.where(key <= qry, s, NEG_INF)
        m_old = m_sc[j]
        m_cur = jnp.max(s, axis=0, keepdims=True)
        if use_f:
            m_new = jnp.maximum(m_old, m_cur + ft[j])
            shift = m_new - ft[j]
        else:
            m_new = jnp.maximum(m_old, m_cur)
            shift = m_new
        p = jnp.exp2(s - shift).astype(BF16)
        alpha = jnp.exp2(m_old - m_new)
        acc_sc[j] = alpha * acc_sc[j] + jnp.dot(vt_ref[0, kb, :, k0:k0 + size], p, preferred_element_type=F32)
        m_sc[j] = m_new

    def full_tile(src, kb):
        for j in range(ns):
            update(src, j, kb, 0, tq, False)

    def diagonal_tile(src):
        for j in range(ns):
            if j > 0:
                update(src, j, qi, 0, j * ATT_SUB, False)
            update(src, j, qi, j * ATT_SUB, ATT_SUB, True)
        for j in range(ns):
            acc = acc_sc[j]
            o_t = acc[0:LANES, :] / acc[LANES:LANES + 1, :]
            o_ref[j * ATT_SUB:(j + 1) * ATT_SUB, :] = o_t.T.astype(o_ref.dtype)

    logits(s_a, 0)

    def pair(u, carry):
        t = 2 * u
        logits(s_b, t + 1)
        full_tile(s_a, t)
        logits(s_a, t + 2)
        full_tile(s_b, t + 1)
        return carry

    lax.fori_loop(0, qi // 2, pair, 0)

    @pl.when(qi % 2 == 1)
    def _():
        logits(s_b, qi)
        full_tile(s_a, qi - 1)
        diagonal_tile(s_b)

    @pl.when(qi % 2 == 0)
    def _():
        diagonal_tile(s_a)


def _attention(q, k, vt, f_t=None):
    hh, s, dk = q.shape
    _, nt, vrows, tq = vt.shape
    use_f = f_t is not None
    in_specs = [pl.BlockSpec((1, tq, dk), lambda h, i: (h, i, 0)),
                pl.BlockSpec((1, s, dk), lambda h, i: (h, 0, 0)),
                pl.BlockSpec((1, nt, vrows, tq), lambda h, i: (h, 0, 0, 0))]
    args = [q, k, vt]
    if use_f:
        in_specs.append(pl.BlockSpec((FOX_HEADS, tq), lambda h, i: (0, i)))
        args.append(f_t)
    ns = tq // ATT_SUB
    return pl.pallas_call(
        functools.partial(_attn_kernel, tq=tq, use_f=use_f),
        grid=(hh, s // tq),
        in_specs=in_specs,
        out_specs=pl.BlockSpec((tq, LANES), lambda h, i: (i, h)),
        out_shape=jax.ShapeDtypeStruct((s, hh * LANES), BF16),
        scratch_shapes=[pltpu.VMEM((ns, 1, ATT_SUB), F32), pltpu.VMEM((ns, vrows, ATT_SUB), F32),
                        pltpu.VMEM((tq, tq), F32), pltpu.VMEM((tq, tq), F32)],
        compiler_params=_cparams(("parallel", "arbitrary")),
        name="fox_attention" if use_f else "mla_attention",
    )(*args)


def _gla_kernel(q_ref, k_ref, v_ref, r_ref, misc_ref, wa_ref, ba_ref, go_ref, o_ref,
                st_ref, la_ref, b_ref, qe_ref, a_ref, u_ref, dl_ref, stb_ref):
    tb = q_ref.shape[0]
    nchunk = tb // GLA_CHUNK
    nsub = GLA_CHUNK // GLA_SUB
    C = GLA_CHUNK

    @pl.when(pl.program_id(1) == 0)
    def _():
        st_ref[...] = jnp.zeros(st_ref.shape, F32)

    m1, m2, _ = _split3(misc_ref[...])
    w1, w2, _ = _split3(wa_ref[...])
    z = (jnp.dot(m1, w1, preferred_element_type=F32) + jnp.dot(m1, w2, preferred_element_type=F32)
         + jnp.dot(m2, w1, preferred_element_type=F32)) + ba_ref[...]
    la_ref[...] = _log_sigmoid(z) * (LOG2E / GLA_TAU)

    r_i = lax.broadcasted_iota(jnp.int32, (C, C), 0)
    c_i = lax.broadcasted_iota(jnp.int32, (C, C), 1)
    tri = jnp.where(c_i <= r_i, 1.0, 0.0).astype(BF16)
    sub_row = lax.broadcasted_iota(jnp.int32, (GLA_SUB, GLA_DK), 0)
    col_sub = lax.broadcasted_iota(jnp.int32, (GLA_SUB, C), 1)
    scale = GLA_DK ** -0.5
    g_o = go_ref[...]

    chunk_rows = [slice(ci * C, (ci + 1) * C) for ci in range(nchunk)]

    for rows in chunk_rows:
        l1, l2, l3 = _split3(la_ref[rows, :])
        b_ref[rows, :] = (jnp.dot(tri, l1, preferred_element_type=F32) + jnp.dot(tri, l2, preferred_element_type=F32)
                          + jnp.dot(tri, l3, preferred_element_type=F32))

    for ci, rows in enumerate(chunk_rows):
        b = b_ref[rows, :]
        qc = q_ref[rows, :].astype(F32) * scale
        kc = k_ref[rows, :].astype(F32)
        b_last = b[C - 1:C, :]
        qe_ref[rows, :] = (qc * jnp.exp2(b)).astype(BF16)
        ke = (kc * jnp.exp2(b_last - b)).astype(BF16)
        u_ref[ci] = lax.dot_general(v_ref[rows, :], ke, (((0,), (0,)), ((), ())), preferred_element_type=F32)
        dl_ref[ci] = jnp.exp2(b_last)

        a_rows = []
        for si in range(nsub):
            lo = si * GLA_SUB
            qs, ks, bs = qc[lo:lo + GLA_SUB], kc[lo:lo + GLA_SUB], b[lo:lo + GLA_SUB]
            a = jnp.zeros((GLA_SUB, C), F32)
            for j in range(GLA_SUB):
                e = jnp.exp2(jnp.where(sub_row >= j, bs - bs[j:j + 1, :], NEG_INF))
                w = jnp.sum(qs * (ks[j:j + 1, :] * e), axis=-1, keepdims=True)
                a = jnp.where(col_sub == lo + j, w, a)
            if si > 0:
                ref_row = b[lo - 1:lo, :]
                q_i = qs * jnp.exp2(bs - ref_row)
                k_j = jnp.concatenate([kc[0:lo] * jnp.exp2(ref_row - b[0:lo]),
                                       jnp.zeros((C - lo, GLA_DK), F32)], axis=0)
                a = a + lax.dot_general(q_i.astype(BF16), k_j.astype(BF16), (((1,), (1,)), ((), ())),
                                        preferred_element_type=F32)
            a_rows.append(a)
        a_ref[rows, :] = jnp.concatenate(a_rows, axis=0).astype(BF16)

    st = st_ref[...]
    for ci in range(nchunk):
        stb_ref[ci] = st.astype(BF16)
        st = st * dl_ref[ci] + u_ref[ci]
    st_ref[...] = st

    for ci, rows in enumerate(chunk_rows):
        o = lax.dot_general(qe_ref[rows, :], stb_ref[ci], (((1,), (1,)), ((), ())), preferred_element_type=F32)
        o = o + jnp.dot(a_ref[rows, :], v_ref[rows, :], preferred_element_type=F32)
        on = o * lax.rsqrt(jnp.mean(o * o, axis=-1, keepdims=True) + EPS) * g_o
        r = r_ref[rows, :].astype(F32)
        o_ref[rows, :] = (on * (r * jax.nn.sigmoid(r))).astype(o_ref.dtype)


def _gla(pb, p_small, wa_p, b_a, g_o, tb=GLA_BLOCK):
    s = pb.shape[0]
    nchunk = tb // GLA_CHUNK
    return pl.pallas_call(
        _gla_kernel,
        grid=(GLA_HEADS, s // tb),
        in_specs=[pl.BlockSpec((tb, GLA_DK), lambda h, i: (i, B_GQ // GLA_DK + h)),
                  pl.BlockSpec((tb, GLA_DK), lambda h, i: (i, B_GK // GLA_DK + h)),
                  pl.BlockSpec((tb, GLA_DV), lambda h, i: (i, B_GV // GLA_DV + h)),
                  pl.BlockSpec((tb, GLA_DV), lambda h, i: (i, B_GR // GLA_DV + h)),
                  pl.BlockSpec((tb, LANES), lambda h, i: (i, A_MISC // LANES)),
                  pl.BlockSpec((LANES, GLA_DK), lambda h, i: (0, h)),
                  pl.BlockSpec((1, GLA_DK), lambda h, i: (0, h)),
                  pl.BlockSpec((1, GLA_DV), lambda h, i: (0, 0))],
        out_specs=pl.BlockSpec((tb, GLA_DV), lambda h, i: (i, h)),
        out_shape=jax.ShapeDtypeStruct((s, GLA_HEADS * GLA_DV), BF16),
        scratch_shapes=[pltpu.VMEM((GLA_DV, GLA_DK), F32),
                        pltpu.VMEM((tb, GLA_DK), F32),
                        pltpu.VMEM((tb, GLA_DK), F32),
                        pltpu.VMEM((tb, GLA_DK), BF16),
                        pltpu.VMEM((tb, GLA_CHUNK), BF16),
                        pltpu.VMEM((nchunk, GLA_DV, GLA_DK), F32),
                        pltpu.VMEM((nchunk, 1, GLA_DK), F32),
                        pltpu.VMEM((nchunk, GLA_DV, GLA_DK), BF16)],
        compiler_params=_cparams(("parallel", "arbitrary")),
        name="gla",
    )(pb, pb, pb, pb, p_small, wa_p, b_a, g_o)


def _in_proj_kernel(a_ref, wt_ref, o_ref, w_sc, *, act):
    @pl.when(pl.program_id(1) == 0)
    def _():
        w_sc[...] = wt_ref[0].astype(BF16)

    y = lax.dot_general(a_ref[...], w_sc[...], (((1,), (1,)), ((), ())), preferred_element_type=F32)
    if act == "sigmoid":
        y = jax.nn.sigmoid(y)
    o_ref[...] = y.astype(o_ref.dtype)


def _in_proj(a, w_all_t, layer, row_start, n, out_dtype, tm, tn, act=None, name="in_proj"):
    m, k = a.shape
    return pl.pallas_call(
        functools.partial(_in_proj_kernel, act=act),
        grid=(n // tn, m // tm),
        in_specs=[pl.BlockSpec((tm, k), lambda j, i: (i, 0)),
                  pl.BlockSpec((pl.Element(1), pl.Element(tn), pl.Element(k)),
                               lambda j, i: (layer, _aligned(row_start(j), F32_SUBLANES), 0))],
        out_specs=pl.BlockSpec((tm, tn), lambda j, i: (i, j)),
        out_shape=jax.ShapeDtypeStruct((m, n), out_dtype),
        scratch_shapes=[pltpu.VMEM((tn, k), BF16)],
        compiler_params=_cparams(("parallel", "arbitrary")),
        name=name,
    )(a, w_all_t)


def _small_w_kernel(kr_ref, fl_ref, ga_ref, o_ref):
    d = o_ref.shape[1]
    z = lambda n: jnp.zeros((n, d), F32)
    o_ref[...] = jnp.concatenate([kr_ref[0], z(LANES - MLA_ROPE), fl_ref[0], ga_ref[0],
                                  z(LANES - FOX_HEADS - GLA_GATE_RANK)], axis=0).astype(BF16)


def _small_in_proj_weights(w_all_t, layer):
    d = w_all_t.shape[2]

    def rows(i):
        return pl.BlockSpec((pl.Element(1), pl.Element(IN_SPLIT[i]), pl.Element(d)),
                            lambda g: (layer, IN_OFF[i], 0))

    return pl.pallas_call(
        _small_w_kernel,
        grid=(1,),
        in_specs=[rows(2), rows(11), rows(6)],
        out_specs=pl.BlockSpec((A_SMALL, d), lambda g: (0, 0)),
        out_shape=jax.ShapeDtypeStruct((A_SMALL, d), BF16),
        compiler_params=_cparams(("arbitrary",)),
        name="small_in_proj_weights",
    )(w_all_t, w_all_t, w_all_t)


def _pad_lanes(v, n):
    return jnp.pad(v, (0, n - v.shape[0])).reshape(1, n)


def _layer(x, h, tabs, g_next, w_in_all_t, layer, g_cq, w_uq, g_ckv, w_ukv, g_mla_q, g_mla_k, w_a2, b_a, g_gla_o,
           g_fox_q, g_fox_k, b_f, w_branch_all, w_out_all, g_ffn, w_gu_all, w_down_all):
    tn = 1024
    off = IN_OFF
    p_latent = _in_proj(h, w_in_all_t, layer, lambda j: off[0], A_LATENT, F32, 1024, tn, name="in_proj_latent")
    p_small = _matmul_nt(h, _small_in_proj_weights(w_in_all_t, layer), F32, 1024, A_SMALL, name="in_proj_small")
    n_first = (off[6] - off[3]) // tn
    pb = _in_proj(h, w_in_all_t, layer,
                  lambda j: jnp.where(j < n_first, off[3] + j * tn, off[7] + (j - n_first) * tn),
                  B_WIDTH, BF16, 1024, tn, name="in_proj_b")
    gsig = _in_proj(h, w_in_all_t, layer, lambda j: off[12] + j * tn, N_BRANCH * D_MODEL, BF16, 1024, tn,
                    act="sigmoid", name="in_proj_gates")

    wuq_p = jnp.pad(w_uq.reshape(MLA_RANK, MLA_HEADS, MLA_QK),
                    ((0, 0), (0, 0), (0, HEAD_PAD - MLA_QK))).reshape(MLA_RANK, MLA_HEADS * HEAD_PAD).astype(BF16)
    w_ukv3 = w_ukv.reshape(MLA_RANK, MLA_HEADS, MLA_NOPE + MLA_V)
    wuk = w_ukv3[:, :, :MLA_NOPE].reshape(MLA_RANK, MLA_HEADS * MLA_NOPE).astype(BF16)
    wuv_t = w_ukv3[:, :, MLA_NOPE:].reshape(MLA_RANK, MLA_HEADS * MLA_V).T.astype(BF16)
    q_a, k_a, vt_a = _mla_prep(p_latent, p_small, wuq_p, wuk, wuv_t, g_cq.reshape(1, -1), g_ckv.reshape(1, -1),
                               _pad_lanes(g_mla_q, HEAD_PAD), g_mla_k[:MLA_NOPE].reshape(1, -1),
                               _pad_lanes(g_mla_k[MLA_NOPE:], LANES), tabs)
    o_a = _attention(q_a, k_a, vt_a)

    wa_p = jnp.zeros((LANES, GLA_HEADS * GLA_DK), F32).at[MISC_GA:MISC_GA + GLA_GATE_RANK].set(w_a2)
    o_b = _gla(pb, p_small, wa_p, b_a.reshape(1, -1), g_gla_o.reshape(1, -1))

    f_cum, f_cum_t = _fox_cumsum(p_small,_pad_lanes(b_f, LANES))
    q_c, k_c = _fox_prep(pb, f_cum, g_fox_q.reshape(1, -1), g_fox_k.reshape(1, -1))
    vt_c = _value_t_proj(h, w_in_all_t, layer, off[10], FOX_HEADS)
    o_c = _attention(q_c, k_c, vt_c, f_cum_t)

    merged = _merge(o_a, o_b, o_c, w_branch_all, layer, gsig, 1024, 512)
    x1, h2 = _matmul_residual(merged, w_out_all, layer, x, 512, "out_proj", g_next=g_ffn)
    act = _swiglu_up(h2, w_gu_all, layer, 1024, 512)
    if g_next is None:
        return _matmul_residual(act, w_down_all, layer, x1, 256, "ffn_down"), None
    return _matmul_residual(act, w_down_all, layer, x1, 256, "ffn_down", g_next=g_next)


def kernel(x, positions, g_mix, w_in, g_cq, w_uq, g_ckv, w_ukv, g_mla_q, g_mla_k, w_a2, b_a, g_gla_o,
           g_fox_q, g_fox_k, b_f, w_branch, w_out, g_ffn, w_gu, w_down):
    bsz, s, d = x.shape
    assert bsz == 1 and d == D_MODEL and s % ATT_TILE == 0
    half = MLA_ROPE // 2
    inv = ROPE_THETA ** (-jnp.arange(half, dtype=F32) / half)
    inv128 = jnp.concatenate([inv, inv, jnp.zeros((LANES - MLA_ROPE,), F32)]).reshape(1, LANES)
    tabs = _rope_tables(positions.reshape(s, 1), inv128)
    depth = w_in.shape[0]
    w_in = jnp.swapaxes(w_in, 1, 2)
    w_branch = w_branch.reshape(depth, N_BRANCH * BRANCH_WIDTH, D_MODEL).astype(BF16)
    w_out = w_out.astype(BF16)
    w_down = w_down.astype(BF16)
    xcur = x.reshape(s, d)
    hcur = _rmsnorm(xcur, g_mix[0])
    for l in range(depth):
        g_next = g_mix[l + 1] if l + 1 < depth else None
        xcur, hcur = _layer(xcur, hcur, tabs, g_next, w_in, l, g_cq[l], w_uq[l], g_ckv[l], w_ukv[l], g_mla_q[l],
                            g_mla_k[l], w_a2[l], b_a[l], g_gla_o[l], g_fox_q[l], g_fox_k[l], b_f[l], w_branch,
                            w_out, g_ffn[l], w_gu, w_down)
    return xcur.reshape(bsz, s, d)
```

```python
import functools

import jax
import jax.numpy as jnp
from jax import lax
from jax.experimental import pallas as pl
from jax.experimental.pallas import tpu as pltpu

F32 = jnp.float32
BF16 = jnp.bfloat16

D_MODEL = 2048
BRANCH_WIDTH = D_MODEL // 2
N_BRANCH = 3
MLA_NOPE = 128
MLA_ROPE = 64
MLA_V = 128
MLA_HEADS = BRANCH_WIDTH // MLA_V
MLA_QK = MLA_NOPE + MLA_ROPE
MLA_RANK = 512
ROPE_THETA = 10000.0
GLA_HEADS = 4
GLA_DV = BRANCH_WIDTH // GLA_HEADS
GLA_DK = GLA_DV // 2
GLA_GATE_RANK = 16
GLA_TAU = 16.0
FOX_DH = 128
FOX_HEADS = BRANCH_WIDTH // FOX_DH
FFN_HIDDEN = ((8 * D_MODEL + 3 * 256 - 1) // (3 * 256)) * 256
EPS = 1e-6
NEG_INF = -1e30

LANES = 128
F32_SUBLANES = 8
BF16_SUBLANES = 16
HEAD_PAD = 256
VMEM_LIMIT = 48 * 1024 * 1024
VMEM_LIMIT_BIG = 56 * 1024 * 1024

IN_SPLIT = (MLA_RANK, MLA_RANK, MLA_ROPE, GLA_HEADS * GLA_DK, GLA_HEADS * GLA_DK, GLA_HEADS * GLA_DV, GLA_GATE_RANK,
            GLA_HEADS * GLA_DV, FOX_HEADS * FOX_DH, FOX_HEADS * FOX_DH, FOX_HEADS * FOX_DH, FOX_HEADS,
            N_BRANCH * D_MODEL)
IN_OFF = tuple(sum(IN_SPLIT[:i]) for i in range(len(IN_SPLIT) + 1))

A_CQ, A_CKV = 0, 512
A_LATENT = 1024
A_KR, A_MISC = 0, 128
A_SMALL = 256
MISC_FL = 0
MISC_GA = FOX_HEADS
B_GQ, B_GK, B_GV, B_GR, B_FQ, B_FK = 0, 512, 1024, 2048, 3072, 4096
B_WIDTH = 5120

ATT_TILE = 1024
ATT_SUB = 256
ATT_VROWS = LANES + BF16_SUBLANES
LOG2E = 1.4426950408889634

GLA_CHUNK = 64
GLA_SUB = 8
GLA_BLOCK = 512


def _aligned(start, m):
    return start if isinstance(start, int) else pl.multiple_of(start, m)


def _cparams(sem):
    return pltpu.CompilerParams(dimension_semantics=sem, vmem_limit_bytes=VMEM_LIMIT)


def _split3(x):
    x1 = x.astype(BF16)
    r1 = x - x1.astype(F32)
    x2 = r1.astype(BF16)
    x3 = (r1 - x2.astype(F32)).astype(BF16)
    return x1, x2, x3


def _log_sigmoid(z):
    return -(jnp.maximum(-z, 0.0) + jnp.log1p(jnp.exp(-jnp.abs(z))))


def _store_values_t(vt_out, vt):
    rows = vt.shape[1]
    r = lax.broadcasted_iota(jnp.int32, (BF16_SUBLANES, rows), 0)
    ones_rows = jnp.where(r == 0, 1.0, 0.0).astype(BF16)
    for h in range(vt.shape[0] // LANES):
        vt_out[h, 0, 0:LANES, :] = vt[h * LANES:(h + 1) * LANES, :].astype(BF16)
        vt_out[h, 0, LANES:ATT_VROWS, :] = ones_rows


def _rmsnorm_kernel(x_ref, g_ref, o_ref):
    x = x_ref[...]
    ms = jnp.mean(x * x, axis=-1, keepdims=True)
    o_ref[...] = (x * lax.rsqrt(ms + EPS) * g_ref[...]).astype(o_ref.dtype)


def _rmsnorm(x, g, tm=512):
    m, d = x.shape
    return pl.pallas_call(
        _rmsnorm_kernel,
        grid=(m // tm,),
        in_specs=[pl.BlockSpec((tm, d), lambda i: (i, 0)),
                  pl.BlockSpec((1, d), lambda i: (0, 0))],
        out_specs=pl.BlockSpec((tm, d), lambda i: (i, 0)),
        out_shape=jax.ShapeDtypeStruct((m, d), BF16),
        compiler_params=_cparams(("parallel",)),
        name="rmsnorm",
    )(x, g.reshape(1, d))


def _mm_nt_kernel(a_ref, wt_ref, o_ref, *, act):
    y = lax.dot_general(a_ref[...], wt_ref[...], (((1,), (1,)), ((), ())), preferred_element_type=F32)
    if act == "sigmoid":
        y = jax.nn.sigmoid(y)
    o_ref[...] = y.astype(o_ref.dtype)


def _matmul_nt(a, w_t, out_dtype, tm, tn, act=None, name="matmul"):
    m, k = a.shape
    n = w_t.shape[0]
    return pl.pallas_call(
        functools.partial(_mm_nt_kernel, act=act),
        grid=(m // tm, n // tn),
        in_specs=[pl.BlockSpec((tm, k), lambda i, j: (i, 0)),
                  pl.BlockSpec((tn, k), lambda i, j: (j, 0))],
        out_specs=pl.BlockSpec((tm, tn), lambda i, j: (i, j)),
        out_shape=jax.ShapeDtypeStruct((m, n), out_dtype),
        compiler_params=_cparams(("parallel", "arbitrary")),
        name=name,
    )(a, w_t)


def _mm_res_kernel(*refs, with_norm):
    if with_norm:
        a_ref, w_ref, r_ref, g_ref, o_ref, h_ref = refs
    else:
        a_ref, w_ref, r_ref, o_ref = refs
    y = r_ref[...] + jnp.dot(a_ref[...], w_ref[...], preferred_element_type=F32)
    o_ref[...] = y
    if with_norm:
        ms = jnp.mean(y * y, axis=-1, keepdims=True)
        h_ref[...] = (y * lax.rsqrt(ms + EPS) * g_ref[...]).astype(h_ref.dtype)


def _matmul_residual(a, w_all, layer, res, tm, name, g_next=None):
    m, k = a.shape
    n = w_all.shape[2]
    with_norm = g_next is not None
    row = pl.BlockSpec((tm, n), lambda i: (i, 0))
    in_specs = [pl.BlockSpec((tm, k), lambda i: (i, 0)),
                pl.BlockSpec((None, k, n), lambda i: (layer, 0, 0), pipeline_mode=pl.Buffered(1)),
                row]
    args = [a, w_all, res]
    out_specs, out_shape = row, jax.ShapeDtypeStruct((m, n), F32)
    if with_norm:
        in_specs.append(pl.BlockSpec((1, n), lambda i: (0, 0)))
        args.append(g_next.reshape(1, n))
        out_specs, out_shape = [row, row], [out_shape, jax.ShapeDtypeStruct((m, n), BF16)]
    return pl.pallas_call(
        functools.partial(_mm_res_kernel, with_norm=with_norm),
        grid=(m // tm,),
        in_specs=in_specs,
        out_specs=out_specs,
        out_shape=out_shape,
        compiler_params=pltpu.CompilerParams(dimension_semantics=("parallel",), vmem_limit_bytes=VMEM_LIMIT_BIG),
        name=name,
    )(*args)


def _swiglu_kernel(a_ref, wg_ref, wu_ref, o_ref, wg_sc, wu_sc):
    @pl.when(pl.program_id(1) == 0)
    def _():
        wg_sc[...] = wg_ref[...].astype(BF16)
        wu_sc[...] = wu_ref[...].astype(BF16)

    a = a_ref[...]
    g = jnp.dot(a, wg_sc[...], preferred_element_type=F32)
    u = jnp.dot(a, wu_sc[...], preferred_element_type=F32)
    o_ref[...] = (g * jax.nn.sigmoid(g) * u).astype(o_ref.dtype)


def _swiglu_up(a, w_gu_all, layer, tm, tn):
    m, k = a.shape
    hidden = w_gu_all.shape[2] // 2
    nj = hidden // tn
    return pl.pallas_call(
        _swiglu_kernel,
        grid=(nj, m // tm),
        in_specs=[pl.BlockSpec((tm, k), lambda j, i: (i, 0)),
                  pl.BlockSpec((None, k, tn), lambda j, i: (layer, 0, j)),
                  pl.BlockSpec((None, k, tn), lambda j, i: (layer, 0, j + nj))],
        out_specs=pl.BlockSpec((tm, tn), lambda j, i: (i, j)),
        out_shape=jax.ShapeDtypeStruct((m, hidden), BF16),
        scratch_shapes=[pltpu.VMEM((k, tn), BF16), pltpu.VMEM((k, tn), BF16)],
        compiler_params=_cparams(("parallel", "arbitrary")),
        name="ffn_up",
    )(a, w_gu_all, w_gu_all)


def _merge_out_kernel(oa_ref, ob_ref, oc_ref, gs_ref, wb_ref, wo_ref, x_ref, g_ref, x1_ref, h_ref):
    bw = oa_ref.shape[1]
    d = x_ref.shape[1]
    merged = None
    for n, o_ref in enumerate((oa_ref, ob_ref, oc_ref)):
        y = jnp.dot(o_ref[...], wb_ref[n * bw:(n + 1) * bw, :], preferred_element_type=F32)
        y = y * gs_ref[:, n * d:(n + 1) * d].astype(F32)
        merged = y if merged is None else merged + y
    x1 = x_ref[...] + jnp.dot(merged.astype(BF16), wo_ref[...], preferred_element_type=F32)
    x1_ref[...] = x1
    ms = jnp.mean(x1 * x1, axis=-1, keepdims=True)
    h_ref[...] = (x1 * lax.rsqrt(ms + EPS) * g_ref[...]).astype(h_ref.dtype)


def _merge_out(o_a, o_b, o_c, gsig, w_branch_all, w_out_all, layer, x, g_next, tm=256):
    m, bw = o_a.shape
    d = x.shape[1]
    o_spec = pl.BlockSpec((tm, bw), lambda i: (i, 0))
    row = pl.BlockSpec((tm, d), lambda i: (i, 0))
    return pl.pallas_call(
        _merge_out_kernel,
        grid=(m // tm,),
        in_specs=[o_spec, o_spec, o_spec,
                  pl.BlockSpec((tm, N_BRANCH * d), lambda i: (i, 0)),
                  pl.BlockSpec((None, N_BRANCH * bw, d), lambda i: (layer, 0, 0), pipeline_mode=pl.Buffered(1)),
                  pl.BlockSpec((None, d, d), lambda i: (layer, 0, 0), pipeline_mode=pl.Buffered(1)),
                  row,
                  pl.BlockSpec((1, d), lambda i: (0, 0))],
        out_specs=[row, row],
        out_shape=[jax.ShapeDtypeStruct((m, d), F32), jax.ShapeDtypeStruct((m, d), BF16)],
        compiler_params=pltpu.CompilerParams(dimension_semantics=("parallel",), vmem_limit_bytes=VMEM_LIMIT_BIG),
        name="merge_out_proj",
    )(o_a, o_b, o_c, gsig, w_branch_all, w_out_all, x, g_next.reshape(1, d))


def _rope_table_kernel(pos_ref, inv_ref, c_ref, s1_ref, s2_ref):
    ang = pos_ref[...].astype(F32) * inv_ref[...]
    c = jnp.cos(ang)
    s = jnp.sin(ang)
    lane = lax.broadcasted_iota(jnp.int32, ang.shape, 1)
    half = MLA_ROPE // 2
    c_ref[...] = jnp.where(lane < MLA_ROPE, c, 0.0)
    s1_ref[...] = jnp.where((lane >= half) & (lane < MLA_ROPE), s, 0.0)
    s2_ref[...] = jnp.where(lane < half, -s, 0.0)


def _rope_tables(pos_col, inv128, tm=512):
    s = pos_col.shape[0]
    spec = pl.BlockSpec((tm, LANES), lambda i: (i, 0))
    shp = jax.ShapeDtypeStruct((s, LANES), F32)
    return pl.pallas_call(
        _rope_table_kernel,
        grid=(s // tm,),
        in_specs=[pl.BlockSpec((tm, 1), lambda i: (i, 0)),
                  pl.BlockSpec((1, LANES), lambda i: (0, 0))],
        out_specs=[spec, spec, spec],
        out_shape=[shp, shp, shp],
        compiler_params=_cparams(("parallel",)),
        name="rope_tables",
    )(pos_col, inv128)


def _rope128(x, c, s1, s2):
    half = MLA_ROPE // 2
    return x * c + pltpu.roll(x, half, 1) * s1 + pltpu.roll(x, LANES - half, 1) * s2


def _mla_prep_kernel(cq_ref, ckv_ref, kr_ref, wuq_ref, wuk_ref, wuvt_ref, gcq_ref, gckv_ref, gq_ref, gk_ref, gkr_ref,
                     c_ref, s1_ref, s2_ref, q_out, k_out, vt_out):
    def norm(x, g):
        return x * lax.rsqrt(jnp.mean(x * x, axis=-1, keepdims=True) + EPS) * g

    cqn = norm(cq_ref[...], gcq_ref[...]).astype(BF16)
    ckvn = norm(ckv_ref[...], gckv_ref[...]).astype(BF16)
    qa = jnp.dot(cqn, wuq_ref[...], preferred_element_type=F32)
    ka = jnp.dot(ckvn, wuk_ref[...], preferred_element_type=F32)
    _store_values_t(vt_out, lax.dot_general(wuvt_ref[...], ckvn, (((1,), (1,)), ((), ())),
                                            preferred_element_type=F32))
    krp = kr_ref[...]
    kr_ss = jnp.sum(krp * krp, axis=-1, keepdims=True)
    c, s1, s2 = c_ref[...], s1_ref[...], s2_ref[...]
    gq, gk, gkr = gq_ref[...], gk_ref[...], gkr_ref[...]
    scale = MLA_QK ** -0.5 * LOG2E
    kr_roped = _rope128(krp * gkr, c, s1, s2)
    for h in range(MLA_HEADS):
        qh = qa[:, h * HEAD_PAD:(h + 1) * HEAD_PAD]
        rq = lax.rsqrt(jnp.sum(qh * qh, axis=-1, keepdims=True) * (1.0 / MLA_QK) + EPS)
        qn = qh * rq * gq
        q_out[h, :, 0:LANES] = (qn[:, 0:LANES] * scale).astype(BF16)
        q_out[h, :, LANES:HEAD_PAD] = (_rope128(qn[:, LANES:HEAD_PAD], c, s1, s2) * scale).astype(BF16)
        kn = ka[:, h * MLA_NOPE:(h + 1) * MLA_NOPE]
        rk = lax.rsqrt((jnp.sum(kn * kn, axis=-1, keepdims=True) + kr_ss) * (1.0 / MLA_QK) + EPS)
        k_out[h, :, 0:LANES] = (kn * rk * gk).astype(BF16)
        k_out[h, :, LANES:HEAD_PAD] = (kr_roped * rk).astype(BF16)


def _vt_spec(heads, tm):
    per_tile = ATT_TILE // tm
    return pl.BlockSpec((heads, 1, ATT_VROWS, tm), lambda i: (0, i // per_tile, 0, i % per_tile))


def _mla_prep(p_latent, p_small, wuq_p, wuk, wuv_t, g_cq, g_ckv, g_q, g_k, g_kr, tabs, tm=256):
    s = p_latent.shape[0]
    h = MLA_HEADS
    row = lambda w: pl.BlockSpec((1, w), lambda i: (0, 0))
    tab = pl.BlockSpec((tm, LANES), lambda i: (i, 0))
    return pl.pallas_call(
        _mla_prep_kernel,
        grid=(s // tm,),
        in_specs=[pl.BlockSpec((tm, MLA_RANK), lambda i: (i, A_CQ // MLA_RANK)),
                  pl.BlockSpec((tm, MLA_RANK), lambda i: (i, A_CKV // MLA_RANK)),
                  pl.BlockSpec((tm, LANES), lambda i: (i, A_KR // LANES)),
                  pl.BlockSpec(wuq_p.shape, lambda i: (0, 0)),
                  pl.BlockSpec(wuk.shape, lambda i: (0, 0)),
                  pl.BlockSpec(wuv_t.shape, lambda i: (0, 0)),
                  row(MLA_RANK), row(MLA_RANK), row(HEAD_PAD), row(LANES), row(LANES),
                  tab, tab, tab],
        out_specs=[pl.BlockSpec((h, tm, HEAD_PAD), lambda i: (0, i, 0)),
                   pl.BlockSpec((h, tm, HEAD_PAD), lambda i: (0, i, 0)),
                   _vt_spec(h, tm)],
        out_shape=[jax.ShapeDtypeStruct((h, s, HEAD_PAD), BF16),
                   jax.ShapeDtypeStruct((h, s, HEAD_PAD), BF16),
                   jax.ShapeDtypeStruct((h, s // ATT_TILE, ATT_VROWS, ATT_TILE), BF16)],
        compiler_params=_cparams(("parallel",)),
        name="mla_prep",
    )(p_latent, p_latent, p_small, wuq_p, wuk, wuv_t, g_cq, g_ckv, g_q, g_k, g_kr, *tabs)


def _fox_cumsum_kernel(misc_ref, bf_ref, f_ref, ft_ref, carry_ref):
    blk = misc_ref.shape[0]

    @pl.when(pl.program_id(0) == 0)
    def _():
        carry_ref[...] = jnp.zeros(carry_ref.shape, F32)

    r = lax.broadcasted_iota(jnp.int32, (blk, blk), 0)
    c = lax.broadcasted_iota(jnp.int32, (blk, blk), 1)
    tri = jnp.where(c <= r, 1.0, 0.0).astype(BF16)
    x1, x2, x3 = _split3(_log_sigmoid(misc_ref[...] + bf_ref[...]))
    cs = (jnp.dot(tri, x1, preferred_element_type=F32)
          + jnp.dot(tri, x2, preferred_element_type=F32)
          + jnp.dot(tri, x3, preferred_element_type=F32)) + carry_ref[...]
    f_ref[...] = cs
    ft_ref[...] = cs.T
    carry_ref[...] = cs[blk - 1:blk, :]


def _fox_cumsum(p_small, bf128, blk=512):
    s = p_small.shape[0]
    return pl.pallas_call(
        _fox_cumsum_kernel,
        grid=(s // blk,),
        in_specs=[pl.BlockSpec((blk, LANES), lambda i: (i, A_MISC // LANES)),
                  pl.BlockSpec((1, LANES), lambda i: (0, 0))],
        out_specs=[pl.BlockSpec((blk, LANES), lambda i: (i, 0)),
                   pl.BlockSpec((LANES, blk), lambda i: (0, i))],
        out_shape=[jax.ShapeDtypeStruct((s, LANES), F32), jax.ShapeDtypeStruct((LANES, s), F32)],
        scratch_shapes=[pltpu.VMEM((1, LANES), F32)],
        compiler_params=_cparams(("arbitrary",)),
        name="fox_cumsum",
    )(p_small, bf128)


def _fox_prep_kernel(fq_ref, fk_ref, f_ref, gq_ref, gk_ref, sel_ref, q_out, k_out):
    tm = fq_ref.shape[0]
    lane = lax.broadcasted_iota(jnp.int32, (tm, LANES), 1)
    ones_tail = jnp.where(lane < 3, 1.0, 0.0).astype(BF16)
    pieces = jnp.concatenate(_split3(f_ref[...] * (-LOG2E)), axis=1)
    scale = FOX_DH ** -0.5 * LOG2E
    gq, gk = gq_ref[...], gk_ref[...]
    for h in range(FOX_HEADS):
        sl = slice(h * FOX_DH, (h + 1) * FOX_DH)
        q = fq_ref[:, sl].astype(F32)
        k = fk_ref[:, sl].astype(F32)
        qn = q * lax.rsqrt(jnp.mean(q * q, axis=-1, keepdims=True) + EPS) * gq
        kn = k * lax.rsqrt(jnp.mean(k * k, axis=-1, keepdims=True) + EPS) * gk
        q_out[h, :, 0:LANES] = (qn * scale).astype(BF16)
        q_out[h, :, LANES:HEAD_PAD] = ones_tail
        k_out[h, :, 0:LANES] = kn.astype(BF16)
        k_out[h, :, LANES:HEAD_PAD] = jnp.dot(pieces, sel_ref[h], preferred_element_type=F32).astype(BF16)


def _fox_prep(pb, f_cum, g_q, g_k, tm=256):
    s = pb.shape[0]
    h = FOX_HEADS
    w = FOX_HEADS * FOX_DH
    row = pl.BlockSpec((1, FOX_DH), lambda i: (0, 0))
    hh, pp = jnp.meshgrid(jnp.arange(h), jnp.arange(3), indexing="ij")
    sel = jnp.zeros((h, 3 * LANES, LANES), BF16).at[hh, pp * LANES + hh, pp].set(1.0)
    return pl.pallas_call(
        _fox_prep_kernel,
        grid=(s // tm,),
        in_specs=[pl.BlockSpec((tm, w), lambda i: (i, B_FQ // w)),
                  pl.BlockSpec((tm, w), lambda i: (i, B_FK // w)),
                  pl.BlockSpec((tm, LANES), lambda i: (i, 0)),
                  row, row,
                  pl.BlockSpec(sel.shape, lambda i: (0, 0, 0))],
        out_specs=[pl.BlockSpec((h, tm, HEAD_PAD), lambda i: (0, i, 0)),
                   pl.BlockSpec((h, tm, HEAD_PAD), lambda i: (0, i, 0))],
        out_shape=[jax.ShapeDtypeStruct((h, s, HEAD_PAD), BF16),
                   jax.ShapeDtypeStruct((h, s, HEAD_PAD), BF16)],
        compiler_params=_cparams(("parallel",)),
        name="fox_prep",
    )(pb, pb, f_cum, g_q, g_k, sel)


def _value_t_kernel(h_ref, w_ref, vt_out, w_sc):
    @pl.when(pl.program_id(0) == 0)
    def _():
        w_sc[...] = w_ref[0].astype(BF16)

    vt = lax.dot_general(w_sc[...], h_ref[...], (((1,), (1,)), ((), ())), preferred_element_type=F32)
    _store_values_t(vt_out, vt)


def _value_t_proj(h, w_all_t, layer, row_start, heads, tm=512):
    s, d = h.shape
    nv = heads * LANES
    return pl.pallas_call(
        _value_t_kernel,
        grid=(s // tm,),
        in_specs=[pl.BlockSpec((tm, d), lambda i: (i, 0)),
                  pl.BlockSpec((pl.Element(1), pl.Element(nv), pl.Element(d)), lambda i: (layer, row_start, 0))],
        out_specs=_vt_spec(heads, tm),
        out_shape=jax.ShapeDtypeStruct((heads, s // ATT_TILE, ATT_VROWS, ATT_TILE), BF16),
        scratch_shapes=[pltpu.VMEM((nv, d), BF16)],
        compiler_params=_cparams(("arbitrary",)),
        name="fox_value_t",
    )(h, w_all_t)


def _attn_kernel(*refs, tq, use_f):
    if use_f:
        q_ref, k_ref, vt_ref, f_ref, o_ref, m_sc, acc_sc, s_a, s_b = refs
    else:
        q_ref, k_ref, vt_ref, o_ref, m_sc, acc_sc, s_a, s_b = refs
    ns = tq // ATT_SUB
    nq = q_ref.shape[1] // tq

    def reset():
        m_sc[...] = jnp.full(m_sc.shape, NEG_INF, F32)
        acc_sc[...] = jnp.zeros(acc_sc.shape, F32)

    def logits(dst, qi, kb):
        dst[...] = lax.dot_general(k_ref[0, pl.ds(_aligned(kb * tq, tq), tq), :],
                                   q_ref[0, pl.ds(_aligned(qi * tq, tq), tq), :],
                                   (((1,), (1,)), ((), ())), preferred_element_type=F32)

    def update(src, j, qi, kb, k0, size, causal):
        s = src[k0:k0 + size, j * ATT_SUB:(j + 1) * ATT_SUB]
        if causal:
            key = lax.broadcasted_iota(jnp.int32, (size, ATT_SUB), 0)
            qry = lax.broadcasted_iota(jnp.int32, (size, ATT_SUB), 1)
            s = jnp.where(key <= qry, s, NEG_INF)
        m_old = m_sc[j]
        m_cur = jnp.max(s, axis=0, keepdims=True)
        if use_f:
            ft = f_ref[pl.ds(qi, 1), j * ATT_SUB:(j + 1) * ATT_SUB] * LOG2E
            m_new = jnp.maximum(m_old, m_cur + ft)
            shift = m_new - ft
        else:
            m_new = jnp.maximum(m_old, m_cur)
            shift = m_new
        p = jnp.exp2(s - shift).astype(BF16)
        alpha = jnp.exp2(m_old - m_new)
        acc_sc[j] = alpha * acc_sc[j] + jnp.dot(vt_ref[0, kb, :, k0:k0 + size], p, preferred_element_type=F32)
        m_sc[j] = m_new

    def full_tile(src, qi, kb):
        for j in range(ns):
            update(src, j, qi, kb, 0, tq, False)

    def diagonal_tile(src, qi):
        for j in range(ns):
            if j > 0:
                update(src, j, qi, qi, 0, j * ATT_SUB, False)
            update(src, j, qi, qi, j * ATT_SUB, ATT_SUB, True)
        for j in range(ns):
            acc = acc_sc[j]
            o_t = acc[0:LANES, :] / acc[LANES:LANES + 1, :]
            o_ref[pl.ds(_aligned(qi * tq + j * ATT_SUB, ATT_SUB), ATT_SUB), :] = o_t.T.astype(o_ref.dtype)
        reset()

    def query_tile(qi, cur, oth, odd):
        def pair(u, carry):
            t = 2 * u
            logits(oth, qi, t + 1)
            full_tile(cur, qi, t)
            logits(cur, qi, t + 2)
            full_tile(oth, qi, t + 1)
            return carry

        lax.fori_loop(0, qi // 2, pair, 0)
        q_next = jnp.minimum(qi + 1, nq - 1)
        if odd:
            logits(oth, qi, qi)
            full_tile(cur, qi, qi - 1)
            logits(cur, q_next, 0)
            diagonal_tile(oth, qi)
        else:
            logits(oth, q_next, 0)
            diagonal_tile(cur, qi)

    assert nq % 4 == 0
    reset()
    logits(s_a, 0, 0)

    def four_query_tiles(g, carry):
        q0 = 4 * g
        query_tile(q0, s_a, s_b, False)
        query_tile(q0 + 1, s_b, s_a, True)
        query_tile(q0 + 2, s_b, s_a, False)
        query_tile(q0 + 3, s_a, s_b, True)
        return carry

    lax.fori_loop(0, nq // 4, four_query_tiles, 0)


def _attention(q, k, vt, f_t=None):
    hh, s, dk = q.shape
    _, nt, vrows, tq = vt.shape
    use_f = f_t is not None
    in_specs = [pl.BlockSpec((1, s, dk), lambda h: (h, 0, 0)),
                pl.BlockSpec((1, s, dk), lambda h: (h, 0, 0)),
                pl.BlockSpec((1, nt, vrows, tq), lambda h: (h, 0, 0, 0))]
    args = [q, k, vt]
    if use_f:
        in_specs.append(pl.BlockSpec((None, nt, tq), lambda h: (h, 0, 0)))
        args.append(f_t.reshape(f_t.shape[0], nt, tq))
    ns = tq // ATT_SUB
    return pl.pallas_call(
        functools.partial(_attn_kernel, tq=tq, use_f=use_f),
        grid=(hh,),
        in_specs=in_specs,
        out_specs=pl.BlockSpec((s, LANES), lambda h: (0, h)),
        out_shape=jax.ShapeDtypeStruct((s, hh * LANES), BF16),
        scratch_shapes=[pltpu.VMEM((ns, 1, ATT_SUB), F32), pltpu.VMEM((ns, vrows, ATT_SUB), F32),
                        pltpu.VMEM((tq, tq), F32), pltpu.VMEM((tq, tq), F32)],
        compiler_params=_cparams(("parallel",)),
        name="fox_attention" if use_f else "mla_attention",
    )(*args)


def _gla_kernel(q_ref, k_ref, v_ref, r_ref, misc_ref, wa_ref, ba_ref, go_ref, o_ref,
                st_ref, la_ref, b_ref, qe_ref, a_ref, u_ref, dl_ref, stb_ref):
    tb = q_ref.shape[0]
    nchunk = tb // GLA_CHUNK
    nsub = GLA_CHUNK // GLA_SUB
    C = GLA_CHUNK

    @pl.when(pl.program_id(1) == 0)
    def _():
        st_ref[...] = jnp.zeros(st_ref.shape, F32)

    m1, m2, _ = _split3(misc_ref[...])
    w1, w2, _ = _split3(wa_ref[...])
    z = (jnp.dot(m1, w1, preferred_element_type=F32) + jnp.dot(m1, w2, preferred_element_type=F32)
         + jnp.dot(m2, w1, preferred_element_type=F32)) + ba_ref[...]
    la_ref[...] = _log_sigmoid(z) * (LOG2E / GLA_TAU)

    r_i = lax.broadcasted_iota(jnp.int32, (C, C), 0)
    c_i = lax.broadcasted_iota(jnp.int32, (C, C), 1)
    tri = jnp.where(c_i <= r_i, 1.0, 0.0).astype(BF16)
    sub_row = lax.broadcasted_iota(jnp.int32, (GLA_SUB, GLA_DK), 0)
    col_sub = lax.broadcasted_iota(jnp.int32, (GLA_SUB, C), 1)
    scale = GLA_DK ** -0.5
    g_o = go_ref[...]

    chunk_rows = [slice(ci * C, (ci + 1) * C) for ci in range(nchunk)]

    for rows in chunk_rows:
        l1, l2, l3 = _split3(la_ref[rows, :])
        b_ref[rows, :] = (jnp.dot(tri, l1, preferred_element_type=F32) + jnp.dot(tri, l2, preferred_element_type=F32)
                          + jnp.dot(tri, l3, preferred_element_type=F32))

    for ci, rows in enumerate(chunk_rows):
        b = b_ref[rows, :]
        qc = q_ref[rows, :].astype(F32) * scale
        kc = k_ref[rows, :].astype(F32)
        b_last = b[C - 1:C, :]
        qe_ref[rows, :] = (qc * jnp.exp2(b)).astype(BF16)
        ke = (kc * jnp.exp2(b_last - b)).astype(BF16)
        u_ref[ci] = lax.dot_general(v_ref[rows, :], ke, (((0,), (0,)), ((), ())), preferred_element_type=F32)
        dl_ref[ci] = jnp.exp2(b_last)

        a_rows = []
        for si in range(nsub):
            lo = si * GLA_SUB
            qs, ks, bs = qc[lo:lo + GLA_SUB], kc[lo:lo + GLA_SUB], b[lo:lo + GLA_SUB]
            a = jnp.zeros((GLA_SUB, C), F32)
            for j in range(GLA_SUB):
                e = jnp.exp2(jnp.where(sub_row >= j, bs - bs[j:j + 1, :], NEG_INF))
                w = jnp.sum(qs * (ks[j:j + 1, :] * e), axis=-1, keepdims=True)
                a = jnp.where(col_sub == lo + j, w, a)
            if si > 0:
                ref_row = b[lo - 1:lo, :]
                q_i = qs * jnp.exp2(bs - ref_row)
                k_j = jnp.concatenate([kc[0:lo] * jnp.exp2(ref_row - b[0:lo]),
                                       jnp.zeros((C - lo, GLA_DK), F32)], axis=0)
                a = a + lax.dot_general(q_i.astype(BF16), k_j.astype(BF16), (((1,), (1,)), ((), ())),
                                        preferred_element_type=F32)
            a_rows.append(a)
        a_ref[rows, :] = jnp.concatenate(a_rows, axis=0).astype(BF16)

    st = st_ref[...]
    for ci in range(nchunk):
        stb_ref[ci] = st.astype(BF16)
        st = st * dl_ref[ci] + u_ref[ci]
    st_ref[...] = st

    for ci, rows in enumerate(chunk_rows):
        o = lax.dot_general(qe_ref[rows, :], stb_ref[ci], (((1,), (1,)), ((), ())), preferred_element_type=F32)
        o = o + jnp.dot(a_ref[rows, :], v_ref[rows, :], preferred_element_type=F32)
        on = o * lax.rsqrt(jnp.mean(o * o, axis=-1, keepdims=True) + EPS) * g_o
        r = r_ref[rows, :].astype(F32)
        o_ref[rows, :] = (on * (r * jax.nn.sigmoid(r))).astype(o_ref.dtype)


def _gla(pb, p_small, wa_p, b_a, g_o, tb=GLA_BLOCK):
    s = pb.shape[0]
    nchunk = tb // GLA_CHUNK
    return pl.pallas_call(
        _gla_kernel,
        grid=(GLA_HEADS, s // tb),
        in_specs=[pl.BlockSpec((tb, GLA_DK), lambda h, i: (i, B_GQ // GLA_DK + h)),
                  pl.BlockSpec((tb, GLA_DK), lambda h, i: (i, B_GK // GLA_DK + h)),
                  pl.BlockSpec((tb, GLA_DV), lambda h, i: (i, B_GV // GLA_DV + h)),
                  pl.BlockSpec((tb, GLA_DV), lambda h, i: (i, B_GR // GLA_DV + h)),
                  pl.BlockSpec((tb, LANES), lambda h, i: (i, A_MISC // LANES)),
                  pl.BlockSpec((LANES, GLA_DK), lambda h, i: (0, h)),
                  pl.BlockSpec((1, GLA_DK), lambda h, i: (0, h)),
                  pl.BlockSpec((1, GLA_DV), lambda h, i: (0, 0))],
        out_specs=pl.BlockSpec((tb, GLA_DV), lambda h, i: (i, h)),
        out_shape=jax.ShapeDtypeStruct((s, GLA_HEADS * GLA_DV), BF16),
        scratch_shapes=[pltpu.VMEM((GLA_DV, GLA_DK), F32),
                        pltpu.VMEM((tb, GLA_DK), F32),
                        pltpu.VMEM((tb, GLA_DK), F32),
                        pltpu.VMEM((tb, GLA_DK), BF16),
                        pltpu.VMEM((tb, GLA_CHUNK), BF16),
                        pltpu.VMEM((nchunk, GLA_DV, GLA_DK), F32),
                        pltpu.VMEM((nchunk, 1, GLA_DK), F32),
                        pltpu.VMEM((nchunk, GLA_DV, GLA_DK), BF16)],
        compiler_params=_cparams(("parallel", "arbitrary")),
        name="gla",
    )(pb, pb, pb, pb, p_small, wa_p, b_a, g_o)


def _in_proj_kernel(a_ref, wt_ref, o_ref, w_sc, *, act):
    @pl.when(pl.program_id(1) == 0)
    def _():
        w_sc[...] = wt_ref[0].astype(BF16)

    y = lax.dot_general(a_ref[...], w_sc[...], (((1,), (1,)), ((), ())), preferred_element_type=F32)
    if act == "sigmoid":
        y = jax.nn.sigmoid(y)
    o_ref[...] = y.astype(o_ref.dtype)


def _in_proj(a, w_all_t, layer, row_start, n, out_dtype, tm, tn, act=None, name="in_proj"):
    m, k = a.shape
    return pl.pallas_call(
        functools.partial(_in_proj_kernel, act=act),
        grid=(n // tn, m // tm),
        in_specs=[pl.BlockSpec((tm, k), lambda j, i: (i, 0)),
                  pl.BlockSpec((pl.Element(1), pl.Element(tn), pl.Element(k)),
                               lambda j, i: (layer, _aligned(row_start(j), F32_SUBLANES), 0))],
        out_specs=pl.BlockSpec((tm, tn), lambda j, i: (i, j)),
        out_shape=jax.ShapeDtypeStruct((m, n), out_dtype),
        scratch_shapes=[pltpu.VMEM((tn, k), BF16)],
        compiler_params=_cparams(("parallel", "arbitrary")),
        name=name,
    )(a, w_all_t)


def _small_w_kernel(kr_ref, fl_ref, ga_ref, o_ref):
    d = o_ref.shape[1]
    z = lambda n: jnp.zeros((n, d), F32)
    o_ref[...] = jnp.concatenate([kr_ref[0], z(LANES - MLA_ROPE), fl_ref[0], ga_ref[0],
                                  z(LANES - FOX_HEADS - GLA_GATE_RANK)], axis=0).astype(BF16)


def _small_in_proj_weights(w_all_t, layer):
    d = w_all_t.shape[2]

    def rows(i):
        return pl.BlockSpec((pl.Element(1), pl.Element(IN_SPLIT[i]), pl.Element(d)),
                            lambda g: (layer, IN_OFF[i], 0))

    return pl.pallas_call(
        _small_w_kernel,
        grid=(1,),
        in_specs=[rows(2), rows(11), rows(6)],
        out_specs=pl.BlockSpec((A_SMALL, d), lambda g: (0, 0)),
        out_shape=jax.ShapeDtypeStruct((A_SMALL, d), BF16),
        compiler_params=_cparams(("arbitrary",)),
        name="small_in_proj_weights",
    )(w_all_t, w_all_t, w_all_t)


def _pad_lanes(v, n):
    return jnp.pad(v, (0, n - v.shape[0])).reshape(1, n)


def _layer(x, h, tabs, g_next, w_in_all_t, layer, g_cq, w_uq, g_ckv, w_ukv, g_mla_q, g_mla_k, w_a2, b_a, g_gla_o,
           g_fox_q, g_fox_k, b_f, w_branch_all, w_out_all, g_ffn, w_gu_all, w_down_all):
    tn = 1024
    off = IN_OFF
    p_latent = _in_proj(h, w_in_all_t, layer, lambda j: off[0], A_LATENT, F32, 1024, tn, name="in_proj_latent")
    p_small = _matmul_nt(h, _small_in_proj_weights(w_in_all_t, layer), F32, 1024, A_SMALL, name="in_proj_small")
    n_first = (off[6] - off[3]) // tn
    pb = _in_proj(h, w_in_all_t, layer,
                  lambda j: jnp.where(j < n_first, off[3] + j * tn, off[7] + (j - n_first) * tn),
                  B_WIDTH, BF16, 1024, tn, name="in_proj_b")
    gsig = _in_proj(h, w_in_all_t, layer, lambda j: off[12] + j * tn, N_BRANCH * D_MODEL, BF16, 1024, tn,
                    act="sigmoid", name="in_proj_gates")

    wuq_p = jnp.pad(w_uq.reshape(MLA_RANK, MLA_HEADS, MLA_QK),
                    ((0, 0), (0, 0), (0, HEAD_PAD - MLA_QK))).reshape(MLA_RANK, MLA_HEADS * HEAD_PAD).astype(BF16)
    w_ukv3 = w_ukv.reshape(MLA_RANK, MLA_HEADS, MLA_NOPE + MLA_V)
    wuk = w_ukv3[:, :, :MLA_NOPE].reshape(MLA_RANK, MLA_HEADS * MLA_NOPE).astype(BF16)
    wuv_t = w_ukv3[:, :, MLA_NOPE:].reshape(MLA_RANK, MLA_HEADS * MLA_V).T.astype(BF16)
    q_a, k_a, vt_a = _mla_prep(p_latent, p_small, wuq_p, wuk, wuv_t, g_cq.reshape(1, -1), g_ckv.reshape(1, -1),
                               _pad_lanes(g_mla_q, HEAD_PAD), g_mla_k[:MLA_NOPE].reshape(1, -1),
                               _pad_lanes(g_mla_k[MLA_NOPE:], LANES), tabs)
    o_a = _attention(q_a, k_a, vt_a)

    wa_p = jnp.zeros((LANES, GLA_HEADS * GLA_DK), F32).at[MISC_GA:MISC_GA + GLA_GATE_RANK].set(w_a2)
    o_b = _gla(pb, p_small, wa_p, b_a.reshape(1, -1), g_gla_o.reshape(1, -1))

    f_cum, f_cum_t = _fox_cumsum(p_small,_pad_lanes(b_f, LANES))
    q_c, k_c = _fox_prep(pb, f_cum, g_fox_q.reshape(1, -1), g_fox_k.reshape(1, -1))
    vt_c = _value_t_proj(h, w_in_all_t, layer, off[10], FOX_HEADS)
    o_c = _attention(q_c, k_c, vt_c, f_cum_t)

    x1, h2 = _merge_out(o_a, o_b, o_c, gsig, w_branch_all, w_out_all, layer, x, g_ffn)
    act = _swiglu_up(h2, w_gu_all, layer, 1024, 512)
    if g_next is None:
        return _matmul_residual(act, w_down_all, layer, x1, 256, "ffn_down"), None
    return _matmul_residual(act, w_down_all, layer, x1, 256, "ffn_down", g_next=g_next)


def kernel(x, positions, g_mix, w_in, g_cq, w_uq, g_ckv, w_ukv, g_mla_q, g_mla_k, w_a2, b_a, g_gla_o,
           g_fox_q, g_fox_k, b_f, w_branch, w_out, g_ffn, w_gu, w_down):
    bsz, s, d = x.shape
    assert bsz == 1 and d == D_MODEL and s % ATT_TILE == 0
    half = MLA_ROPE // 2
    inv = ROPE_THETA ** (-jnp.arange(half, dtype=F32) / half)
    inv128 = jnp.concatenate([inv, inv, jnp.zeros((LANES - MLA_ROPE,), F32)]).reshape(1, LANES)
    tabs = _rope_tables(positions.reshape(s, 1), inv128)
    depth = w_in.shape[0]
    w_in = jnp.swapaxes(w_in, 1, 2)
    w_branch = w_branch.reshape(depth, N_BRANCH * BRANCH_WIDTH, D_MODEL).astype(BF16)
    w_out = w_out.astype(BF16)
    w_down = w_down.astype(BF16)
    xcur = x.reshape(s, d)
    hcur = _rmsnorm(xcur, g_mix[0])
    for l in range(depth):
        g_next = g_mix[l + 1] if l + 1 < depth else None
        xcur, hcur = _layer(xcur, hcur, tabs, g_next, w_in, l, g_cq[l], w_uq[l], g_ckv[l], w_ukv[l], g_mla_q[l],
                            g_mla_k[l], w_a2[l], b_a[l], g_gla_o[l], g_fox_q[l], g_fox_k[l], b_f[l], w_branch,
                            w_out, g_ffn[l], w_gu, w_down)
    return xcur.reshape(bsz, s, d)
```

```python
import functools

import jax
import jax.numpy as jnp
from jax import lax
from jax.experimental import pallas as pl
from jax.experimental.pallas import tpu as pltpu

F32 = jnp.float32
BF16 = jnp.bfloat16

D_MODEL = 2048
BRANCH_WIDTH = D_MODEL // 2
N_BRANCH = 3
MLA_NOPE = 128
MLA_ROPE = 64
MLA_V = 128
MLA_HEADS = BRANCH_WIDTH // MLA_V
MLA_QK = MLA_NOPE + MLA_ROPE
MLA_RANK = 512
ROPE_THETA = 10000.0
GLA_HEADS = 4
GLA_DV = BRANCH_WIDTH // GLA_HEADS
GLA_DK = GLA_DV // 2
GLA_GATE_RANK = 16
GLA_TAU = 16.0
FOX_DH = 128
FOX_HEADS = BRANCH_WIDTH // FOX_DH
FFN_HIDDEN = ((8 * D_MODEL + 3 * 256 - 1) // (3 * 256)) * 256
EPS = 1e-6
NEG_INF = -1e30

LANES = 128
F32_SUBLANES = 8
BF16_SUBLANES = 16
HEAD_PAD = 256
VMEM_LIMIT = 48 * 1024 * 1024
VMEM_LIMIT_BIG = 56 * 1024 * 1024

IN_SPLIT = (MLA_RANK, MLA_RANK, MLA_ROPE, GLA_HEADS * GLA_DK, GLA_HEADS * GLA_DK, GLA_HEADS * GLA_DV, GLA_GATE_RANK,
            GLA_HEADS * GLA_DV, FOX_HEADS * FOX_DH, FOX_HEADS * FOX_DH, FOX_HEADS * FOX_DH, FOX_HEADS,
            N_BRANCH * D_MODEL)
IN_OFF = tuple(sum(IN_SPLIT[:i]) for i in range(len(IN_SPLIT) + 1))

A_CQ, A_CKV = 0, 512
A_LATENT = 1024
A_KR, A_MISC = 0, 128
A_SMALL = 256
MISC_FL = 0
MISC_GA = FOX_HEADS
B_GQ, B_GK, B_GV, B_GR, B_FQ, B_FK = 0, 512, 1024, 2048, 3072, 4096
B_WIDTH = 5120

ATT_TILE = 1024
ATT_SUB = 256
ATT_VROWS = LANES + BF16_SUBLANES
LOG2E = 1.4426950408889634

GLA_CHUNK = 64
GLA_SUB = 8
GLA_BLOCK = 512


def _aligned(start, m):
    return start if isinstance(start, int) else pl.multiple_of(start, m)


def _cparams(sem):
    return pltpu.CompilerParams(dimension_semantics=sem, vmem_limit_bytes=VMEM_LIMIT)


def _split3(x):
    x1 = x.astype(BF16)
    r1 = x - x1.astype(F32)
    x2 = r1.astype(BF16)
    x3 = (r1 - x2.astype(F32)).astype(BF16)
    return x1, x2, x3


def _sigmoid(x):
    return 0.5 * jnp.tanh(0.5 * x) + 0.5


def _log_sigmoid(z):
    return -(jnp.maximum(-z, 0.0) + jnp.log1p(jnp.exp(-jnp.abs(z))))


def _store_values_t(vt_out, vt):
    rows = vt.shape[1]
    r = lax.broadcasted_iota(jnp.int32, (BF16_SUBLANES, rows), 0)
    ones_rows = jnp.where(r == 0, 1.0, 0.0).astype(BF16)
    for h in range(vt.shape[0] // LANES):
        vt_out[h, 0, 0:LANES, :] = vt[h * LANES:(h + 1) * LANES, :].astype(BF16)
        vt_out[h, 0, LANES:ATT_VROWS, :] = ones_rows


def _rmsnorm_kernel(x_ref, g_ref, o_ref):
    x = x_ref[...]
    ms = jnp.mean(x * x, axis=-1, keepdims=True)
    o_ref[...] = (x * lax.rsqrt(ms + EPS) * g_ref[...]).astype(o_ref.dtype)


def _rmsnorm(x, g, tm=512):
    m, d = x.shape
    return pl.pallas_call(
        _rmsnorm_kernel,
        grid=(m // tm,),
        in_specs=[pl.BlockSpec((tm, d), lambda i: (i, 0)),
                  pl.BlockSpec((1, d), lambda i: (0, 0))],
        out_specs=pl.BlockSpec((tm, d), lambda i: (i, 0)),
        out_shape=jax.ShapeDtypeStruct((m, d), BF16),
        compiler_params=_cparams(("parallel",)),
        name="rmsnorm",
    )(x, g.reshape(1, d))


def _mm_nt_kernel(a_ref, wt_ref, o_ref):
    y = lax.dot_general(a_ref[...], wt_ref[...], (((1,), (1,)), ((), ())), preferred_element_type=F32)
    o_ref[...] = y.astype(o_ref.dtype)


def _matmul_nt(a, w_t, out_dtype, tm, tn, name):
    m, k = a.shape
    n = w_t.shape[0]
    return pl.pallas_call(
        _mm_nt_kernel,
        grid=(m // tm, n // tn),
        in_specs=[pl.BlockSpec((tm, k), lambda i, j: (i, 0)),
                  pl.BlockSpec((tn, k), lambda i, j: (j, 0))],
        out_specs=pl.BlockSpec((tm, tn), lambda i, j: (i, j)),
        out_shape=jax.ShapeDtypeStruct((m, n), out_dtype),
        compiler_params=_cparams(("parallel", "arbitrary")),
        name=name,
    )(a, w_t)


def _mm_res_kernel(*refs, with_norm):
    if with_norm:
        a_ref, w_ref, r_ref, g_ref, o_ref, h_ref = refs
    else:
        a_ref, w_ref, r_ref, o_ref = refs
    y = r_ref[...] + jnp.dot(a_ref[...], w_ref[...], preferred_element_type=F32)
    o_ref[...] = y
    if with_norm:
        ms = jnp.mean(y * y, axis=-1, keepdims=True)
        h_ref[...] = (y * lax.rsqrt(ms + EPS) * g_ref[...]).astype(h_ref.dtype)


def _matmul_residual(a, w_all, layer, res, tm, name, g_next=None):
    m, k = a.shape
    n = w_all.shape[2]
    with_norm = g_next is not None
    row = pl.BlockSpec((tm, n), lambda i: (i, 0))
    in_specs = [pl.BlockSpec((tm, k), lambda i: (i, 0)),
                pl.BlockSpec((None, k, n), lambda i: (layer, 0, 0), pipeline_mode=pl.Buffered(1)),
                row]
    args = [a, w_all, res]
    out_specs, out_shape = row, jax.ShapeDtypeStruct((m, n), F32)
    if with_norm:
        in_specs.append(pl.BlockSpec((1, n), lambda i: (0, 0)))
        args.append(g_next.reshape(1, n))
        out_specs, out_shape = [row, row], [out_shape, jax.ShapeDtypeStruct((m, n), BF16)]
    return pl.pallas_call(
        functools.partial(_mm_res_kernel, with_norm=with_norm),
        grid=(m // tm,),
        in_specs=in_specs,
        out_specs=out_specs,
        out_shape=out_shape,
        compiler_params=pltpu.CompilerParams(dimension_semantics=("parallel",), vmem_limit_bytes=VMEM_LIMIT_BIG),
        name=name,
    )(*args)


def _swiglu_kernel(a_ref, wg_ref, wu_ref, o_ref, wg_sc, wu_sc):
    @pl.when(pl.program_id(1) == 0)
    def _():
        wg_sc[...] = wg_ref[...].astype(BF16)
        wu_sc[...] = wu_ref[...].astype(BF16)

    a = a_ref[...]
    g = jnp.dot(a, wg_sc[...], preferred_element_type=F32)
    u = jnp.dot(a, wu_sc[...], preferred_element_type=F32)
    o_ref[...] = (g * _sigmoid(g) * u).astype(o_ref.dtype)


def _swiglu_up(a, w_gu_all, layer, tm, tn):
    m, k = a.shape
    hidden = w_gu_all.shape[2] // 2
    nj = hidden // tn
    return pl.pallas_call(
        _swiglu_kernel,
        grid=(nj, m // tm),
        in_specs=[pl.BlockSpec((tm, k), lambda j, i: (i, 0)),
                  pl.BlockSpec((None, k, tn), lambda j, i: (layer, 0, j)),
                  pl.BlockSpec((None, k, tn), lambda j, i: (layer, 0, j + nj))],
        out_specs=pl.BlockSpec((tm, tn), lambda j, i: (i, j)),
        out_shape=jax.ShapeDtypeStruct((m, hidden), BF16),
        scratch_shapes=[pltpu.VMEM((k, tn), BF16), pltpu.VMEM((k, tn), BF16)],
        compiler_params=_cparams(("parallel", "arbitrary")),
        name="ffn_up",
    )(a, w_gu_all, w_gu_all)


def _merge_out_kernel(oa_ref, ob_ref, oc_ref, gs_ref, wb_ref, wo_ref, x_ref, g_ref, x1_ref, h_ref):
    bw = oa_ref.shape[1]
    d = x_ref.shape[1]
    merged = None
    for n, o_ref in enumerate((oa_ref, ob_ref, oc_ref)):
        y = jnp.dot(o_ref[...], wb_ref[n * bw:(n + 1) * bw, :], preferred_element_type=F32)
        y = y * gs_ref[:, n * d:(n + 1) * d].astype(F32)
        merged = y if merged is None else merged + y
    x1 = x_ref[...] + jnp.dot(merged.astype(BF16), wo_ref[...], preferred_element_type=F32)
    x1_ref[...] = x1
    ms = jnp.mean(x1 * x1, axis=-1, keepdims=True)
    h_ref[...] = (x1 * lax.rsqrt(ms + EPS) * g_ref[...]).astype(h_ref.dtype)


def _merge_out(o_a, o_b, o_c, gsig, w_branch_all, w_out_all, layer, x, g_next, tm=256):
    m, bw = o_a.shape
    d = x.shape[1]
    o_spec = pl.BlockSpec((tm, bw), lambda i: (i, 0))
    row = pl.BlockSpec((tm, d), lambda i: (i, 0))
    return pl.pallas_call(
        _merge_out_kernel,
        grid=(m // tm,),
        in_specs=[o_spec, o_spec, o_spec,
                  pl.BlockSpec((tm, N_BRANCH * d), lambda i: (i, 0)),
                  pl.BlockSpec((None, N_BRANCH * bw, d), lambda i: (layer, 0, 0), pipeline_mode=pl.Buffered(1)),
                  pl.BlockSpec((None, d, d), lambda i: (layer, 0, 0), pipeline_mode=pl.Buffered(1)),
                  row,
                  pl.BlockSpec((1, d), lambda i: (0, 0))],
        out_specs=[row, row],
        out_shape=[jax.ShapeDtypeStruct((m, d), F32), jax.ShapeDtypeStruct((m, d), BF16)],
        compiler_params=pltpu.CompilerParams(dimension_semantics=("parallel",), vmem_limit_bytes=VMEM_LIMIT_BIG),
        name="merge_out_proj",
    )(o_a, o_b, o_c, gsig, w_branch_all, w_out_all, x, g_next.reshape(1, d))


def _rope_table_kernel(pos_ref, inv_ref, c_ref, s1_ref, s2_ref):
    ang = pos_ref[...].astype(F32) * inv_ref[...]
    c = jnp.cos(ang)
    s = jnp.sin(ang)
    lane = lax.broadcasted_iota(jnp.int32, ang.shape, 1)
    half = MLA_ROPE // 2
    c_ref[...] = jnp.where(lane < MLA_ROPE, c, 0.0)
    s1_ref[...] = jnp.where((lane >= half) & (lane < MLA_ROPE), s, 0.0)
    s2_ref[...] = jnp.where(lane < half, -s, 0.0)


def _rope_tables(pos_col, inv128, tm=512):
    s = pos_col.shape[0]
    spec = pl.BlockSpec((tm, LANES), lambda i: (i, 0))
    shp = jax.ShapeDtypeStruct((s, LANES), F32)
    return pl.pallas_call(
        _rope_table_kernel,
        grid=(s // tm,),
        in_specs=[pl.BlockSpec((tm, 1), lambda i: (i, 0)),
                  pl.BlockSpec((1, LANES), lambda i: (0, 0))],
        out_specs=[spec, spec, spec],
        out_shape=[shp, shp, shp],
        compiler_params=_cparams(("parallel",)),
        name="rope_tables",
    )(pos_col, inv128)


def _rope128(x, c, s1, s2):
    half = MLA_ROPE // 2
    return x * c + pltpu.roll(x, half, 1) * s1 + pltpu.roll(x, LANES - half, 1) * s2


def _mla_prep_kernel(cq_ref, ckv_ref, kr_ref, wuq_ref, wuk_ref, wuvt_ref, gcq_ref, gckv_ref, gq_ref, gk_ref, gkr_ref,
                     c_ref, s1_ref, s2_ref, q_out, k_out, vt_out):
    def norm(x, g):
        return x * lax.rsqrt(jnp.mean(x * x, axis=-1, keepdims=True) + EPS) * g

    cqn = norm(cq_ref[...], gcq_ref[...]).astype(BF16)
    ckvn = norm(ckv_ref[...], gckv_ref[...]).astype(BF16)
    qa = jnp.dot(cqn, wuq_ref[...], preferred_element_type=F32)
    ka = jnp.dot(ckvn, wuk_ref[...], preferred_element_type=F32)
    _store_values_t(vt_out, lax.dot_general(wuvt_ref[...], ckvn, (((1,), (1,)), ((), ())),
                                            preferred_element_type=F32))
    krp = kr_ref[...]
    kr_ss = jnp.sum(krp * krp, axis=-1, keepdims=True)
    c, s1, s2 = c_ref[...], s1_ref[...], s2_ref[...]
    gq, gk, gkr = gq_ref[...], gk_ref[...], gkr_ref[...]
    scale = MLA_QK ** -0.5 * LOG2E
    kr_roped = _rope128(krp * gkr, c, s1, s2)
    for h in range(MLA_HEADS):
        qh = qa[:, h * HEAD_PAD:(h + 1) * HEAD_PAD]
        rq = lax.rsqrt(jnp.sum(qh * qh, axis=-1, keepdims=True) * (1.0 / MLA_QK) + EPS)
        qn = qh * rq * gq
        q_out[h, :, 0:LANES] = (qn[:, 0:LANES] * scale).astype(BF16)
        q_out[h, :, LANES:HEAD_PAD] = (_rope128(qn[:, LANES:HEAD_PAD], c, s1, s2) * scale).astype(BF16)
        kn = ka[:, h * MLA_NOPE:(h + 1) * MLA_NOPE]
        rk = lax.rsqrt((jnp.sum(kn * kn, axis=-1, keepdims=True) + kr_ss) * (1.0 / MLA_QK) + EPS)
        k_out[h, :, 0:LANES] = (kn * rk * gk).astype(BF16)
        k_out[h, :, LANES:HEAD_PAD] = (kr_roped * rk).astype(BF16)


def _vt_spec(heads, tm):
    per_tile = ATT_TILE // tm
    return pl.BlockSpec((heads, 1, ATT_VROWS, tm), lambda i: (0, i // per_tile, 0, i % per_tile))


def _mla_prep(p_latent, p_small, wuq_p, wuk, wuv_t, g_cq, g_ckv, g_q, g_k, g_kr, tabs, tm=256):
    s = p_latent.shape[0]
    h = MLA_HEADS
    row = lambda w: pl.BlockSpec((1, w), lambda i: (0, 0))
    tab = pl.BlockSpec((tm, LANES), lambda i: (i, 0))
    return pl.pallas_call(
        _mla_prep_kernel,
        grid=(s // tm,),
        in_specs=[pl.BlockSpec((tm, MLA_RANK), lambda i: (i, A_CQ // MLA_RANK)),
                  pl.BlockSpec((tm, MLA_RANK), lambda i: (i, A_CKV // MLA_RANK)),
                  pl.BlockSpec((tm, LANES), lambda i: (i, A_KR // LANES)),
                  pl.BlockSpec(wuq_p.shape, lambda i: (0, 0)),
                  pl.BlockSpec(wuk.shape, lambda i: (0, 0)),
                  pl.BlockSpec(wuv_t.shape, lambda i: (0, 0)),
                  row(MLA_RANK), row(MLA_RANK), row(HEAD_PAD), row(LANES), row(LANES),
                  tab, tab, tab],
        out_specs=[pl.BlockSpec((h, tm, HEAD_PAD), lambda i: (0, i, 0)),
                   pl.BlockSpec((h, tm, HEAD_PAD), lambda i: (0, i, 0)),
                   _vt_spec(h, tm)],
        out_shape=[jax.ShapeDtypeStruct((h, s, HEAD_PAD), BF16),
                   jax.ShapeDtypeStruct((h, s, HEAD_PAD), BF16),
                   jax.ShapeDtypeStruct((h, s // ATT_TILE, ATT_VROWS, ATT_TILE), BF16)],
        compiler_params=_cparams(("parallel",)),
        name="mla_prep",
    )(p_latent, p_latent, p_small, wuq_p, wuk, wuv_t, g_cq, g_ckv, g_q, g_k, g_kr, *tabs)


def _fox_cumsum_kernel(misc_ref, bf_ref, f_ref, ft_ref, carry_ref):
    blk = misc_ref.shape[0]

    @pl.when(pl.program_id(0) == 0)
    def _():
        carry_ref[...] = jnp.zeros(carry_ref.shape, F32)

    r = lax.broadcasted_iota(jnp.int32, (blk, blk), 0)
    c = lax.broadcasted_iota(jnp.int32, (blk, blk), 1)
    tri = jnp.where(c <= r, 1.0, 0.0).astype(BF16)
    x1, x2, x3 = _split3(_log_sigmoid(misc_ref[...] + bf_ref[...]))
    cs = (jnp.dot(tri, x1, preferred_element_type=F32)
          + jnp.dot(tri, x2, preferred_element_type=F32)
          + jnp.dot(tri, x3, preferred_element_type=F32)) + carry_ref[...]
    f_ref[...] = cs
    ft_ref[...] = cs.T
    carry_ref[...] = cs[blk - 1:blk, :]


def _fox_cumsum(p_small, bf128, blk=512):
    s = p_small.shape[0]
    return pl.pallas_call(
        _fox_cumsum_kernel,
        grid=(s // blk,),
        in_specs=[pl.BlockSpec((blk, LANES), lambda i: (i, A_MISC // LANES)),
                  pl.BlockSpec((1, LANES), lambda i: (0, 0))],
        out_specs=[pl.BlockSpec((blk, LANES), lambda i: (i, 0)),
                   pl.BlockSpec((LANES, blk), lambda i: (0, i))],
        out_shape=[jax.ShapeDtypeStruct((s, LANES), F32), jax.ShapeDtypeStruct((LANES, s), F32)],
        scratch_shapes=[pltpu.VMEM((1, LANES), F32)],
        compiler_params=_cparams(("arbitrary",)),
        name="fox_cumsum",
    )(p_small, bf128)


def _fox_prep_kernel(fq_ref, fk_ref, f_ref, gq_ref, gk_ref, sel_ref, q_out, k_out):
    tm = fq_ref.shape[0]
    lane = lax.broadcasted_iota(jnp.int32, (tm, LANES), 1)
    ones_tail = jnp.where(lane < 3, 1.0, 0.0).astype(BF16)
    pieces = jnp.concatenate(_split3(f_ref[...] * (-LOG2E)), axis=1)
    scale = FOX_DH ** -0.5 * LOG2E
    gq, gk = gq_ref[...], gk_ref[...]
    for h in range(FOX_HEADS):
        sl = slice(h * FOX_DH, (h + 1) * FOX_DH)
        q = fq_ref[:, sl].astype(F32)
        k = fk_ref[:, sl].astype(F32)
        qn = q * lax.rsqrt(jnp.mean(q * q, axis=-1, keepdims=True) + EPS) * gq
        kn = k * lax.rsqrt(jnp.mean(k * k, axis=-1, keepdims=True) + EPS) * gk
        q_out[h, :, 0:LANES] = (qn * scale).astype(BF16)
        q_out[h, :, LANES:HEAD_PAD] = ones_tail
        k_out[h, :, 0:LANES] = kn.astype(BF16)
        k_out[h, :, LANES:HEAD_PAD] = jnp.dot(pieces, sel_ref[h], preferred_element_type=F32).astype(BF16)


def _fox_prep(pb, f_cum, g_q, g_k, tm=256):
    s = pb.shape[0]
    h = FOX_HEADS
    w = FOX_HEADS * FOX_DH
    row = pl.BlockSpec((1, FOX_DH), lambda i: (0, 0))
    hh, pp = jnp.meshgrid(jnp.arange(h), jnp.arange(3), indexing="ij")
    sel = jnp.zeros((h, 3 * LANES, LANES), BF16).at[hh, pp * LANES + hh, pp].set(1.0)
    return pl.pallas_call(
        _fox_prep_kernel,
        grid=(s // tm,),
        in_specs=[pl.BlockSpec((tm, w), lambda i: (i, B_FQ // w)),
                  pl.BlockSpec((tm, w), lambda i: (i, B_FK // w)),
                  pl.BlockSpec((tm, LANES), lambda i: (i, 0)),
                  row, row,
                  pl.BlockSpec(sel.shape, lambda i: (0, 0, 0))],
        out_specs=[pl.BlockSpec((h, tm, HEAD_PAD), lambda i: (0, i, 0)),
                   pl.BlockSpec((h, tm, HEAD_PAD), lambda i: (0, i, 0))],
        out_shape=[jax.ShapeDtypeStruct((h, s, HEAD_PAD), BF16),
                   jax.ShapeDtypeStruct((h, s, HEAD_PAD), BF16)],
        compiler_params=_cparams(("parallel",)),
        name="fox_prep",
    )(pb, pb, f_cum, g_q, g_k, sel)


def _value_t_kernel(h_ref, w_ref, vt_out, w_sc):
    @pl.when(pl.program_id(0) == 0)
    def _():
        w_sc[...] = w_ref[0].astype(BF16)

    vt = lax.dot_general(w_sc[...], h_ref[...], (((1,), (1,)), ((), ())), preferred_element_type=F32)
    _store_values_t(vt_out, vt)


def _value_t_proj(h, w_all_t, layer, row_start, heads, tm=512):
    s, d = h.shape
    nv = heads * LANES
    return pl.pallas_call(
        _value_t_kernel,
        grid=(s // tm,),
        in_specs=[pl.BlockSpec((tm, d), lambda i: (i, 0)),
                  pl.BlockSpec((pl.Element(1), pl.Element(nv), pl.Element(d)), lambda i: (layer, row_start, 0))],
        out_specs=_vt_spec(heads, tm),
        out_shape=jax.ShapeDtypeStruct((heads, s // ATT_TILE, ATT_VROWS, ATT_TILE), BF16),
        scratch_shapes=[pltpu.VMEM((nv, d), BF16)],
        compiler_params=_cparams(("arbitrary",)),
        name="fox_value_t",
    )(h, w_all_t)


def _attn_kernel(*refs, tq, use_f):
    if use_f:
        q_ref, k_ref, vt_ref, f_ref, o_ref, m_sc, acc_sc, s_a, s_b = refs
    else:
        q_ref, k_ref, vt_ref, o_ref, m_sc, acc_sc, s_a, s_b = refs
    ns = tq // ATT_SUB
    nq = q_ref.shape[1] // tq

    def reset():
        m_sc[...] = jnp.full(m_sc.shape, NEG_INF, F32)
        acc_sc[...] = jnp.zeros(acc_sc.shape, F32)

    def logits(dst, qi, kb):
        dst[...] = lax.dot_general(k_ref[0, pl.ds(_aligned(kb * tq, tq), tq), :],
                                   q_ref[0, pl.ds(_aligned(qi * tq, tq), tq), :],
                                   (((1,), (1,)), ((), ())), preferred_element_type=F32)

    def update(src, j, qi, kb, k0, size, causal):
        s = src[k0:k0 + size, j * ATT_SUB:(j + 1) * ATT_SUB]
        if causal:
            key = lax.broadcasted_iota(jnp.int32, (size, ATT_SUB), 0)
            qry = lax.broadcasted_iota(jnp.int32, (size, ATT_SUB), 1)
            s = jnp.where(key <= qry, s, NEG_INF)
        m_old = m_sc[j]
        m_cur = jnp.max(s, axis=0, keepdims=True)
        if use_f:
            ft = f_ref[pl.ds(qi, 1), j * ATT_SUB:(j + 1) * ATT_SUB] * LOG2E
            m_new = jnp.maximum(m_old, m_cur + ft)
            shift = m_new - ft
        else:
            m_new = jnp.maximum(m_old, m_cur)
            shift = m_new
        p = jnp.exp2(s - shift).astype(BF16)
        alpha = jnp.exp2(m_old - m_new)
        acc_sc[j] = alpha * acc_sc[j] + jnp.dot(vt_ref[0, kb, :, k0:k0 + size], p, preferred_element_type=F32)
        m_sc[j] = m_new

    def full_tile(src, qi, kb):
        for j in range(ns):
            update(src, j, qi, kb, 0, tq, False)

    def diagonal_tile(src, qi):
        for j in range(ns):
            if j > 0:
                update(src, j, qi, qi, 0, j * ATT_SUB, False)
            update(src, j, qi, qi, j * ATT_SUB, ATT_SUB, True)
        for j in range(ns):
            acc = acc_sc[j]
            o_t = acc[0:LANES, :] / acc[LANES:LANES + 1, :]
            o_ref[pl.ds(_aligned(qi * tq + j * ATT_SUB, ATT_SUB), ATT_SUB), :] = o_t.T.astype(o_ref.dtype)
        reset()

    def query_tile(qi, cur, oth, odd):
        def pair(u, carry):
            t = 2 * u
            logits(oth, qi, t + 1)
            full_tile(cur, qi, t)
            logits(cur, qi, t + 2)
            full_tile(oth, qi, t + 1)
            return carry

        lax.fori_loop(0, qi // 2, pair, 0)
        q_next = jnp.minimum(qi + 1, nq - 1)
        if odd:
            logits(oth, qi, qi)
            full_tile(cur, qi, qi - 1)
            logits(cur, q_next, 0)
            diagonal_tile(oth, qi)
        else:
            logits(oth, q_next, 0)
            diagonal_tile(cur, qi)

    assert nq % 4 == 0
    reset()
    logits(s_a, 0, 0)

    def four_query_tiles(g, carry):
        q0 = 4 * g
        query_tile(q0, s_a, s_b, False)
        query_tile(q0 + 1, s_b, s_a, True)
        query_tile(q0 + 2, s_b, s_a, False)
        query_tile(q0 + 3, s_a, s_b, True)
        return carry

    lax.fori_loop(0, nq // 4, four_query_tiles, 0)


def _attention(q, k, vt, f_t=None):
    hh, s, dk = q.shape
    _, nt, vrows, tq = vt.shape
    use_f = f_t is not None
    in_specs = [pl.BlockSpec((1, s, dk), lambda h: (h, 0, 0)),
                pl.BlockSpec((1, s, dk), lambda h: (h, 0, 0)),
                pl.BlockSpec((1, nt, vrows, tq), lambda h: (h, 0, 0, 0))]
    args = [q, k, vt]
    if use_f:
        in_specs.append(pl.BlockSpec((None, nt, tq), lambda h: (h, 0, 0)))
        args.append(f_t.reshape(f_t.shape[0], nt, tq))
    ns = tq // ATT_SUB
    return pl.pallas_call(
        functools.partial(_attn_kernel, tq=tq, use_f=use_f),
        grid=(hh,),
        in_specs=in_specs,
        out_specs=pl.BlockSpec((s, LANES), lambda h: (0, h)),
        out_shape=jax.ShapeDtypeStruct((s, hh * LANES), BF16),
        scratch_shapes=[pltpu.VMEM((ns, 1, ATT_SUB), F32), pltpu.VMEM((ns, vrows, ATT_SUB), F32),
                        pltpu.VMEM((tq, tq), F32), pltpu.VMEM((tq, tq), F32)],
        compiler_params=_cparams(("parallel",)),
        name="fox_attention" if use_f else "mla_attention",
    )(*args)


def _gla_kernel(q_ref, k_ref, v_ref, r_ref, misc_ref, wa_ref, ba_ref, go_ref, o_ref,
                st_ref, la_ref, b_ref, qe_ref, a_ref, u_ref, dl_ref, stb_ref):
    tb = q_ref.shape[0]
    nchunk = tb // GLA_CHUNK
    nsub = GLA_CHUNK // GLA_SUB
    C = GLA_CHUNK

    @pl.when(pl.program_id(1) == 0)
    def _():
        st_ref[...] = jnp.zeros(st_ref.shape, F32)

    m1, m2, _ = _split3(misc_ref[...])
    w1, w2, _ = _split3(wa_ref[...])
    z = (jnp.dot(m1, w1, preferred_element_type=F32) + jnp.dot(m1, w2, preferred_element_type=F32)
         + jnp.dot(m2, w1, preferred_element_type=F32)) + ba_ref[...]
    la_ref[...] = _log_sigmoid(z) * (LOG2E / GLA_TAU)

    r_i = lax.broadcasted_iota(jnp.int32, (C, C), 0)
    c_i = lax.broadcasted_iota(jnp.int32, (C, C), 1)
    tri = jnp.where(c_i <= r_i, 1.0, 0.0).astype(BF16)
    sub_row = lax.broadcasted_iota(jnp.int32, (GLA_SUB, GLA_DK), 0)
    col_sub = lax.broadcasted_iota(jnp.int32, (GLA_SUB, C), 1)
    scale = GLA_DK ** -0.5
    g_o = go_ref[...]

    chunk_rows = [slice(ci * C, (ci + 1) * C) for ci in range(nchunk)]

    for rows in chunk_rows:
        l1, l2, l3 = _split3(la_ref[rows, :])
        b_ref[rows, :] = (jnp.dot(tri, l1, preferred_element_type=F32) + jnp.dot(tri, l2, preferred_element_type=F32)
                          + jnp.dot(tri, l3, preferred_element_type=F32))

    for ci, rows in enumerate(chunk_rows):
        b = b_ref[rows, :]
        qc = q_ref[rows, :].astype(F32) * scale
        kc = k_ref[rows, :].astype(F32)
        b_last = b[C - 1:C, :]
        qe_ref[rows, :] = (qc * jnp.exp2(b)).astype(BF16)
        ke = (kc * jnp.exp2(b_last - b)).astype(BF16)
        u_ref[ci] = lax.dot_general(v_ref[rows, :], ke, (((0,), (0,)), ((), ())), preferred_element_type=F32)
        dl_ref[ci] = jnp.exp2(b_last)

        a_rows = []
        for si in range(nsub):
            lo = si * GLA_SUB
            qs, ks, bs = qc[lo:lo + GLA_SUB], kc[lo:lo + GLA_SUB], b[lo:lo + GLA_SUB]
            a = jnp.zeros((GLA_SUB, C), F32)
            for j in range(GLA_SUB):
                e = jnp.exp2(jnp.where(sub_row >= j, bs - bs[j:j + 1, :], NEG_INF))
                w = jnp.sum(qs * (ks[j:j + 1, :] * e), axis=-1, keepdims=True)
                a = jnp.where(col_sub == lo + j, w, a)
            if si > 0:
                ref_row = b[lo - 1:lo, :]
                q_i = qs * jnp.exp2(bs - ref_row)
                k_j = jnp.concatenate([kc[0:lo] * jnp.exp2(ref_row - b[0:lo]),
                                       jnp.zeros((C - lo, GLA_DK), F32)], axis=0)
                a = a + lax.dot_general(q_i.astype(BF16), k_j.astype(BF16), (((1,), (1,)), ((), ())),
                                        preferred_element_type=F32)
            a_rows.append(a)
        a_ref[rows, :] = jnp.concatenate(a_rows, axis=0).astype(BF16)

    st = st_ref[...]
    for ci in range(nchunk):
        stb_ref[ci] = st.astype(BF16)
        st = st * dl_ref[ci] + u_ref[ci]
    st_ref[...] = st

    for ci, rows in enumerate(chunk_rows):
        o = lax.dot_general(qe_ref[rows, :], stb_ref[ci], (((1,), (1,)), ((), ())), preferred_element_type=F32)
        o = o + jnp.dot(a_ref[rows, :], v_ref[rows, :], preferred_element_type=F32)
        on = o * lax.rsqrt(jnp.mean(o * o, axis=-1, keepdims=True) + EPS) * g_o
        r = r_ref[rows, :].astype(F32)
        o_ref[rows, :] = (on * (r * _sigmoid(r))).astype(o_ref.dtype)


def _gla(pb, p_small, wa_p, b_a, g_o, tb=GLA_BLOCK):
    s = pb.shape[0]
    nchunk = tb // GLA_CHUNK
    return pl.pallas_call(
        _gla_kernel,
        grid=(GLA_HEADS, s // tb),
        in_specs=[pl.BlockSpec((tb, GLA_DK), lambda h, i: (i, B_GQ // GLA_DK + h)),
                  pl.BlockSpec((tb, GLA_DK), lambda h, i: (i, B_GK // GLA_DK + h)),
                  pl.BlockSpec((tb, GLA_DV), lambda h, i: (i, B_GV // GLA_DV + h)),
                  pl.BlockSpec((tb, GLA_DV), lambda h, i: (i, B_GR // GLA_DV + h)),
                  pl.BlockSpec((tb, LANES), lambda h, i: (i, A_MISC // LANES)),
                  pl.BlockSpec((LANES, GLA_DK), lambda h, i: (0, h)),
                  pl.BlockSpec((1, GLA_DK), lambda h, i: (0, h)),
                  pl.BlockSpec((1, GLA_DV), lambda h, i: (0, 0))],
        out_specs=pl.BlockSpec((tb, GLA_DV), lambda h, i: (i, h)),
        out_shape=jax.ShapeDtypeStruct((s, GLA_HEADS * GLA_DV), BF16),
        scratch_shapes=[pltpu.VMEM((GLA_DV, GLA_DK), F32),
                        pltpu.VMEM((tb, GLA_DK), F32),
                        pltpu.VMEM((tb, GLA_DK), F32),
                        pltpu.VMEM((tb, GLA_DK), BF16),
                        pltpu.VMEM((tb, GLA_CHUNK), BF16),
                        pltpu.VMEM((nchunk, GLA_DV, GLA_DK), F32),
                        pltpu.VMEM((nchunk, 1, GLA_DK), F32),
                        pltpu.VMEM((nchunk, GLA_DV, GLA_DK), BF16)],
        compiler_params=_cparams(("parallel", "arbitrary")),
        name="gla",
    )(pb, pb, pb, pb, p_small, wa_p, b_a, g_o)


def _in_proj_kernel(a_ref, wt_ref, o_ref, w_sc, *, act):
    @pl.when(pl.program_id(1) == 0)
    def _():
        w_sc[...] = wt_ref[0].astype(BF16)

    y = lax.dot_general(a_ref[...], w_sc[...], (((1,), (1,)), ((), ())), preferred_element_type=F32)
    if act == "sigmoid":
        y = _sigmoid(y)
    o_ref[...] = y.astype(o_ref.dtype)


def _in_proj(a, w_all_t, layer, row_start, n, out_dtype, tm, tn, act=None, name="in_proj"):
    m, k = a.shape
    return pl.pallas_call(
        functools.partial(_in_proj_kernel, act=act),
        grid=(n // tn, m // tm),
        in_specs=[pl.BlockSpec((tm, k), lambda j, i: (i, 0)),
                  pl.BlockSpec((pl.Element(1), pl.Element(tn), pl.Element(k)),
                               lambda j, i: (layer, _aligned(row_start(j), F32_SUBLANES), 0))],
        out_specs=pl.BlockSpec((tm, tn), lambda j, i: (i, j)),
        out_shape=jax.ShapeDtypeStruct((m, n), out_dtype),
        scratch_shapes=[pltpu.VMEM((tn, k), BF16)],
        compiler_params=_cparams(("parallel", "arbitrary")),
        name=name,
    )(a, w_all_t)


def _small_w_kernel(kr_ref, fl_ref, ga_ref, o_ref):
    d = o_ref.shape[1]
    z = lambda n: jnp.zeros((n, d), F32)
    o_ref[...] = jnp.concatenate([kr_ref[0], z(LANES - MLA_ROPE), fl_ref[0], ga_ref[0],
                                  z(LANES - FOX_HEADS - GLA_GATE_RANK)], axis=0).astype(BF16)


def _small_in_proj_weights(w_all_t, layer):
    d = w_all_t.shape[2]

    def rows(i):
        return pl.BlockSpec((pl.Element(1), pl.Element(IN_SPLIT[i]), pl.Element(d)),
                            lambda g: (layer, IN_OFF[i], 0))

    return pl.pallas_call(
        _small_w_kernel,
        grid=(1,),
        in_specs=[rows(2), rows(11), rows(6)],
        out_specs=pl.BlockSpec((A_SMALL, d), lambda g: (0, 0)),
        out_shape=jax.ShapeDtypeStruct((A_SMALL, d), BF16),
        compiler_params=_cparams(("arbitrary",)),
        name="small_in_proj_weights",
    )(w_all_t, w_all_t, w_all_t)


def _pad_lanes(v, n):
    return jnp.pad(v, (0, n - v.shape[0])).reshape(1, n)


def _layer(x, h, tabs, g_next, w_in_all_t, layer, g_cq, w_uq, g_ckv, w_ukv, g_mla_q, g_mla_k, w_a2, b_a, g_gla_o,
           g_fox_q, g_fox_k, b_f, w_branch_all, w_out_all, g_ffn, w_gu_all, w_down_all):
    tn = 1024
    off = IN_OFF
    p_latent = _in_proj(h, w_in_all_t, layer, lambda j: off[0], A_LATENT, F32, 1024, tn, name="in_proj_latent")
    p_small = _matmul_nt(h, _small_in_proj_weights(w_in_all_t, layer), F32, 1024, A_SMALL, name="in_proj_small")
    n_first = (off[6] - off[3]) // tn
    pb = _in_proj(h, w_in_all_t, layer,
                  lambda j: jnp.where(j < n_first, off[3] + j * tn, off[7] + (j - n_first) * tn),
                  B_WIDTH, BF16, 1024, tn, name="in_proj_b")
    gsig = _in_proj(h, w_in_all_t, layer, lambda j: off[12] + j * tn, N_BRANCH * D_MODEL, BF16, 1024, tn,
                    act="sigmoid", name="in_proj_gates")

    wuq_p = jnp.pad(w_uq.reshape(MLA_RANK, MLA_HEADS, MLA_QK),
                    ((0, 0), (0, 0), (0, HEAD_PAD - MLA_QK))).reshape(MLA_RANK, MLA_HEADS * HEAD_PAD).astype(BF16)
    w_ukv3 = w_ukv.reshape(MLA_RANK, MLA_HEADS, MLA_NOPE + MLA_V)
    wuk = w_ukv3[:, :, :MLA_NOPE].reshape(MLA_RANK, MLA_HEADS * MLA_NOPE).astype(BF16)
    wuv_t = w_ukv3[:, :, MLA_NOPE:].reshape(MLA_RANK, MLA_HEADS * MLA_V).T.astype(BF16)
    q_a, k_a, vt_a = _mla_prep(p_latent, p_small, wuq_p, wuk, wuv_t, g_cq.reshape(1, -1), g_ckv.reshape(1, -1),
                               _pad_lanes(g_mla_q, HEAD_PAD), g_mla_k[:MLA_NOPE].reshape(1, -1),
                               _pad_lanes(g_mla_k[MLA_NOPE:], LANES), tabs)
    o_a = _attention(q_a, k_a, vt_a)

    wa_p = jnp.zeros((LANES, GLA_HEADS * GLA_DK), F32).at[MISC_GA:MISC_GA + GLA_GATE_RANK].set(w_a2)
    o_b = _gla(pb, p_small, wa_p, b_a.reshape(1, -1), g_gla_o.reshape(1, -1))

    f_cum, f_cum_t = _fox_cumsum(p_small,_pad_lanes(b_f, LANES))
    q_c, k_c = _fox_prep(pb, f_cum, g_fox_q.reshape(1, -1), g_fox_k.reshape(1, -1))
    vt_c = _value_t_proj(h, w_in_all_t, layer, off[10], FOX_HEADS)
    o_c = _attention(q_c, k_c, vt_c, f_cum_t)

    x1, h2 = _merge_out(o_a, o_b, o_c, gsig, w_branch_all, w_out_all, layer, x, g_ffn)
    act = _swiglu_up(h2, w_gu_all, layer, 1024, 512)
    if g_next is None:
        return _matmul_residual(act, w_down_all, layer, x1, 256, "ffn_down"), None
    return _matmul_residual(act, w_down_all, layer, x1, 256, "ffn_down", g_next=g_next)


def kernel(x, positions, g_mix, w_in, g_cq, w_uq, g_ckv, w_ukv, g_mla_q, g_mla_k, w_a2, b_a, g_gla_o,
           g_fox_q, g_fox_k, b_f, w_branch, w_out, g_ffn, w_gu, w_down):
    bsz, s, d = x.shape
    assert bsz == 1 and d == D_MODEL and s % ATT_TILE == 0
    half = MLA_ROPE // 2
    inv = ROPE_THETA ** (-jnp.arange(half, dtype=F32) / half)
    inv128 = jnp.concatenate([inv, inv, jnp.zeros((LANES - MLA_ROPE,), F32)]).reshape(1, LANES)
    tabs = _rope_tables(positions.reshape(s, 1), inv128)
    depth = w_in.shape[0]
    w_in = jnp.swapaxes(w_in, 1, 2)
    w_branch = w_branch.reshape(depth, N_BRANCH * BRANCH_WIDTH, D_MODEL).astype(BF16)
    w_out = w_out.astype(BF16)
    w_down = w_down.astype(BF16)
    xcur = x.reshape(s, d)
    hcur = _rmsnorm(xcur, g_mix[0])
    for l in range(depth):
        g_next = g_mix[l + 1] if l + 1 < depth else None
        xcur, hcur = _layer(xcur, hcur, tabs, g_next, w_in, l, g_cq[l], w_uq[l], g_ckv[l], w_ukv[l], g_mla_q[l],
                            g_mla_k[l], w_a2[l], b_a[l], g_gla_o[l], g_fox_q[l], g_fox_k[l], b_f[l], w_branch,
                            w_out, g_ffn[l], w_gu, w_down)
    return xcur.reshape(bsz, s, d)
```

```python
import functools

import jax
import jax.numpy as jnp
from jax import lax
from jax.experimental import pallas as pl
from jax.experimental.pallas import tpu as pltpu

F32 = jnp.float32
BF16 = jnp.bfloat16

D_MODEL = 2048
BRANCH_WIDTH = D_MODEL // 2
N_BRANCH = 3
MLA_NOPE = 128
MLA_ROPE = 64
MLA_V = 128
MLA_HEADS = BRANCH_WIDTH // MLA_V
MLA_QK = MLA_NOPE + MLA_ROPE
MLA_RANK = 512
ROPE_THETA = 10000.0
GLA_HEADS = 4
GLA_DV = BRANCH_WIDTH // GLA_HEADS
GLA_DK = GLA_DV // 2
GLA_GATE_RANK = 16
GLA_TAU = 16.0
FOX_DH = 128
FOX_HEADS = BRANCH_WIDTH // FOX_DH
FFN_HIDDEN = ((8 * D_MODEL + 3 * 256 - 1) // (3 * 256)) * 256
EPS = 1e-6
NEG_INF = -1e30

LANES = 128
F32_SUBLANES = 8
BF16_SUBLANES = 16
HEAD_PAD = 256
VMEM_LIMIT = 48 * 1024 * 1024
VMEM_LIMIT_BIG = 56 * 1024 * 1024

IN_SPLIT = (MLA_RANK, MLA_RANK, MLA_ROPE, GLA_HEADS * GLA_DK, GLA_HEADS * GLA_DK, GLA_HEADS * GLA_DV, GLA_GATE_RANK,
            GLA_HEADS * GLA_DV, FOX_HEADS * FOX_DH, FOX_HEADS * FOX_DH, FOX_HEADS * FOX_DH, FOX_HEADS,
            N_BRANCH * D_MODEL)
IN_OFF = tuple(sum(IN_SPLIT[:i]) for i in range(len(IN_SPLIT) + 1))

A_CQ, A_CKV = 0, 512
A_LATENT = 1024
A_KR, A_MISC = 0, 128
A_SMALL = 256
MISC_FL = 0
MISC_GA = FOX_HEADS
B_GQ, B_GK, B_GV, B_GR, B_FQ, B_FK = 0, 512, 1024, 2048, 3072, 4096
B_WIDTH = 5120

ATT_TILE = 1024
ATT_SUB = 256
ATT_VROWS = LANES + BF16_SUBLANES
LOG2E = 1.4426950408889634

GLA_CHUNK = 64
GLA_SUB = 8
GLA_BLOCK = 1024


def _aligned(start, m):
    return start if isinstance(start, int) else pl.multiple_of(start, m)


def _cparams(sem):
    return pltpu.CompilerParams(dimension_semantics=sem, vmem_limit_bytes=VMEM_LIMIT)


def _split3(x):
    x1 = x.astype(BF16)
    r1 = x - x1.astype(F32)
    x2 = r1.astype(BF16)
    x3 = (r1 - x2.astype(F32)).astype(BF16)
    return x1, x2, x3


def _sigmoid(x):
    return 0.5 * jnp.tanh(0.5 * x) + 0.5


def _log_sigmoid(z):
    return -(jnp.maximum(-z, 0.0) + jnp.log1p(jnp.exp(-jnp.abs(z))))


def _store_values_t(vt_out, vt):
    rows = vt.shape[1]
    r = lax.broadcasted_iota(jnp.int32, (BF16_SUBLANES, rows), 0)
    ones_rows = jnp.where(r == 0, 1.0, 0.0).astype(BF16)
    for h in range(vt.shape[0] // LANES):
        vt_out[h, 0, 0:LANES, :] = vt[h * LANES:(h + 1) * LANES, :].astype(BF16)
        vt_out[h, 0, LANES:ATT_VROWS, :] = ones_rows


def _rmsnorm_kernel(x_ref, g_ref, o_ref):
    x = x_ref[...]
    ms = jnp.mean(x * x, axis=-1, keepdims=True)
    o_ref[...] = (x * lax.rsqrt(ms + EPS) * g_ref[...]).astype(o_ref.dtype)


def _rmsnorm(x, g, tm=512):
    m, d = x.shape
    return pl.pallas_call(
        _rmsnorm_kernel,
        grid=(m // tm,),
        in_specs=[pl.BlockSpec((tm, d), lambda i: (i, 0)),
                  pl.BlockSpec((1, d), lambda i: (0, 0))],
        out_specs=pl.BlockSpec((tm, d), lambda i: (i, 0)),
        out_shape=jax.ShapeDtypeStruct((m, d), BF16),
        compiler_params=_cparams(("parallel",)),
        name="rmsnorm",
    )(x, g.reshape(1, d))


def _mm_nt_kernel(a_ref, wt_ref, o_ref):
    y = lax.dot_general(a_ref[...], wt_ref[...], (((1,), (1,)), ((), ())), preferred_element_type=F32)
    o_ref[...] = y.astype(o_ref.dtype)


def _matmul_nt(a, w_t, out_dtype, tm, tn, name):
    m, k = a.shape
    n = w_t.shape[0]
    return pl.pallas_call(
        _mm_nt_kernel,
        grid=(m // tm, n // tn),
        in_specs=[pl.BlockSpec((tm, k), lambda i, j: (i, 0)),
                  pl.BlockSpec((tn, k), lambda i, j: (j, 0))],
        out_specs=pl.BlockSpec((tm, tn), lambda i, j: (i, j)),
        out_shape=jax.ShapeDtypeStruct((m, n), out_dtype),
        compiler_params=_cparams(("parallel", "arbitrary")),
        name=name,
    )(a, w_t)


def _mm_res_kernel(*refs, with_norm):
    if with_norm:
        a_ref, w_ref, r_ref, g_ref, o_ref, h_ref = refs
    else:
        a_ref, w_ref, r_ref, o_ref = refs
    y = r_ref[...] + jnp.dot(a_ref[...], w_ref[...], preferred_element_type=F32)
    o_ref[...] = y
    if with_norm:
        ms = jnp.mean(y * y, axis=-1, keepdims=True)
        h_ref[...] = (y * lax.rsqrt(ms + EPS) * g_ref[...]).astype(h_ref.dtype)


def _matmul_residual(a, w_all, layer, res, tm, name, g_next=None):
    m, k = a.shape
    n = w_all.shape[2]
    with_norm = g_next is not None
    row = pl.BlockSpec((tm, n), lambda i: (i, 0))
    in_specs = [pl.BlockSpec((tm, k), lambda i: (i, 0)),
                pl.BlockSpec((None, k, n), lambda i: (layer, 0, 0), pipeline_mode=pl.Buffered(1)),
                row]
    args = [a, w_all, res]
    out_specs, out_shape = row, jax.ShapeDtypeStruct((m, n), F32)
    if with_norm:
        in_specs.append(pl.BlockSpec((1, n), lambda i: (0, 0)))
        args.append(g_next.reshape(1, n))
        out_specs, out_shape = [row, row], [out_shape, jax.ShapeDtypeStruct((m, n), BF16)]
    return pl.pallas_call(
        functools.partial(_mm_res_kernel, with_norm=with_norm),
        grid=(m // tm,),
        in_specs=in_specs,
        out_specs=out_specs,
        out_shape=out_shape,
        compiler_params=pltpu.CompilerParams(dimension_semantics=("parallel",), vmem_limit_bytes=VMEM_LIMIT_BIG),
        name=name,
    )(*args)


def _swiglu_kernel(a_ref, wg_ref, wu_ref, o_ref, wg_sc, wu_sc):
    @pl.when(pl.program_id(1) == 0)
    def _():
        wg_sc[...] = wg_ref[...].astype(BF16)
        wu_sc[...] = wu_ref[...].astype(BF16)

    a = a_ref[...]
    g = jnp.dot(a, wg_sc[...], preferred_element_type=F32)
    u = jnp.dot(a, wu_sc[...], preferred_element_type=F32)
    o_ref[...] = (g * _sigmoid(g) * u).astype(o_ref.dtype)


def _swiglu_up(a, w_gu_all, layer, tm, tn):
    m, k = a.shape
    hidden = w_gu_all.shape[2] // 2
    nj = hidden // tn
    return pl.pallas_call(
        _swiglu_kernel,
        grid=(nj, m // tm),
        in_specs=[pl.BlockSpec((tm, k), lambda j, i: (i, 0)),
                  pl.BlockSpec((None, k, tn), lambda j, i: (layer, 0, j)),
                  pl.BlockSpec((None, k, tn), lambda j, i: (layer, 0, j + nj))],
        out_specs=pl.BlockSpec((tm, tn), lambda j, i: (i, j)),
        out_shape=jax.ShapeDtypeStruct((m, hidden), BF16),
        scratch_shapes=[pltpu.VMEM((k, tn), BF16), pltpu.VMEM((k, tn), BF16)],
        compiler_params=pltpu.CompilerParams(dimension_semantics=("parallel", "arbitrary"),
                                             vmem_limit_bytes=VMEM_LIMIT_BIG),
        name="ffn_up",
    )(a, w_gu_all, w_gu_all)


def _merge_out_kernel(oa_ref, ob_ref, oc_ref, gs_ref, wb_ref, wo_ref, x_ref, g_ref, x1_ref, h_ref):
    bw = oa_ref.shape[1]
    d = x_ref.shape[1]
    merged = None
    for n, o_ref in enumerate((oa_ref, ob_ref, oc_ref)):
        y = jnp.dot(o_ref[...], wb_ref[n * bw:(n + 1) * bw, :], preferred_element_type=F32)
        y = y * gs_ref[:, n * d:(n + 1) * d].astype(F32)
        merged = y if merged is None else merged + y
    x1 = x_ref[...] + jnp.dot(merged.astype(BF16), wo_ref[...], preferred_element_type=F32)
    x1_ref[...] = x1
    ms = jnp.mean(x1 * x1, axis=-1, keepdims=True)
    h_ref[...] = (x1 * lax.rsqrt(ms + EPS) * g_ref[...]).astype(h_ref.dtype)


def _merge_out(o_a, o_b, o_c, gsig, w_branch_all, w_out_all, layer, x, g_next, tm=256):
    m, bw = o_a.shape
    d = x.shape[1]
    o_spec = pl.BlockSpec((tm, bw), lambda i: (i, 0))
    row = pl.BlockSpec((tm, d), lambda i: (i, 0))
    return pl.pallas_call(
        _merge_out_kernel,
        grid=(m // tm,),
        in_specs=[o_spec, o_spec, o_spec,
                  pl.BlockSpec((tm, N_BRANCH * d), lambda i: (i, 0)),
                  pl.BlockSpec((None, N_BRANCH * bw, d), lambda i: (layer, 0, 0), pipeline_mode=pl.Buffered(1)),
                  pl.BlockSpec((None, d, d), lambda i: (layer, 0, 0), pipeline_mode=pl.Buffered(1)),
                  row,
                  pl.BlockSpec((1, d), lambda i: (0, 0))],
        out_specs=[row, row],
        out_shape=[jax.ShapeDtypeStruct((m, d), F32), jax.ShapeDtypeStruct((m, d), BF16)],
        compiler_params=pltpu.CompilerParams(dimension_semantics=("parallel",), vmem_limit_bytes=VMEM_LIMIT_BIG),
        name="merge_out_proj",
    )(o_a, o_b, o_c, gsig, w_branch_all, w_out_all, x, g_next.reshape(1, d))


def _rope_table_kernel(pos_ref, inv_ref, c_ref, s1_ref, s2_ref):
    ang = pos_ref[...].astype(F32) * inv_ref[...]
    c = jnp.cos(ang)
    s = jnp.sin(ang)
    lane = lax.broadcasted_iota(jnp.int32, ang.shape, 1)
    half = MLA_ROPE // 2
    c_ref[...] = jnp.where(lane < MLA_ROPE, c, 0.0)
    s1_ref[...] = jnp.where((lane >= half) & (lane < MLA_ROPE), s, 0.0)
    s2_ref[...] = jnp.where(lane < half, -s, 0.0)


def _rope_tables(pos_col, inv128, tm=512):
    s = pos_col.shape[0]
    spec = pl.BlockSpec((tm, LANES), lambda i: (i, 0))
    shp = jax.ShapeDtypeStruct((s, LANES), F32)
    return pl.pallas_call(
        _rope_table_kernel,
        grid=(s // tm,),
        in_specs=[pl.BlockSpec((tm, 1), lambda i: (i, 0)),
                  pl.BlockSpec((1, LANES), lambda i: (0, 0))],
        out_specs=[spec, spec, spec],
        out_shape=[shp, shp, shp],
        compiler_params=_cparams(("parallel",)),
        name="rope_tables",
    )(pos_col, inv128)


def _rope128(x, c, s1, s2):
    half = MLA_ROPE // 2
    return x * c + pltpu.roll(x, half, 1) * s1 + pltpu.roll(x, LANES - half, 1) * s2


def _mla_prep_kernel(cq_ref, ckv_ref, kr_ref, wuq_ref, wuk_ref, wuvt_ref, gcq_ref, gckv_ref, gq_ref, gk_ref, gkr_ref,
                     c_ref, s1_ref, s2_ref, q_out, k_out, vt_out):
    def norm(x, g):
        return x * lax.rsqrt(jnp.mean(x * x, axis=-1, keepdims=True) + EPS) * g

    cqn = norm(cq_ref[...], gcq_ref[...]).astype(BF16)
    ckvn = norm(ckv_ref[...], gckv_ref[...]).astype(BF16)
    qa = jnp.dot(cqn, wuq_ref[...], preferred_element_type=F32)
    ka = jnp.dot(ckvn, wuk_ref[...], preferred_element_type=F32)
    _store_values_t(vt_out, lax.dot_general(wuvt_ref[...], ckvn, (((1,), (1,)), ((), ())),
                                            preferred_element_type=F32))
    krp = kr_ref[...]
    kr_ss = jnp.sum(krp * krp, axis=-1, keepdims=True)
    c, s1, s2 = c_ref[...], s1_ref[...], s2_ref[...]
    gq, gk, gkr = gq_ref[...], gk_ref[...], gkr_ref[...]
    scale = MLA_QK ** -0.5 * LOG2E
    kr_roped = _rope128(krp * gkr, c, s1, s2)
    for h in range(MLA_HEADS):
        qh = qa[:, h * HEAD_PAD:(h + 1) * HEAD_PAD]
        rq = lax.rsqrt(jnp.sum(qh * qh, axis=-1, keepdims=True) * (1.0 / MLA_QK) + EPS)
        qn = qh * rq * gq
        q_out[h, :, 0:LANES] = (qn[:, 0:LANES] * scale).astype(BF16)
        q_out[h, :, LANES:HEAD_PAD] = (_rope128(qn[:, LANES:HEAD_PAD], c, s1, s2) * scale).astype(BF16)
        kn = ka[:, h * MLA_NOPE:(h + 1) * MLA_NOPE]
        rk = lax.rsqrt((jnp.sum(kn * kn, axis=-1, keepdims=True) + kr_ss) * (1.0 / MLA_QK) + EPS)
        k_out[h, :, 0:LANES] = (kn * rk * gk).astype(BF16)
        k_out[h, :, LANES:HEAD_PAD] = (kr_roped * rk).astype(BF16)


def _vt_spec(heads, tm):
    per_tile = ATT_TILE // tm
    return pl.BlockSpec((heads, 1, ATT_VROWS, tm), lambda i: (0, i // per_tile, 0, i % per_tile))


def _mla_prep(p_latent, p_small, wuq_p, wuk, wuv_t, g_cq, g_ckv, g_q, g_k, g_kr, tabs, tm=256):
    s = p_latent.shape[0]
    h = MLA_HEADS
    row = lambda w: pl.BlockSpec((1, w), lambda i: (0, 0))
    tab = pl.BlockSpec((tm, LANES), lambda i: (i, 0))
    return pl.pallas_call(
        _mla_prep_kernel,
        grid=(s // tm,),
        in_specs=[pl.BlockSpec((tm, MLA_RANK), lambda i: (i, A_CQ // MLA_RANK)),
                  pl.BlockSpec((tm, MLA_RANK), lambda i: (i, A_CKV // MLA_RANK)),
                  pl.BlockSpec((tm, LANES), lambda i: (i, A_KR // LANES)),
                  pl.BlockSpec(wuq_p.shape, lambda i: (0, 0)),
                  pl.BlockSpec(wuk.shape, lambda i: (0, 0)),
                  pl.BlockSpec(wuv_t.shape, lambda i: (0, 0)),
                  row(MLA_RANK), row(MLA_RANK), row(HEAD_PAD), row(LANES), row(LANES),
                  tab, tab, tab],
        out_specs=[pl.BlockSpec((h, tm, HEAD_PAD), lambda i: (0, i, 0)),
                   pl.BlockSpec((h, tm, HEAD_PAD), lambda i: (0, i, 0)),
                   _vt_spec(h, tm)],
        out_shape=[jax.ShapeDtypeStruct((h, s, HEAD_PAD), BF16),
                   jax.ShapeDtypeStruct((h, s, HEAD_PAD), BF16),
                   jax.ShapeDtypeStruct((h, s // ATT_TILE, ATT_VROWS, ATT_TILE), BF16)],
        compiler_params=_cparams(("parallel",)),
        name="mla_prep",
    )(p_latent, p_latent, p_small, wuq_p, wuk, wuv_t, g_cq, g_ckv, g_q, g_k, g_kr, *tabs)


def _fox_cumsum_kernel(misc_ref, bf_ref, f_ref, ft_ref, carry_ref):
    blk = misc_ref.shape[0]

    @pl.when(pl.program_id(0) == 0)
    def _():
        carry_ref[...] = jnp.zeros(carry_ref.shape, F32)

    r = lax.broadcasted_iota(jnp.int32, (blk, blk), 0)
    c = lax.broadcasted_iota(jnp.int32, (blk, blk), 1)
    tri = jnp.where(c <= r, 1.0, 0.0).astype(BF16)
    x1, x2, x3 = _split3(_log_sigmoid(misc_ref[...] + bf_ref[...]))
    cs = (jnp.dot(tri, x1, preferred_element_type=F32)
          + jnp.dot(tri, x2, preferred_element_type=F32)
          + jnp.dot(tri, x3, preferred_element_type=F32)) + carry_ref[...]
    f_ref[...] = cs
    ft_ref[...] = cs.T
    carry_ref[...] = cs[blk - 1:blk, :]


def _fox_cumsum(p_small, bf128, blk=512):
    s = p_small.shape[0]
    return pl.pallas_call(
        _fox_cumsum_kernel,
        grid=(s // blk,),
        in_specs=[pl.BlockSpec((blk, LANES), lambda i: (i, A_MISC // LANES)),
                  pl.BlockSpec((1, LANES), lambda i: (0, 0))],
        out_specs=[pl.BlockSpec((blk, LANES), lambda i: (i, 0)),
                   pl.BlockSpec((LANES, blk), lambda i: (0, i))],
        out_shape=[jax.ShapeDtypeStruct((s, LANES), F32), jax.ShapeDtypeStruct((LANES, s), F32)],
        scratch_shapes=[pltpu.VMEM((1, LANES), F32)],
        compiler_params=_cparams(("arbitrary",)),
        name="fox_cumsum",
    )(p_small, bf128)


def _fox_prep_kernel(fq_ref, fk_ref, f_ref, gq_ref, gk_ref, sel_ref, q_out, k_out):
    tm = fq_ref.shape[0]
    lane = lax.broadcasted_iota(jnp.int32, (tm, LANES), 1)
    ones_tail = jnp.where(lane < 3, 1.0, 0.0).astype(BF16)
    pieces = jnp.concatenate(_split3(f_ref[...] * (-LOG2E)), axis=1)
    scale = FOX_DH ** -0.5 * LOG2E
    gq, gk = gq_ref[...], gk_ref[...]
    for h in range(FOX_HEADS):
        sl = slice(h * FOX_DH, (h + 1) * FOX_DH)
        q = fq_ref[:, sl].astype(F32)
        k = fk_ref[:, sl].astype(F32)
        qn = q * lax.rsqrt(jnp.mean(q * q, axis=-1, keepdims=True) + EPS) * gq
        kn = k * lax.rsqrt(jnp.mean(k * k, axis=-1, keepdims=True) + EPS) * gk
        q_out[h, :, 0:LANES] = (qn * scale).astype(BF16)
        q_out[h, :, LANES:HEAD_PAD] = ones_tail
        k_out[h, :, 0:LANES] = kn.astype(BF16)
        k_out[h, :, LANES:HEAD_PAD] = jnp.dot(pieces, sel_ref[h], preferred_element_type=F32).astype(BF16)


def _fox_prep(pb, f_cum, g_q, g_k, tm=256):
    s = pb.shape[0]
    h = FOX_HEADS
    w = FOX_HEADS * FOX_DH
    row = pl.BlockSpec((1, FOX_DH), lambda i: (0, 0))
    hh, pp = jnp.meshgrid(jnp.arange(h), jnp.arange(3), indexing="ij")
    sel = jnp.zeros((h, 3 * LANES, LANES), BF16).at[hh, pp * LANES + hh, pp].set(1.0)
    return pl.pallas_call(
        _fox_prep_kernel,
        grid=(s // tm,),
        in_specs=[pl.BlockSpec((tm, w), lambda i: (i, B_FQ // w)),
                  pl.BlockSpec((tm, w), lambda i: (i, B_FK // w)),
                  pl.BlockSpec((tm, LANES), lambda i: (i, 0)),
                  row, row,
                  pl.BlockSpec(sel.shape, lambda i: (0, 0, 0))],
        out_specs=[pl.BlockSpec((h, tm, HEAD_PAD), lambda i: (0, i, 0)),
                   pl.BlockSpec((h, tm, HEAD_PAD), lambda i: (0, i, 0))],
        out_shape=[jax.ShapeDtypeStruct((h, s, HEAD_PAD), BF16),
                   jax.ShapeDtypeStruct((h, s, HEAD_PAD), BF16)],
        compiler_params=_cparams(("parallel",)),
        name="fox_prep",
    )(pb, pb, f_cum, g_q, g_k, sel)


def _value_t_kernel(h_ref, w_ref, vt_out, w_sc):
    @pl.when(pl.program_id(0) == 0)
    def _():
        w_sc[...] = w_ref[0].astype(BF16)

    vt = lax.dot_general(w_sc[...], h_ref[...], (((1,), (1,)), ((), ())), preferred_element_type=F32)
    _store_values_t(vt_out, vt)


def _value_t_proj(h, w_all_t, layer, row_start, heads, tm=512):
    s, d = h.shape
    nv = heads * LANES
    return pl.pallas_call(
        _value_t_kernel,
        grid=(s // tm,),
        in_specs=[pl.BlockSpec((tm, d), lambda i: (i, 0)),
                  pl.BlockSpec((pl.Element(1), pl.Element(nv), pl.Element(d)), lambda i: (layer, row_start, 0))],
        out_specs=_vt_spec(heads, tm),
        out_shape=jax.ShapeDtypeStruct((heads, s // ATT_TILE, ATT_VROWS, ATT_TILE), BF16),
        scratch_shapes=[pltpu.VMEM((nv, d), BF16)],
        compiler_params=_cparams(("arbitrary",)),
        name="fox_value_t",
    )(h, w_all_t)


def _attn_kernel(*refs, tq, use_f):
    if use_f:
        q_ref, k_ref, vt_ref, f_ref, o_ref, m_sc, acc_sc, s_a, s_b = refs
    else:
        q_ref, k_ref, vt_ref, o_ref, m_sc, acc_sc, s_a, s_b = refs
    ns = tq // ATT_SUB
    nq = q_ref.shape[1] // tq

    def reset():
        m_sc[...] = jnp.full(m_sc.shape, NEG_INF, F32)
        acc_sc[...] = jnp.zeros(acc_sc.shape, F32)

    def logits(dst, qi, kb):
        dst[...] = lax.dot_general(k_ref[0, pl.ds(_aligned(kb * tq, tq), tq), :],
                                   q_ref[0, pl.ds(_aligned(qi * tq, tq), tq), :],
                                   (((1,), (1,)), ((), ())), preferred_element_type=F32)

    def update(src, j, qi, kb, k0, size, causal):
        s = src[k0:k0 + size, j * ATT_SUB:(j + 1) * ATT_SUB]
        if causal:
            key = lax.broadcasted_iota(jnp.int32, (size, ATT_SUB), 0)
            qry = lax.broadcasted_iota(jnp.int32, (size, ATT_SUB), 1)
            s = jnp.where(key <= qry, s, NEG_INF)
        m_old = m_sc[j]
        m_cur = jnp.max(s, axis=0, keepdims=True)
        if use_f:
            ft = f_ref[pl.ds(qi, 1), j * ATT_SUB:(j + 1) * ATT_SUB] * LOG2E
            m_new = jnp.maximum(m_old, m_cur + ft)
            shift = m_new - ft
        else:
            m_new = jnp.maximum(m_old, m_cur)
            shift = m_new
        p = jnp.exp2(s - shift).astype(BF16)
        alpha = jnp.exp2(m_old - m_new)
        acc_sc[j] = alpha * acc_sc[j] + jnp.dot(vt_ref[0, kb, :, k0:k0 + size], p, preferred_element_type=F32)
        m_sc[j] = m_new

    def full_tile(src, qi, kb):
        for j in range(ns):
            update(src, j, qi, kb, 0, tq, False)

    def diagonal_tile(src, qi):
        for j in range(ns):
            if j > 0:
                update(src, j, qi, qi, 0, j * ATT_SUB, False)
            update(src, j, qi, qi, j * ATT_SUB, ATT_SUB, True)
        for j in range(ns):
            acc = acc_sc[j]
            o_t = acc[0:LANES, :] / acc[LANES:LANES + 1, :]
            o_ref[pl.ds(_aligned(qi * tq + j * ATT_SUB, ATT_SUB), ATT_SUB), :] = o_t.T.astype(o_ref.dtype)
        reset()

    def query_tile(qi, cur, oth, odd):
        def pair(u, carry):
            t = 2 * u
            logits(oth, qi, t + 1)
            full_tile(cur, qi, t)
            logits(cur, qi, t + 2)
            full_tile(oth, qi, t + 1)
            return carry

        lax.fori_loop(0, qi // 2, pair, 0)
        q_next = jnp.minimum(qi + 1, nq - 1)
        if odd:
            logits(oth, qi, qi)
            full_tile(cur, qi, qi - 1)
            logits(cur, q_next, 0)
            diagonal_tile(oth, qi)
        else:
            logits(oth, q_next, 0)
            diagonal_tile(cur, qi)

    assert nq % 4 == 0
    reset()
    logits(s_a, 0, 0)

    def four_query_tiles(g, carry):
        q0 = 4 * g
        query_tile(q0, s_a, s_b, False)
        query_tile(q0 + 1, s_b, s_a, True)
        query_tile(q0 + 2, s_b, s_a, False)
        query_tile(q0 + 3, s_a, s_b, True)
        return carry

    lax.fori_loop(0, nq // 4, four_query_tiles, 0)


def _attention(q, k, vt, f_t=None):
    hh, s, dk = q.shape
    _, nt, vrows, tq = vt.shape
    use_f = f_t is not None
    in_specs = [pl.BlockSpec((1, s, dk), lambda h: (h, 0, 0)),
                pl.BlockSpec((1, s, dk), lambda h: (h, 0, 0)),
                pl.BlockSpec((1, nt, vrows, tq), lambda h: (h, 0, 0, 0))]
    args = [q, k, vt]
    if use_f:
        in_specs.append(pl.BlockSpec((None, nt, tq), lambda h: (h, 0, 0)))
        args.append(f_t.reshape(f_t.shape[0], nt, tq))
    ns = tq // ATT_SUB
    return pl.pallas_call(
        functools.partial(_attn_kernel, tq=tq, use_f=use_f),
        grid=(hh,),
        in_specs=in_specs,
        out_specs=pl.BlockSpec((s, LANES), lambda h: (0, h)),
        out_shape=jax.ShapeDtypeStruct((s, hh * LANES), BF16),
        scratch_shapes=[pltpu.VMEM((ns, 1, ATT_SUB), F32), pltpu.VMEM((ns, vrows, ATT_SUB), F32),
                        pltpu.VMEM((tq, tq), F32), pltpu.VMEM((tq, tq), F32)],
        compiler_params=_cparams(("parallel",)),
        name="fox_attention" if use_f else "mla_attention",
    )(*args)


def _gla_kernel(q_ref, k_ref, v_ref, r_ref, misc_ref, wa_ref, ba_ref, go_ref, o_ref,
                st_ref, la_ref, b_ref, qe_ref, a_ref, u_ref, dl_ref, stb_ref):
    tb = q_ref.shape[0]
    nchunk = tb // GLA_CHUNK
    nsub = GLA_CHUNK // GLA_SUB
    C = GLA_CHUNK

    @pl.when(pl.program_id(1) == 0)
    def _():
        st_ref[...] = jnp.zeros(st_ref.shape, F32)

    m1, m2, _ = _split3(misc_ref[...])
    w1, w2, _ = _split3(wa_ref[...])
    z = (jnp.dot(m1, w1, preferred_element_type=F32) + jnp.dot(m1, w2, preferred_element_type=F32)
         + jnp.dot(m2, w1, preferred_element_type=F32)) + ba_ref[...]
    la_ref[...] = _log_sigmoid(z) * (LOG2E / GLA_TAU)

    r_i = lax.broadcasted_iota(jnp.int32, (C, C), 0)
    c_i = lax.broadcasted_iota(jnp.int32, (C, C), 1)
    tri = jnp.where(c_i <= r_i, 1.0, 0.0).astype(BF16)
    sub_row = lax.broadcasted_iota(jnp.int32, (GLA_SUB, GLA_DK), 0)
    col_sub = lax.broadcasted_iota(jnp.int32, (GLA_SUB, C), 1)
    scale = GLA_DK ** -0.5
    g_o = go_ref[...]

    chunk_rows = [slice(ci * C, (ci + 1) * C) for ci in range(nchunk)]

    for rows in chunk_rows:
        l1, l2, l3 = _split3(la_ref[rows, :])
        b_ref[rows, :] = (jnp.dot(tri, l1, preferred_element_type=F32) + jnp.dot(tri, l2, preferred_element_type=F32)
                          + jnp.dot(tri, l3, preferred_element_type=F32))

    for ci, rows in enumerate(chunk_rows):
        b = b_ref[rows, :]
        qc = q_ref[rows, :].astype(F32) * scale
        kc = k_ref[rows, :].astype(F32)
        b_last = b[C - 1:C, :]
        qe_ref[rows, :] = (qc * jnp.exp2(b)).astype(BF16)
        ke = (kc * jnp.exp2(b_last - b)).astype(BF16)
        u_ref[ci] = lax.dot_general(v_ref[rows, :], ke, (((0,), (0,)), ((), ())), preferred_element_type=F32)
        dl_ref[ci] = jnp.exp2(b_last)

        a_rows = []
        for si in range(nsub):
            lo = si * GLA_SUB
            qs, ks, bs = qc[lo:lo + GLA_SUB], kc[lo:lo + GLA_SUB], b[lo:lo + GLA_SUB]
            a = jnp.zeros((GLA_SUB, C), F32)
            for j in range(GLA_SUB):
                e = jnp.exp2(jnp.where(sub_row >= j, bs - bs[j:j + 1, :], NEG_INF))
                w = jnp.sum(qs * (ks[j:j + 1, :] * e), axis=-1, keepdims=True)
                a = jnp.where(col_sub == lo + j, w, a)
            if si > 0:
                ref_row = b[lo - 1:lo, :]
                q_i = qs * jnp.exp2(bs - ref_row)
                k_j = jnp.concatenate([kc[0:lo] * jnp.exp2(ref_row - b[0:lo]),
                                       jnp.zeros((C - lo, GLA_DK), F32)], axis=0)
                a = a + lax.dot_general(q_i.astype(BF16), k_j.astype(BF16), (((1,), (1,)), ((), ())),
                                        preferred_element_type=F32)
            a_rows.append(a)
        a_ref[rows, :] = jnp.concatenate(a_rows, axis=0).astype(BF16)

    st = st_ref[...]
    for ci in range(nchunk):
        stb_ref[ci] = st.astype(BF16)
        st = st * dl_ref[ci] + u_ref[ci]
    st_ref[...] = st

    for ci, rows in enumerate(chunk_rows):
        o = lax.dot_general(qe_ref[rows, :], stb_ref[ci], (((1,), (1,)), ((), ())), preferred_element_type=F32)
        o = o + jnp.dot(a_ref[rows, :], v_ref[rows, :], preferred_element_type=F32)
        on = o * lax.rsqrt(jnp.mean(o * o, axis=-1, keepdims=True) + EPS) * g_o
        r = r_ref[rows, :].astype(F32)
        o_ref[rows, :] = (on * (r * _sigmoid(r))).astype(o_ref.dtype)


def _gla(pb, p_small, wa_p, b_a, g_o, tb=GLA_BLOCK):
    s = pb.shape[0]
    nchunk = tb // GLA_CHUNK
    return pl.pallas_call(
        _gla_kernel,
        grid=(GLA_HEADS, s // tb),
        in_specs=[pl.BlockSpec((tb, GLA_DK), lambda h, i: (i, B_GQ // GLA_DK + h)),
                  pl.BlockSpec((tb, GLA_DK), lambda h, i: (i, B_GK // GLA_DK + h)),
                  pl.BlockSpec((tb, GLA_DV), lambda h, i: (i, B_GV // GLA_DV + h)),
                  pl.BlockSpec((tb, GLA_DV), lambda h, i: (i, B_GR // GLA_DV + h)),
                  pl.BlockSpec((tb, LANES), lambda h, i: (i, A_MISC // LANES)),
                  pl.BlockSpec((LANES, GLA_DK), lambda h, i: (0, h)),
                  pl.BlockSpec((1, GLA_DK), lambda h, i: (0, h)),
                  pl.BlockSpec((1, GLA_DV), lambda h, i: (0, 0))],
        out_specs=pl.BlockSpec((tb, GLA_DV), lambda h, i: (i, h)),
        out_shape=jax.ShapeDtypeStruct((s, GLA_HEADS * GLA_DV), BF16),
        scratch_shapes=[pltpu.VMEM((GLA_DV, GLA_DK), F32),
                        pltpu.VMEM((tb, GLA_DK), F32),
                        pltpu.VMEM((tb, GLA_DK), F32),
                        pltpu.VMEM((tb, GLA_DK), BF16),
                        pltpu.VMEM((tb, GLA_CHUNK), BF16),
                        pltpu.VMEM((nchunk, GLA_DV, GLA_DK), F32),
                        pltpu.VMEM((nchunk, 1, GLA_DK), F32),
                        pltpu.VMEM((nchunk, GLA_DV, GLA_DK), BF16)],
        compiler_params=_cparams(("parallel", "arbitrary")),
        name="gla",
    )(pb, pb, pb, pb, p_small, wa_p, b_a, g_o)


def _in_proj_kernel(a_ref, wt_ref, o_ref, w_sc, *, act):
    @pl.when(pl.program_id(1) == 0)
    def _():
        w_sc[...] = wt_ref[0].astype(BF16)

    y = lax.dot_general(a_ref[...], w_sc[...], (((1,), (1,)), ((), ())), preferred_element_type=F32)
    if act == "sigmoid":
        y = _sigmoid(y)
    o_ref[...] = y.astype(o_ref.dtype)


def _in_proj(a, w_all_t, layer, row_start, n, out_dtype, tm, tn, act=None, name="in_proj"):
    m, k = a.shape
    return pl.pallas_call(
        functools.partial(_in_proj_kernel, act=act),
        grid=(n // tn, m // tm),
        in_specs=[pl.BlockSpec((tm, k), lambda j, i: (i, 0)),
                  pl.BlockSpec((pl.Element(1), pl.Element(tn), pl.Element(k)),
                               lambda j, i: (layer, _aligned(row_start(j), F32_SUBLANES), 0))],
        out_specs=pl.BlockSpec((tm, tn), lambda j, i: (i, j)),
        out_shape=jax.ShapeDtypeStruct((m, n), out_dtype),
        scratch_shapes=[pltpu.VMEM((tn, k), BF16)],
        compiler_params=pltpu.CompilerParams(dimension_semantics=("parallel", "arbitrary"),
                                             vmem_limit_bytes=VMEM_LIMIT_BIG),
        name=name,
    )(a, w_all_t)


def _small_w_kernel(kr_ref, fl_ref, ga_ref, o_ref):
    d = o_ref.shape[1]
    z = lambda n: jnp.zeros((n, d), F32)
    o_ref[...] = jnp.concatenate([kr_ref[0], z(LANES - MLA_ROPE), fl_ref[0], ga_ref[0],
                                  z(LANES - FOX_HEADS - GLA_GATE_RANK)], axis=0).astype(BF16)


def _small_in_proj_weights(w_all_t, layer):
    d = w_all_t.shape[2]

    def rows(i):
        return pl.BlockSpec((pl.Element(1), pl.Element(IN_SPLIT[i]), pl.Element(d)),
                            lambda g: (layer, IN_OFF[i], 0))

    return pl.pallas_call(
        _small_w_kernel,
        grid=(1,),
        in_specs=[rows(2), rows(11), rows(6)],
        out_specs=pl.BlockSpec((A_SMALL, d), lambda g: (0, 0)),
        out_shape=jax.ShapeDtypeStruct((A_SMALL, d), BF16),
        compiler_params=_cparams(("arbitrary",)),
        name="small_in_proj_weights",
    )(w_all_t, w_all_t, w_all_t)


def _pad_lanes(v, n):
    return jnp.pad(v, (0, n - v.shape[0])).reshape(1, n)


def _layer(x, h, tabs, g_next, w_in_all_t, layer, g_cq, w_uq, g_ckv, w_ukv, g_mla_q, g_mla_k, w_a2, b_a, g_gla_o,
           g_fox_q, g_fox_k, b_f, w_branch_all, w_out_all, g_ffn, w_gu_all, w_down_all):
    tn = 1024
    off = IN_OFF
    p_latent = _in_proj(h, w_in_all_t, layer, lambda j: off[0], A_LATENT, F32, 1024, tn, name="in_proj_latent")
    p_small = _matmul_nt(h, _small_in_proj_weights(w_in_all_t, layer), F32, 1024, A_SMALL, name="in_proj_small")
    n_first = (off[6] - off[3]) // tn
    pb = _in_proj(h, w_in_all_t, layer,
                  lambda j: jnp.where(j < n_first, off[3] + j * tn, off[7] + (j - n_first) * tn),
                  B_WIDTH, BF16, 2048, tn, name="in_proj_b")
    gsig = _in_proj(h, w_in_all_t, layer, lambda j: off[12] + j * tn, N_BRANCH * D_MODEL, BF16, 2048, tn,
                    act="sigmoid", name="in_proj_gates")

    wuq_p = jnp.pad(w_uq.reshape(MLA_RANK, MLA_HEADS, MLA_QK),
                    ((0, 0), (0, 0), (0, HEAD_PAD - MLA_QK))).reshape(MLA_RANK, MLA_HEADS * HEAD_PAD).astype(BF16)
    w_ukv3 = w_ukv.reshape(MLA_RANK, MLA_HEADS, MLA_NOPE + MLA_V)
    wuk = w_ukv3[:, :, :MLA_NOPE].reshape(MLA_RANK, MLA_HEADS * MLA_NOPE).astype(BF16)
    wuv_t = w_ukv3[:, :, MLA_NOPE:].reshape(MLA_RANK, MLA_HEADS * MLA_V).T.astype(BF16)
    q_a, k_a, vt_a = _mla_prep(p_latent, p_small, wuq_p, wuk, wuv_t, g_cq.reshape(1, -1), g_ckv.reshape(1, -1),
                               _pad_lanes(g_mla_q, HEAD_PAD), g_mla_k[:MLA_NOPE].reshape(1, -1),
                               _pad_lanes(g_mla_k[MLA_NOPE:], LANES), tabs)
    o_a = _attention(q_a, k_a, vt_a)

    wa_p = jnp.zeros((LANES, GLA_HEADS * GLA_DK), F32).at[MISC_GA:MISC_GA + GLA_GATE_RANK].set(w_a2)
    o_b = _gla(pb, p_small, wa_p, b_a.reshape(1, -1), g_gla_o.reshape(1, -1))

    f_cum, f_cum_t = _fox_cumsum(p_small,_pad_lanes(b_f, LANES))
    q_c, k_c = _fox_prep(pb, f_cum, g_fox_q.reshape(1, -1), g_fox_k.reshape(1, -1))
    vt_c = _value_t_proj(h, w_in_all_t, layer, off[10], FOX_HEADS)
    o_c = _attention(q_c, k_c, vt_c, f_cum_t)

    x1, h2 = _merge_out(o_a, o_b, o_c, gsig, w_branch_all, w_out_all, layer, x, g_ffn)
    act = _swiglu_up(h2, w_gu_all, layer, 2048, 512)
    if g_next is None:
        return _matmul_residual(act, w_down_all, layer, x1, 256, "ffn_down"), None
    return _matmul_residual(act, w_down_all, layer, x1, 256, "ffn_down", g_next=g_next)


def kernel(x, positions, g_mix, w_in, g_cq, w_uq, g_ckv, w_ukv, g_mla_q, g_mla_k, w_a2, b_a, g_gla_o,
           g_fox_q, g_fox_k, b_f, w_branch, w_out, g_ffn, w_gu, w_down):
    bsz, s, d = x.shape
    assert bsz == 1 and d == D_MODEL and s % ATT_TILE == 0
    half = MLA_ROPE // 2
    inv = ROPE_THETA ** (-jnp.arange(half, dtype=F32) / half)
    inv128 = jnp.concatenate([inv, inv, jnp.zeros((LANES - MLA_ROPE,), F32)]).reshape(1, LANES)
    tabs = _rope_tables(positions.reshape(s, 1), inv128)
    depth = w_in.shape[0]
    w_in = jnp.swapaxes(w_in, 1, 2)
    w_branch = w_branch.reshape(depth, N_BRANCH * BRANCH_WIDTH, D_MODEL).astype(BF16)
    w_out = w_out.astype(BF16)
    w_down = w_down.astype(BF16)
    xcur = x.reshape(s, d)
    hcur = _rmsnorm(xcur, g_mix[0])
    for l in range(depth):
        g_next = g_mix[l + 1] if l + 1 < depth else None
        xcur, hcur = _layer(xcur, hcur, tabs, g_next, w_in, l, g_cq[l], w_uq[l], g_ckv[l], w_ukv[l], g_mla_q[l],
                            g_mla_k[l], w_a2[l], b_a[l], g_gla_o[l], g_fox_q[l], g_fox_k[l], b_f[l], w_branch,
                            w_out, g_ffn[l], w_gu, w_down)
    return xcur.reshape(bsz, s, d)
```

```python
import functools

import jax
import jax.numpy as jnp
from jax import lax
from jax.experimental import pallas as pl
from jax.experimental.pallas import tpu as pltpu

F32 = jnp.float32
BF16 = jnp.bfloat16

D_MODEL = 2048
BRANCH_WIDTH = D_MODEL // 2
N_BRANCH = 3
MLA_NOPE = 128
MLA_ROPE = 64
MLA_V = 128
MLA_HEADS = BRANCH_WIDTH // MLA_V
MLA_QK = MLA_NOPE + MLA_ROPE
MLA_RANK = 512
ROPE_THETA = 10000.0
GLA_HEADS = 4
GLA_DV = BRANCH_WIDTH // GLA_HEADS
GLA_DK = GLA_DV // 2
GLA_GATE_RANK = 16
GLA_TAU = 16.0
FOX_DH = 128
FOX_HEADS = BRANCH_WIDTH // FOX_DH
EPS = 1e-6
NEG_INF = -1e30

LANES = 128
F32_SUBLANES = 8
BF16_SUBLANES = 16
HEAD_PAD = 256
VMEM_LIMIT = 48 * 1024 * 1024
VMEM_LIMIT_BIG = 56 * 1024 * 1024

IN_SPLIT = (MLA_RANK, MLA_RANK, MLA_ROPE, GLA_HEADS * GLA_DK, GLA_HEADS * GLA_DK, GLA_HEADS * GLA_DV, GLA_GATE_RANK,
            GLA_HEADS * GLA_DV, FOX_HEADS * FOX_DH, FOX_HEADS * FOX_DH, FOX_HEADS * FOX_DH, FOX_HEADS,
            N_BRANCH * D_MODEL)
IN_OFF = tuple(sum(IN_SPLIT[:i]) for i in range(len(IN_SPLIT) + 1))

A_CQ, A_CKV = 0, 512
A_LATENT = 1024
A_KR, A_MISC = 0, 128
A_SMALL = 256
MISC_GA = FOX_HEADS
B_GQ, B_GK, B_GV, B_GR, B_FQ, B_FK = 0, 512, 1024, 2048, 3072, 4096
B_WIDTH = 5120

ATT_TILE = 1024
ATT_SUB = 256
ATT_VROWS = LANES + BF16_SUBLANES
LOG2E = 1.4426950408889634

GLA_CHUNK = 64
GLA_SUB = 8
GLA_BLOCK = 1024


def _aligned(start, m):
    return start if isinstance(start, int) else pl.multiple_of(start, m)


def _cparams(sem):
    return pltpu.CompilerParams(dimension_semantics=sem, vmem_limit_bytes=VMEM_LIMIT)


def _split3(x):
    x1 = x.astype(BF16)
    r1 = x - x1.astype(F32)
    x2 = r1.astype(BF16)
    x3 = (r1 - x2.astype(F32)).astype(BF16)
    return x1, x2, x3


def _sigmoid(x):
    return 0.5 * jnp.tanh(0.5 * x) + 0.5


def _log_sigmoid(z):
    return -(jnp.maximum(-z, 0.0) + jnp.log1p(jnp.exp(-jnp.abs(z))))


def _store_values_t(vt_out, vt):
    rows = vt.shape[1]
    r = lax.broadcasted_iota(jnp.int32, (BF16_SUBLANES, rows), 0)
    ones_rows = jnp.where(r == 0, 1.0, 0.0).astype(BF16)
    for h in range(vt.shape[0] // LANES):
        vt_out[h, 0, 0:LANES, :] = vt[h * LANES:(h + 1) * LANES, :].astype(BF16)
        vt_out[h, 0, LANES:ATT_VROWS, :] = ones_rows


def _rmsnorm_kernel(x_ref, g_ref, o_ref):
    x = x_ref[...]
    ms = jnp.mean(x * x, axis=-1, keepdims=True)
    o_ref[...] = (x * lax.rsqrt(ms + EPS) * g_ref[...]).astype(o_ref.dtype)


def _rmsnorm(x, g, tm=512):
    m, d = x.shape
    return pl.pallas_call(
        _rmsnorm_kernel,
        grid=(m // tm,),
        in_specs=[pl.BlockSpec((tm, d), lambda i: (i, 0)),
                  pl.BlockSpec((1, d), lambda i: (0, 0))],
        out_specs=pl.BlockSpec((tm, d), lambda i: (i, 0)),
        out_shape=jax.ShapeDtypeStruct((m, d), BF16),
        compiler_params=_cparams(("parallel",)),
        name="rmsnorm",
    )(x, g.reshape(1, d))


def _mm_nt_kernel(a_ref, wt_ref, o_ref):
    y = lax.dot_general(a_ref[...], wt_ref[...], (((1,), (1,)), ((), ())), preferred_element_type=F32)
    o_ref[...] = y.astype(o_ref.dtype)


def _matmul_nt(a, w_t, out_dtype, tm, tn, name):
    m, k = a.shape
    n = w_t.shape[0]
    return pl.pallas_call(
        _mm_nt_kernel,
        grid=(m // tm, n // tn),
        in_specs=[pl.BlockSpec((tm, k), lambda i, j: (i, 0)),
                  pl.BlockSpec((tn, k), lambda i, j: (j, 0))],
        out_specs=pl.BlockSpec((tm, tn), lambda i, j: (i, j)),
        out_shape=jax.ShapeDtypeStruct((m, n), out_dtype),
        compiler_params=_cparams(("parallel", "arbitrary")),
        name=name,
    )(a, w_t)


def _mm_res_kernel(*refs, with_norm):
    if with_norm:
        a_ref, w_ref, r_ref, g_ref, o_ref, h_ref = refs
    else:
        a_ref, w_ref, r_ref, o_ref = refs
    y = r_ref[...] + jnp.dot(a_ref[...], w_ref[...], preferred_element_type=F32)
    o_ref[...] = y
    if with_norm:
        ms = jnp.mean(y * y, axis=-1, keepdims=True)
        h_ref[...] = (y * lax.rsqrt(ms + EPS) * g_ref[...]).astype(h_ref.dtype)


def _matmul_residual(a, w_all, layer, res, tm, name, g_next=None):
    m, k = a.shape
    n = w_all.shape[2]
    with_norm = g_next is not None
    row = pl.BlockSpec((tm, n), lambda i: (i, 0))
    in_specs = [pl.BlockSpec((tm, k), lambda i: (i, 0)),
                pl.BlockSpec((None, k, n), lambda i: (layer, 0, 0), pipeline_mode=pl.Buffered(1)),
                row]
    args = [a, w_all, res]
    out_specs, out_shape = row, jax.ShapeDtypeStruct((m, n), F32)
    if with_norm:
        in_specs.append(pl.BlockSpec((1, n), lambda i: (0, 0)))
        args.append(g_next.reshape(1, n))
        out_specs, out_shape = [row, row], [out_shape, jax.ShapeDtypeStruct((m, n), BF16)]
    return pl.pallas_call(
        functools.partial(_mm_res_kernel, with_norm=with_norm),
        grid=(m // tm,),
        in_specs=in_specs,
        out_specs=out_specs,
        out_shape=out_shape,
        compiler_params=pltpu.CompilerParams(dimension_semantics=("parallel",), vmem_limit_bytes=VMEM_LIMIT_BIG),
        name=name,
    )(*args)


def _swiglu_kernel(a_ref, wg_ref, wu_ref, o_ref, wg_sc, wu_sc):
    @pl.when(pl.program_id(1) == 0)
    def _():
        wg_sc[...] = wg_ref[...].astype(BF16)
        wu_sc[...] = wu_ref[...].astype(BF16)

    a = a_ref[...]
    g = jnp.dot(a, wg_sc[...], preferred_element_type=F32)
    u = jnp.dot(a, wu_sc[...], preferred_element_type=F32)
    o_ref[...] = (g * _sigmoid(g) * u).astype(o_ref.dtype)


def _swiglu_up(a, w_gu_all, layer, tm, tn):
    m, k = a.shape
    hidden = w_gu_all.shape[2] // 2
    nj = hidden // tn
    return pl.pallas_call(
        _swiglu_kernel,
        grid=(nj, m // tm),
        in_specs=[pl.BlockSpec((tm, k), lambda j, i: (i, 0)),
                  pl.BlockSpec((None, k, tn), lambda j, i: (layer, 0, j)),
                  pl.BlockSpec((None, k, tn), lambda j, i: (layer, 0, j + nj))],
        out_specs=pl.BlockSpec((tm, tn), lambda j, i: (i, j)),
        out_shape=jax.ShapeDtypeStruct((m, hidden), BF16),
        scratch_shapes=[pltpu.VMEM((k, tn), BF16), pltpu.VMEM((k, tn), BF16)],
        compiler_params=_cparams(("parallel", "arbitrary")),
        name="ffn_up",
    )(a, w_gu_all, w_gu_all)


def _merge_out_kernel(oa_ref, ob_ref, oc_ref, gs_ref, wb_ref, wo_ref, x_ref, g_ref, x1_ref, h_ref):
    bw = oa_ref.shape[1]
    d = x_ref.shape[1]
    merged = None
    for n, o_ref in enumerate((oa_ref, ob_ref, oc_ref)):
        y = jnp.dot(o_ref[...], wb_ref[n * bw:(n + 1) * bw, :], preferred_element_type=F32)
        y = y * gs_ref[:, n * d:(n + 1) * d].astype(F32)
        merged = y if merged is None else merged + y
    x1 = x_ref[...] + jnp.dot(merged.astype(BF16), wo_ref[...], preferred_element_type=F32)
    x1_ref[...] = x1
    ms = jnp.mean(x1 * x1, axis=-1, keepdims=True)
    h_ref[...] = (x1 * lax.rsqrt(ms + EPS) * g_ref[...]).astype(h_ref.dtype)


def _merge_out(o_a, o_b, o_c, gsig, w_branch_all, w_out_all, layer, x, g_next, tm=256):
    m, bw = o_a.shape
    d = x.shape[1]
    o_spec = pl.BlockSpec((tm, bw), lambda i: (i, 0))
    row = pl.BlockSpec((tm, d), lambda i: (i, 0))
    return pl.pallas_call(
        _merge_out_kernel,
        grid=(m // tm,),
        in_specs=[o_spec, o_spec, o_spec,
                  pl.BlockSpec((tm, N_BRANCH * d), lambda i: (i, 0)),
                  pl.BlockSpec((None, N_BRANCH * bw, d), lambda i: (layer, 0, 0), pipeline_mode=pl.Buffered(1)),
                  pl.BlockSpec((None, d, d), lambda i: (layer, 0, 0), pipeline_mode=pl.Buffered(1)),
                  row,
                  pl.BlockSpec((1, d), lambda i: (0, 0))],
        out_specs=[row, row],
        out_shape=[jax.ShapeDtypeStruct((m, d), F32), jax.ShapeDtypeStruct((m, d), BF16)],
        compiler_params=pltpu.CompilerParams(dimension_semantics=("parallel",), vmem_limit_bytes=VMEM_LIMIT_BIG),
        name="merge_out_proj",
    )(o_a, o_b, o_c, gsig, w_branch_all, w_out_all, x, g_next.reshape(1, d))


def _rope_table_kernel(pos_ref, inv_ref, c_ref, s1_ref, s2_ref):
    ang = pos_ref[...].astype(F32) * inv_ref[...]
    c = jnp.cos(ang)
    s = jnp.sin(ang)
    lane = lax.broadcasted_iota(jnp.int32, ang.shape, 1)
    half = MLA_ROPE // 2
    c_ref[...] = jnp.where(lane < MLA_ROPE, c, 0.0)
    s1_ref[...] = jnp.where((lane >= half) & (lane < MLA_ROPE), s, 0.0)
    s2_ref[...] = jnp.where(lane < half, -s, 0.0)


def _rope_tables(pos_col, inv128, tm=512):
    s = pos_col.shape[0]
    spec = pl.BlockSpec((tm, LANES), lambda i: (i, 0))
    shp = jax.ShapeDtypeStruct((s, LANES), F32)
    return pl.pallas_call(
        _rope_table_kernel,
        grid=(s // tm,),
        in_specs=[pl.BlockSpec((tm, 1), lambda i: (i, 0)),
                  pl.BlockSpec((1, LANES), lambda i: (0, 0))],
        out_specs=[spec, spec, spec],
        out_shape=[shp, shp, shp],
        compiler_params=_cparams(("parallel",)),
        name="rope_tables",
    )(pos_col, inv128)


def _rope128(x, c, s1, s2):
    half = MLA_ROPE // 2
    return x * c + pltpu.roll(x, half, 1) * s1 + pltpu.roll(x, LANES - half, 1) * s2


def _mla_prep_kernel(cq_ref, ckv_ref, kr_ref, wuq_ref, wuk_ref, wuvt_ref, gcq_ref, gckv_ref, gq_ref, gk_ref, gkr_ref,
                     c_ref, s1_ref, s2_ref, q_out, k_out, vt_out):
    def norm(x, g):
        return x * lax.rsqrt(jnp.mean(x * x, axis=-1, keepdims=True) + EPS) * g

    cqn = norm(cq_ref[...], gcq_ref[...]).astype(BF16)
    ckvn = norm(ckv_ref[...], gckv_ref[...]).astype(BF16)
    qa = jnp.dot(cqn, wuq_ref[...], preferred_element_type=F32)
    ka = jnp.dot(ckvn, wuk_ref[...], preferred_element_type=F32)
    _store_values_t(vt_out, lax.dot_general(wuvt_ref[...], ckvn, (((1,), (1,)), ((), ())),
                                            preferred_element_type=F32))
    krp = kr_ref[...]
    kr_ss = jnp.sum(krp * krp, axis=-1, keepdims=True)
    c, s1, s2 = c_ref[...], s1_ref[...], s2_ref[...]
    gq, gk, gkr = gq_ref[...], gk_ref[...], gkr_ref[...]
    scale = MLA_QK ** -0.5 * LOG2E
    kr_roped = _rope128(krp * gkr, c, s1, s2)
    for h in range(MLA_HEADS):
        qh = qa[:, h * HEAD_PAD:(h + 1) * HEAD_PAD]
        rq = lax.rsqrt(jnp.sum(qh * qh, axis=-1, keepdims=True) * (1.0 / MLA_QK) + EPS)
        qn = qh * rq * gq
        q_out[h, :, 0:LANES] = (qn[:, 0:LANES] * scale).astype(BF16)
        q_out[h, :, LANES:HEAD_PAD] = (_rope128(qn[:, LANES:HEAD_PAD], c, s1, s2) * scale).astype(BF16)
        kn = ka[:, h * MLA_NOPE:(h + 1) * MLA_NOPE]
        rk = lax.rsqrt((jnp.sum(kn * kn, axis=-1, keepdims=True) + kr_ss) * (1.0 / MLA_QK) + EPS)
        k_out[h, :, 0:LANES] = (kn * rk * gk).astype(BF16)
        k_out[h, :, LANES:HEAD_PAD] = (kr_roped * rk).astype(BF16)


def _vt_spec(heads, tm):
    per_tile = ATT_TILE // tm
    return pl.BlockSpec((heads, 1, ATT_VROWS, tm), lambda i: (0, i // per_tile, 0, i % per_tile))


def _mla_prep(p_latent, p_small, wuq_p, wuk, wuv_t, g_cq, g_ckv, g_q, g_k, g_kr, tabs, tm=512):
    s = p_latent.shape[0]
    h = MLA_HEADS
    row = lambda w: pl.BlockSpec((1, w), lambda i: (0, 0))
    tab = pl.BlockSpec((tm, LANES), lambda i: (i, 0))
    return pl.pallas_call(
        _mla_prep_kernel,
        grid=(s // tm,),
        in_specs=[pl.BlockSpec((tm, MLA_RANK), lambda i: (i, A_CQ // MLA_RANK)),
                  pl.BlockSpec((tm, MLA_RANK), lambda i: (i, A_CKV // MLA_RANK)),
                  pl.BlockSpec((tm, LANES), lambda i: (i, A_KR // LANES)),
                  pl.BlockSpec(wuq_p.shape, lambda i: (0, 0)),
                  pl.BlockSpec(wuk.shape, lambda i: (0, 0)),
                  pl.BlockSpec(wuv_t.shape, lambda i: (0, 0)),
                  row(MLA_RANK), row(MLA_RANK), row(HEAD_PAD), row(LANES), row(LANES),
                  tab, tab, tab],
        out_specs=[pl.BlockSpec((h, tm, HEAD_PAD), lambda i: (0, i, 0)),
                   pl.BlockSpec((h, tm, HEAD_PAD), lambda i: (0, i, 0)),
                   _vt_spec(h, tm)],
        out_shape=[jax.ShapeDtypeStruct((h, s, HEAD_PAD), BF16),
                   jax.ShapeDtypeStruct((h, s, HEAD_PAD), BF16),
                   jax.ShapeDtypeStruct((h, s // ATT_TILE, ATT_VROWS, ATT_TILE), BF16)],
        compiler_params=_cparams(("parallel",)),
        name="mla_prep",
    )(p_latent, p_latent, p_small, wuq_p, wuk, wuv_t, g_cq, g_ckv, g_q, g_k, g_kr, *tabs)


def _fox_cumsum_kernel(misc_ref, bf_ref, f_ref, ft_ref, carry_ref):
    blk = misc_ref.shape[0]

    @pl.when(pl.program_id(0) == 0)
    def _():
        carry_ref[...] = jnp.zeros(carry_ref.shape, F32)

    r = lax.broadcasted_iota(jnp.int32, (blk, blk), 0)
    c = lax.broadcasted_iota(jnp.int32, (blk, blk), 1)
    tri = jnp.where(c <= r, 1.0, 0.0).astype(BF16)
    x1, x2, x3 = _split3(_log_sigmoid(misc_ref[...] + bf_ref[...]))
    cs = (jnp.dot(tri, x1, preferred_element_type=F32)
          + jnp.dot(tri, x2, preferred_element_type=F32)
          + jnp.dot(tri, x3, preferred_element_type=F32)) + carry_ref[...]
    f_ref[...] = cs
    ft_ref[...] = cs.T
    carry_ref[...] = cs[blk - 1:blk, :]


def _fox_cumsum(p_small, bf128, blk=512):
    s = p_small.shape[0]
    return pl.pallas_call(
        _fox_cumsum_kernel,
        grid=(s // blk,),
        in_specs=[pl.BlockSpec((blk, LANES), lambda i: (i, A_MISC // LANES)),
                  pl.BlockSpec((1, LANES), lambda i: (0, 0))],
        out_specs=[pl.BlockSpec((blk, LANES), lambda i: (i, 0)),
                   pl.BlockSpec((LANES, blk), lambda i: (0, i))],
        out_shape=[jax.ShapeDtypeStruct((s, LANES), F32), jax.ShapeDtypeStruct((LANES, s), F32)],
        scratch_shapes=[pltpu.VMEM((1, LANES), F32)],
        compiler_params=_cparams(("arbitrary",)),
        name="fox_cumsum",
    )(p_small, bf128)


def _fox_prep_kernel(fq_ref, fk_ref, f_ref, gq_ref, gk_ref, sel_ref, q_out, k_out):
    tm = fq_ref.shape[0]
    lane = lax.broadcasted_iota(jnp.int32, (tm, LANES), 1)
    ones_tail = jnp.where(lane < 3, 1.0, 0.0).astype(BF16)
    pieces = jnp.concatenate(_split3(f_ref[...] * (-LOG2E)), axis=1)
    scale = FOX_DH ** -0.5 * LOG2E
    gq, gk = gq_ref[...], gk_ref[...]
    for h in range(FOX_HEADS):
        sl = slice(h * FOX_DH, (h + 1) * FOX_DH)
        q = fq_ref[:, sl].astype(F32)
        k = fk_ref[:, sl].astype(F32)
        qn = q * lax.rsqrt(jnp.mean(q * q, axis=-1, keepdims=True) + EPS) * gq
        kn = k * lax.rsqrt(jnp.mean(k * k, axis=-1, keepdims=True) + EPS) * gk
        q_out[h, :, 0:LANES] = (qn * scale).astype(BF16)
        q_out[h, :, LANES:HEAD_PAD] = ones_tail
        k_out[h, :, 0:LANES] = kn.astype(BF16)
        k_out[h, :, LANES:HEAD_PAD] = jnp.dot(pieces, sel_ref[h], preferred_element_type=F32).astype(BF16)


def _fox_prep(pb, f_cum, g_q, g_k, tm=512):
    s = pb.shape[0]
    h = FOX_HEADS
    w = FOX_HEADS * FOX_DH
    row = pl.BlockSpec((1, FOX_DH), lambda i: (0, 0))
    hh, pp = jnp.meshgrid(jnp.arange(h), jnp.arange(3), indexing="ij")
    sel = jnp.zeros((h, 3 * LANES, LANES), BF16).at[hh, pp * LANES + hh, pp].set(1.0)
    return pl.pallas_call(
        _fox_prep_kernel,
        grid=(s // tm,),
        in_specs=[pl.BlockSpec((tm, w), lambda i: (i, B_FQ // w)),
                  pl.BlockSpec((tm, w), lambda i: (i, B_FK // w)),
                  pl.BlockSpec((tm, LANES), lambda i: (i, 0)),
                  row, row,
                  pl.BlockSpec(sel.shape, lambda i: (0, 0, 0))],
        out_specs=[pl.BlockSpec((h, tm, HEAD_PAD), lambda i: (0, i, 0)),
                   pl.BlockSpec((h, tm, HEAD_PAD), lambda i: (0, i, 0))],
        out_shape=[jax.ShapeDtypeStruct((h, s, HEAD_PAD), BF16),
                   jax.ShapeDtypeStruct((h, s, HEAD_PAD), BF16)],
        compiler_params=_cparams(("parallel",)),
        name="fox_prep",
    )(pb, pb, f_cum, g_q, g_k, sel)


def _value_t_kernel(h_ref, w_ref, vt_out, w_sc):
    @pl.when(pl.program_id(0) == 0)
    def _():
        w_sc[...] = w_ref[0].astype(BF16)

    vt = lax.dot_general(w_sc[...], h_ref[...], (((1,), (1,)), ((), ())), preferred_element_type=F32)
    _store_values_t(vt_out, vt)


def _value_t_proj(h, w_all_t, layer, row_start, heads, tm=512):
    s, d = h.shape
    nv = heads * LANES
    return pl.pallas_call(
        _value_t_kernel,
        grid=(s // tm,),
        in_specs=[pl.BlockSpec((tm, d), lambda i: (i, 0)),
                  pl.BlockSpec((pl.Element(1), pl.Element(nv), pl.Element(d)), lambda i: (layer, row_start, 0))],
        out_specs=_vt_spec(heads, tm),
        out_shape=jax.ShapeDtypeStruct((heads, s // ATT_TILE, ATT_VROWS, ATT_TILE), BF16),
        scratch_shapes=[pltpu.VMEM((nv, d), BF16)],
        compiler_params=_cparams(("arbitrary",)),
        name="fox_value_t",
    )(h, w_all_t)


def _attn_kernel(*refs, tq, use_f):
    if use_f:
        q_ref, k_ref, vt_ref, f_ref, o_ref, m_sc, acc_sc, s_a, s_b = refs
    else:
        q_ref, k_ref, vt_ref, o_ref, m_sc, acc_sc, s_a, s_b = refs
    ns = tq // ATT_SUB
    nq = q_ref.shape[1] // tq

    def reset():
        m_sc[...] = jnp.full(m_sc.shape, NEG_INF, F32)
        acc_sc[...] = jnp.zeros(acc_sc.shape, F32)

    def logits(dst, qi, kb):
        dst[...] = lax.dot_general(k_ref[0, pl.ds(_aligned(kb * tq, tq), tq), :],
                                   q_ref[0, pl.ds(_aligned(qi * tq, tq), tq), :],
                                   (((1,), (1,)), ((), ())), preferred_element_type=F32)

    def update(src, j, qi, kb, k0, size, causal):
        s = src[k0:k0 + size, j * ATT_SUB:(j + 1) * ATT_SUB]
        if causal:
            key = lax.broadcasted_iota(jnp.int32, (size, ATT_SUB), 0)
            qry = lax.broadcasted_iota(jnp.int32, (size, ATT_SUB), 1)
            s = jnp.where(key <= qry, s, NEG_INF)
        m_old = m_sc[j]
        m_cur = jnp.max(s, axis=0, keepdims=True)
        if use_f:
            ft = f_ref[pl.ds(qi, 1), j * ATT_SUB:(j + 1) * ATT_SUB] * LOG2E
            m_new = jnp.maximum(m_old, m_cur + ft)
            shift = m_new - ft
        else:
            m_new = jnp.maximum(m_old, m_cur)
            shift = m_new
        p = jnp.exp2(s - shift).astype(BF16)
        alpha = jnp.exp2(m_old - m_new)
        acc_sc[j] = alpha * acc_sc[j] + jnp.dot(vt_ref[0, kb, :, k0:k0 + size], p, preferred_element_type=F32)
        m_sc[j] = m_new

    def full_tile(src, qi, kb):
        for j in range(ns):
            update(src, j, qi, kb, 0, tq, False)

    def diagonal_tile(src, qi):
        for j in range(ns):
            if j > 0:
                update(src, j, qi, qi, 0, j * ATT_SUB, False)
            update(src, j, qi, qi, j * ATT_SUB, ATT_SUB, True)
        for j in range(ns):
            acc = acc_sc[j]
            o_t = acc[0:LANES, :] / acc[LANES:LANES + 1, :]
            o_ref[pl.ds(_aligned(qi * tq + j * ATT_SUB, ATT_SUB), ATT_SUB), :] = o_t.T.astype(o_ref.dtype)
        reset()

    def query_tile(qi, cur, oth, odd):
        def pair(u, carry):
            t = 2 * u
            logits(oth, qi, t + 1)
            full_tile(cur, qi, t)
            logits(cur, qi, t + 2)
            full_tile(oth, qi, t + 1)
            return carry

        lax.fori_loop(0, qi // 2, pair, 0)
        q_next = jnp.minimum(qi + 1, nq - 1)
        if odd:
            logits(oth, qi, qi)
            full_tile(cur, qi, qi - 1)
            logits(cur, q_next, 0)
            diagonal_tile(oth, qi)
        else:
            logits(oth, q_next, 0)
            diagonal_tile(cur, qi)

    assert nq % 4 == 0
    reset()
    logits(s_a, 0, 0)

    def four_query_tiles(g, carry):
        q0 = 4 * g
        query_tile(q0, s_a, s_b, False)
        query_tile(q0 + 1, s_b, s_a, True)
        query_tile(q0 + 2, s_b, s_a, False)
        query_tile(q0 + 3, s_a, s_b, True)
        return carry

    lax.fori_loop(0, nq // 4, four_query_tiles, 0)


def _attention(q, k, vt, f_t=None):
    hh, s, dk = q.shape
    _, nt, vrows, tq = vt.shape
    use_f = f_t is not None
    in_specs = [pl.BlockSpec((1, s, dk), lambda h: (h, 0, 0)),
                pl.BlockSpec((1, s, dk), lambda h: (h, 0, 0)),
                pl.BlockSpec((1, nt, vrows, tq), lambda h: (h, 0, 0, 0))]
    args = [q, k, vt]
    if use_f:
        in_specs.append(pl.BlockSpec((None, nt, tq), lambda h: (h, 0, 0)))
        args.append(f_t.reshape(f_t.shape[0], nt, tq))
    ns = tq // ATT_SUB
    return pl.pallas_call(
        functools.partial(_attn_kernel, tq=tq, use_f=use_f),
        grid=(hh,),
        in_specs=in_specs,
        out_specs=pl.BlockSpec((s, LANES), lambda h: (0, h)),
        out_shape=jax.ShapeDtypeStruct((s, hh * LANES), BF16),
        scratch_shapes=[pltpu.VMEM((ns, 1, ATT_SUB), F32), pltpu.VMEM((ns, vrows, ATT_SUB), F32),
                        pltpu.VMEM((tq, tq), F32), pltpu.VMEM((tq, tq), F32)],
        compiler_params=_cparams(("parallel",)),
        name="fox_attention" if use_f else "mla_attention",
    )(*args)


def _gla_kernel(q_ref, k_ref, v_ref, r_ref, misc_ref, wa_ref, ba_ref, go_ref, o_ref,
                st_ref, la_ref, b_ref, qe_ref, a_ref, u_ref, dl_ref, stb_ref):
    tb = q_ref.shape[0]
    nchunk = tb // GLA_CHUNK
    nsub = GLA_CHUNK // GLA_SUB
    C = GLA_CHUNK

    @pl.when(pl.program_id(1) == 0)
    def _():
        st_ref[...] = jnp.zeros(st_ref.shape, F32)

    m1, m2, _ = _split3(misc_ref[...])
    w1, w2, _ = _split3(wa_ref[...])
    z = (jnp.dot(m1, w1, preferred_element_type=F32) + jnp.dot(m1, w2, preferred_element_type=F32)
         + jnp.dot(m2, w1, preferred_element_type=F32)) + ba_ref[...]
    la_ref[...] = _log_sigmoid(z) * (LOG2E / GLA_TAU)

    r_i = lax.broadcasted_iota(jnp.int32, (C, C), 0)
    c_i = lax.broadcasted_iota(jnp.int32, (C, C), 1)
    tri = jnp.where(c_i <= r_i, 1.0, 0.0).astype(BF16)
    sub_row = lax.broadcasted_iota(jnp.int32, (GLA_SUB, GLA_DK), 0)
    col_sub = lax.broadcasted_iota(jnp.int32, (GLA_SUB, C), 1)
    scale = GLA_DK ** -0.5
    g_o = go_ref[...]

    chunk_rows = [slice(ci * C, (ci + 1) * C) for ci in range(nchunk)]

    for rows in chunk_rows:
        l1, l2, l3 = _split3(la_ref[rows, :])
        b_ref[rows, :] = (jnp.dot(tri, l1, preferred_element_type=F32) + jnp.dot(tri, l2, preferred_element_type=F32)
                          + jnp.dot(tri, l3, preferred_element_type=F32))

    for ci, rows in enumerate(chunk_rows):
        b = b_ref[rows, :]
        qc = q_ref[rows, :].astype(F32) * scale
        kc = k_ref[rows, :].astype(F32)
        b_last = b[C - 1:C, :]
        qe_ref[rows, :] = (qc * jnp.exp2(b)).astype(BF16)
        ke = (kc * jnp.exp2(b_last - b)).astype(BF16)
        u_ref[ci] = lax.dot_general(v_ref[rows, :], ke, (((0,), (0,)), ((), ())), preferred_element_type=F32)
        dl_ref[ci] = jnp.exp2(b_last)

        a_rows = []
        for si in range(nsub):
            lo = si * GLA_SUB
            qs, ks, bs = qc[lo:lo + GLA_SUB], kc[lo:lo + GLA_SUB], b[lo:lo + GLA_SUB]
            a = jnp.zeros((GLA_SUB, C), F32)
            for j in range(GLA_SUB):
                e = jnp.exp2(jnp.where(sub_row >= j, bs - bs[j:j + 1, :], NEG_INF))
                w = jnp.sum(qs * (ks[j:j + 1, :] * e), axis=-1, keepdims=True)
                a = jnp.where(col_sub == lo + j, w, a)
            if si > 0:
                ref_row = b[lo - 1:lo, :]
                q_i = qs * jnp.exp2(bs - ref_row)
                k_j = jnp.concatenate([kc[0:lo] * jnp.exp2(ref_row - b[0:lo]),
                                       jnp.zeros((C - lo, GLA_DK), F32)], axis=0)
                a = a + lax.dot_general(q_i.astype(BF16), k_j.astype(BF16), (((1,), (1,)), ((), ())),
                                        preferred_element_type=F32)
            a_rows.append(a)
        a_ref[rows, :] = jnp.concatenate(a_rows, axis=0).astype(BF16)

    st = st_ref[...]
    for ci in range(nchunk):
        stb_ref[ci] = st.astype(BF16)
        st = st * dl_ref[ci] + u_ref[ci]
    st_ref[...] = st

    for ci, rows in enumerate(chunk_rows):
        o = lax.dot_general(qe_ref[rows, :], stb_ref[ci], (((1,), (1,)), ((), ())), preferred_element_type=F32)
        o = o + jnp.dot(a_ref[rows, :], v_ref[rows, :], preferred_element_type=F32)
        on = o * lax.rsqrt(jnp.mean(o * o, axis=-1, keepdims=True) + EPS) * g_o
        r = r_ref[rows, :].astype(F32)
        o_ref[rows, :] = (on * (r * _sigmoid(r))).astype(o_ref.dtype)


def _gla(pb, p_small, wa_p, b_a, g_o, tb=GLA_BLOCK):
    s = pb.shape[0]
    nchunk = tb // GLA_CHUNK
    return pl.pallas_call(
        _gla_kernel,
        grid=(GLA_HEADS, s // tb),
        in_specs=[pl.BlockSpec((tb, GLA_DK), lambda h, i: (i, B_GQ // GLA_DK + h)),
                  pl.BlockSpec((tb, GLA_DK), lambda h, i: (i, B_GK // GLA_DK + h)),
                  pl.BlockSpec((tb, GLA_DV), lambda h, i: (i, B_GV // GLA_DV + h)),
                  pl.BlockSpec((tb, GLA_DV), lambda h, i: (i, B_GR // GLA_DV + h)),
                  pl.BlockSpec((tb, LANES), lambda h, i: (i, A_MISC // LANES)),
                  pl.BlockSpec((LANES, GLA_DK), lambda h, i: (0, h)),
                  pl.BlockSpec((1, GLA_DK), lambda h, i: (0, h)),
                  pl.BlockSpec((1, GLA_DV), lambda h, i: (0, 0))],
        out_specs=pl.BlockSpec((tb, GLA_DV), lambda h, i: (i, h)),
        out_shape=jax.ShapeDtypeStruct((s, GLA_HEADS * GLA_DV), BF16),
        scratch_shapes=[pltpu.VMEM((GLA_DV, GLA_DK), F32),
                        pltpu.VMEM((tb, GLA_DK), F32),
                        pltpu.VMEM((tb, GLA_DK), F32),
                        pltpu.VMEM((tb, GLA_DK), BF16),
                        pltpu.VMEM((tb, GLA_CHUNK), BF16),
                        pltpu.VMEM((nchunk, GLA_DV, GLA_DK), F32),
                        pltpu.VMEM((nchunk, 1, GLA_DK), F32),
                        pltpu.VMEM((nchunk, GLA_DV, GLA_DK), BF16)],
        compiler_params=_cparams(("parallel", "arbitrary")),
        name="gla",
    )(pb, pb, pb, pb, p_small, wa_p, b_a, g_o)


def _in_proj_kernel(a_ref, wt_ref, o_ref, w_sc, *, act):
    @pl.when(pl.program_id(1) == 0)
    def _():
        w_sc[...] = wt_ref[0].astype(BF16)

    y = lax.dot_general(a_ref[...], w_sc[...], (((1,), (1,)), ((), ())), preferred_element_type=F32)
    if act == "sigmoid":
        y = _sigmoid(y)
    o_ref[...] = y.astype(o_ref.dtype)


def _in_proj(a, w_all_t, layer, row_start, n, out_dtype, tm, tn, act=None, name="in_proj"):
    m, k = a.shape
    return pl.pallas_call(
        functools.partial(_in_proj_kernel, act=act),
        grid=(n // tn, m // tm),
        in_specs=[pl.BlockSpec((tm, k), lambda j, i: (i, 0)),
                  pl.BlockSpec((pl.Element(1), pl.Element(tn), pl.Element(k)),
                               lambda j, i: (layer, _aligned(row_start(j), F32_SUBLANES), 0))],
        out_specs=pl.BlockSpec((tm, tn), lambda j, i: (i, j)),
        out_shape=jax.ShapeDtypeStruct((m, n), out_dtype),
        scratch_shapes=[pltpu.VMEM((tn, k), BF16)],
        compiler_params=pltpu.CompilerParams(dimension_semantics=("parallel", "arbitrary"),
                                             vmem_limit_bytes=VMEM_LIMIT_BIG),
        name=name,
    )(a, w_all_t)


def _small_w_kernel(kr_ref, fl_ref, ga_ref, o_ref):
    d = o_ref.shape[1]
    z = lambda n: jnp.zeros((n, d), F32)
    o_ref[...] = jnp.concatenate([kr_ref[0], z(LANES - MLA_ROPE), fl_ref[0], ga_ref[0],
                                  z(LANES - FOX_HEADS - GLA_GATE_RANK)], axis=0).astype(BF16)


def _small_in_proj_weights(w_all_t, layer):
    d = w_all_t.shape[2]

    def rows(i):
        return pl.BlockSpec((pl.Element(1), pl.Element(IN_SPLIT[i]), pl.Element(d)),
                            lambda g: (layer, IN_OFF[i], 0))

    return pl.pallas_call(
        _small_w_kernel,
        grid=(1,),
        in_specs=[rows(2), rows(11), rows(6)],
        out_specs=pl.BlockSpec((A_SMALL, d), lambda g: (0, 0)),
        out_shape=jax.ShapeDtypeStruct((A_SMALL, d), BF16),
        compiler_params=_cparams(("arbitrary",)),
        name="small_in_proj_weights",
    )(w_all_t, w_all_t, w_all_t)


def _pad_lanes(v, n):
    return jnp.pad(v, (0, n - v.shape[0])).reshape(1, n)


def _layer(x, h, tabs, g_next, w_in_all_t, layer, g_cq, w_uq, g_ckv, w_ukv, g_mla_q, g_mla_k, w_a2, b_a, g_gla_o,
           g_fox_q, g_fox_k, b_f, w_branch_all, w_out_all, g_ffn, w_gu_all, w_down_all):
    tn = 1024
    off = IN_OFF
    p_latent = _in_proj(h, w_in_all_t, layer, lambda j: off[0], A_LATENT, F32, 1024, tn, name="in_proj_latent")
    p_small = _matmul_nt(h, _small_in_proj_weights(w_in_all_t, layer), F32, 1024, A_SMALL, name="in_proj_small")
    n_first = (off[6] - off[3]) // tn
    pb = _in_proj(h, w_in_all_t, layer,
                  lambda j: jnp.where(j < n_first, off[3] + j * tn, off[7] + (j - n_first) * tn),
                  B_WIDTH, BF16, 2048, tn, name="in_proj_b")
    gsig = _in_proj(h, w_in_all_t, layer, lambda j: off[12] + j * tn, N_BRANCH * D_MODEL, BF16, 2048, tn,
                    act="sigmoid", name="in_proj_gates")

    wuq_p = jnp.pad(w_uq.reshape(MLA_RANK, MLA_HEADS, MLA_QK),
                    ((0, 0), (0, 0), (0, HEAD_PAD - MLA_QK))).reshape(MLA_RANK, MLA_HEADS * HEAD_PAD).astype(BF16)
    w_ukv3 = w_ukv.reshape(MLA_RANK, MLA_HEADS, MLA_NOPE + MLA_V)
    wuk = w_ukv3[:, :, :MLA_NOPE].reshape(MLA_RANK, MLA_HEADS * MLA_NOPE).astype(BF16)
    wuv_t = w_ukv3[:, :, MLA_NOPE:].reshape(MLA_RANK, MLA_HEADS * MLA_V).T.astype(BF16)
    q_a, k_a, vt_a = _mla_prep(p_latent, p_small, wuq_p, wuk, wuv_t, g_cq.reshape(1, -1), g_ckv.reshape(1, -1),
                               _pad_lanes(g_mla_q, HEAD_PAD), g_mla_k[:MLA_NOPE].reshape(1, -1),
                               _pad_lanes(g_mla_k[MLA_NOPE:], LANES), tabs)
    o_a = _attention(q_a, k_a, vt_a)

    wa_p = jnp.zeros((LANES, GLA_HEADS * GLA_DK), F32).at[MISC_GA:MISC_GA + GLA_GATE_RANK].set(w_a2)
    o_b = _gla(pb, p_small, wa_p, b_a.reshape(1, -1), g_gla_o.reshape(1, -1))

    f_cum, f_cum_t = _fox_cumsum(p_small,_pad_lanes(b_f, LANES))
    q_c, k_c = _fox_prep(pb, f_cum, g_fox_q.reshape(1, -1), g_fox_k.reshape(1, -1))
    vt_c = _value_t_proj(h, w_in_all_t, layer, off[10], FOX_HEADS)
    o_c = _attention(q_c, k_c, vt_c, f_cum_t)

    x1, h2 = _merge_out(o_a, o_b, o_c, gsig, w_branch_all, w_out_all, layer, x, g_ffn)
    act = _swiglu_up(h2, w_gu_all, layer, 1024, 512)
    if g_next is None:
        return _matmul_residual(act, w_down_all, layer, x1, 256, "ffn_down"), None
    return _matmul_residual(act, w_down_all, layer, x1, 256, "ffn_down", g_next=g_next)


def kernel(x, positions, g_mix, w_in, g_cq, w_uq, g_ckv, w_ukv, g_mla_q, g_mla_k, w_a2, b_a, g_gla_o,
           g_fox_q, g_fox_k, b_f, w_branch, w_out, g_ffn, w_gu, w_down):
    bsz, s, d = x.shape
    assert bsz == 1 and d == D_MODEL and s % ATT_TILE == 0
    half = MLA_ROPE // 2
    inv = ROPE_THETA ** (-jnp.arange(half, dtype=F32) / half)
    inv128 = jnp.concatenate([inv, inv, jnp.zeros((LANES - MLA_ROPE,), F32)]).reshape(1, LANES)
    tabs = _rope_tables(positions.reshape(s, 1), inv128)
    depth = w_in.shape[0]
    w_in = jnp.swapaxes(w_in, 1, 2)
    w_branch = w_branch.reshape(depth, N_BRANCH * BRANCH_WIDTH, D_MODEL).astype(BF16)
    w_out = w_out.astype(BF16)
    w_down = w_down.astype(BF16)
    xcur = x.reshape(s, d)
    hcur = _rmsnorm(xcur, g_mix[0])
    for l in range(depth):
        g_next = g_mix[l + 1] if l + 1 < depth else None
        xcur, hcur = _layer(xcur, hcur, tabs, g_next, w_in, l, g_cq[l], w_uq[l], g_ckv[l], w_ukv[l], g_mla_q[l],
                            g_mla_k[l], w_a2[l], b_a[l], g_gla_o[l], g_fox_q[l], g_fox_k[l], b_f[l], w_branch,
                            w_out, g_ffn[l], w_gu, w_down)
    return xcur.reshape(bsz, s, d)
```

```python
import functools

import jax
import jax.numpy as jnp
from jax import lax
from jax.experimental import pallas as pl
from jax.experimental.pallas import tpu as pltpu

F32 = jnp.float32
BF16 = jnp.bfloat16

D_MODEL = 2048
BRANCH_WIDTH = D_MODEL // 2
N_BRANCH = 3
MLA_NOPE = 128
MLA_ROPE = 64
MLA_V = 128
MLA_HEADS = BRANCH_WIDTH // MLA_V
MLA_QK = MLA_NOPE + MLA_ROPE
MLA_RANK = 512
ROPE_THETA = 10000.0
GLA_HEADS = 4
GLA_DV = BRANCH_WIDTH // GLA_HEADS
GLA_DK = GLA_DV // 2
GLA_GATE_RANK = 16
GLA_TAU = 16.0
FOX_DH = 128
FOX_HEADS = BRANCH_WIDTH // FOX_DH
EPS = 1e-6
NEG_INF = -1e30

LANES = 128
F32_SUBLANES = 8
BF16_SUBLANES = 16
HEAD_PAD = 256
VMEM_LIMIT = 48 * 1024 * 1024
VMEM_LIMIT_BIG = 56 * 1024 * 1024

IN_SPLIT = (MLA_RANK, MLA_RANK, MLA_ROPE, GLA_HEADS * GLA_DK, GLA_HEADS * GLA_DK, GLA_HEADS * GLA_DV, GLA_GATE_RANK,
            GLA_HEADS * GLA_DV, FOX_HEADS * FOX_DH, FOX_HEADS * FOX_DH, FOX_HEADS * FOX_DH, FOX_HEADS,
            N_BRANCH * D_MODEL)
IN_OFF = tuple(sum(IN_SPLIT[:i]) for i in range(len(IN_SPLIT) + 1))

A_CQ, A_CKV = 0, 512
A_LATENT = 1024
A_KR, A_MISC = 0, 128
A_SMALL = 256
MISC_GA = FOX_HEADS
B_GQ, B_GK, B_GV, B_GR, B_FQ, B_FK = 0, 512, 1024, 2048, 3072, 4096
B_WIDTH = 5120

ATT_TILE = 1024
ATT_SUB = 256
ATT_VROWS = LANES + BF16_SUBLANES
LOG2E = 1.4426950408889634

GLA_CHUNK = 64
GLA_SUB = 8
GLA_BLOCK = 1024


def _aligned(start, m):
    return start if isinstance(start, int) else pl.multiple_of(start, m)


def _cparams(sem):
    return pltpu.CompilerParams(dimension_semantics=sem, vmem_limit_bytes=VMEM_LIMIT)


def _split3(x):
    x1 = x.astype(BF16)
    r1 = x - x1.astype(F32)
    x2 = r1.astype(BF16)
    x3 = (r1 - x2.astype(F32)).astype(BF16)
    return x1, x2, x3


def _sigmoid(x):
    return 0.5 * jnp.tanh(0.5 * x) + 0.5


def _log_sigmoid(z):
    return -(jnp.maximum(-z, 0.0) + jnp.log1p(jnp.exp(-jnp.abs(z))))


def _store_values_t(vt_out, vt):
    rows = vt.shape[1]
    r = lax.broadcasted_iota(jnp.int32, (BF16_SUBLANES, rows), 0)
    ones_rows = jnp.where(r == 0, 1.0, 0.0).astype(BF16)
    for h in range(vt.shape[0] // LANES):
        vt_out[h, 0, 0:LANES, :] = vt[h * LANES:(h + 1) * LANES, :].astype(BF16)
        vt_out[h, 0, LANES:ATT_VROWS, :] = ones_rows


def _rmsnorm_kernel(x_ref, g_ref, o_ref):
    x = x_ref[...]
    ms = jnp.mean(x * x, axis=-1, keepdims=True)
    o_ref[...] = (x * lax.rsqrt(ms + EPS) * g_ref[...]).astype(o_ref.dtype)


def _rmsnorm(x, g, tm=512):
    m, d = x.shape
    return pl.pallas_call(
        _rmsnorm_kernel,
        grid=(m // tm,),
        in_specs=[pl.BlockSpec((tm, d), lambda i: (i, 0)),
                  pl.BlockSpec((1, d), lambda i: (0, 0))],
        out_specs=pl.BlockSpec((tm, d), lambda i: (i, 0)),
        out_shape=jax.ShapeDtypeStruct((m, d), BF16),
        compiler_params=_cparams(("parallel",)),
        name="rmsnorm",
    )(x, g.reshape(1, d))


def _mm_nt_kernel(a_ref, wt_ref, o_ref):
    y = lax.dot_general(a_ref[...], wt_ref[...], (((1,), (1,)), ((), ())), preferred_element_type=F32)
    o_ref[...] = y.astype(o_ref.dtype)


def _matmul_nt(a, w_t, out_dtype, tm, tn, name):
    m, k = a.shape
    n = w_t.shape[0]
    return pl.pallas_call(
        _mm_nt_kernel,
        grid=(m // tm, n // tn),
        in_specs=[pl.BlockSpec((tm, k), lambda i, j: (i, 0)),
                  pl.BlockSpec((tn, k), lambda i, j: (j, 0))],
        out_specs=pl.BlockSpec((tm, tn), lambda i, j: (i, j)),
        out_shape=jax.ShapeDtypeStruct((m, n), out_dtype),
        compiler_params=_cparams(("parallel", "arbitrary")),
        name=name,
    )(a, w_t)


def _mm_res_kernel(*refs, with_norm):
    if with_norm:
        a_ref, w_ref, r_ref, g_ref, o_ref, h_ref = refs
    else:
        a_ref, w_ref, r_ref, o_ref = refs
    y = r_ref[...] + jnp.dot(a_ref[...], w_ref[...], preferred_element_type=F32)
    o_ref[...] = y
    if with_norm:
        ms = jnp.mean(y * y, axis=-1, keepdims=True)
        h_ref[...] = (y * lax.rsqrt(ms + EPS) * g_ref[...]).astype(h_ref.dtype)


def _matmul_residual(a, w_all, layer, res, tm, name, g_next=None):
    m, k = a.shape
    n = w_all.shape[2]
    with_norm = g_next is not None
    row = pl.BlockSpec((tm, n), lambda i: (i, 0))
    in_specs = [pl.BlockSpec((tm, k), lambda i: (i, 0)),
                pl.BlockSpec((None, k, n), lambda i: (layer, 0, 0), pipeline_mode=pl.Buffered(1)),
                row]
    args = [a, w_all, res]
    out_specs, out_shape = row, jax.ShapeDtypeStruct((m, n), F32)
    if with_norm:
        in_specs.append(pl.BlockSpec((1, n), lambda i: (0, 0)))
        args.append(g_next.reshape(1, n))
        out_specs, out_shape = [row, row], [out_shape, jax.ShapeDtypeStruct((m, n), BF16)]
    return pl.pallas_call(
        functools.partial(_mm_res_kernel, with_norm=with_norm),
        grid=(m // tm,),
        in_specs=in_specs,
        out_specs=out_specs,
        out_shape=out_shape,
        compiler_params=pltpu.CompilerParams(dimension_semantics=("parallel",), vmem_limit_bytes=VMEM_LIMIT_BIG),
        name=name,
    )(*args)


def _swiglu_kernel(a_ref, wg_ref, wu_ref, o_ref, wg_sc, wu_sc):
    @pl.when(pl.program_id(1) == 0)
    def _():
        wg_sc[...] = wg_ref[...].astype(BF16)
        wu_sc[...] = wu_ref[...].astype(BF16)

    a = a_ref[...]
    g = jnp.dot(a, wg_sc[...], preferred_element_type=F32)
    u = jnp.dot(a, wu_sc[...], preferred_element_type=F32)
    o_ref[...] = (g * _sigmoid(g) * u).astype(o_ref.dtype)


def _swiglu_up(a, w_gu_all, layer, tm, tn):
    m, k = a.shape
    hidden = w_gu_all.shape[2] // 2
    nj = hidden // tn
    return pl.pallas_call(
        _swiglu_kernel,
        grid=(nj, m // tm),
        in_specs=[pl.BlockSpec((tm, k), lambda j, i: (i, 0)),
                  pl.BlockSpec((None, k, tn), lambda j, i: (layer, 0, j)),
                  pl.BlockSpec((None, k, tn), lambda j, i: (layer, 0, j + nj))],
        out_specs=pl.BlockSpec((tm, tn), lambda j, i: (i, j)),
        out_shape=jax.ShapeDtypeStruct((m, hidden), BF16),
        scratch_shapes=[pltpu.VMEM((k, tn), BF16), pltpu.VMEM((k, tn), BF16)],
        compiler_params=_cparams(("parallel", "arbitrary")),
        name="ffn_up",
    )(a, w_gu_all, w_gu_all)


def _merge_out_kernel(oa_ref, ob_ref, oc_ref, gs_ref, wb_ref, wo_ref, x_ref, g_ref, x1_ref, h_ref):
    bw = oa_ref.shape[1]
    d = x_ref.shape[1]
    merged = None
    for n, o_ref in enumerate((oa_ref, ob_ref, oc_ref)):
        y = jnp.dot(o_ref[...], wb_ref[n * bw:(n + 1) * bw, :], preferred_element_type=F32)
        y = y * gs_ref[:, n * d:(n + 1) * d].astype(F32)
        merged = y if merged is None else merged + y
    x1 = x_ref[...] + jnp.dot(merged.astype(BF16), wo_ref[...], preferred_element_type=F32)
    x1_ref[...] = x1
    ms = jnp.mean(x1 * x1, axis=-1, keepdims=True)
    h_ref[...] = (x1 * lax.rsqrt(ms + EPS) * g_ref[...]).astype(h_ref.dtype)


def _merge_out(o_a, o_b, o_c, gsig, w_branch_all, w_out_all, layer, x, g_next, tm=256):
    m, bw = o_a.shape
    d = x.shape[1]
    o_spec = pl.BlockSpec((tm, bw), lambda i: (i, 0))
    row = pl.BlockSpec((tm, d), lambda i: (i, 0))
    return pl.pallas_call(
        _merge_out_kernel,
        grid=(m // tm,),
        in_specs=[o_spec, o_spec, o_spec,
                  pl.BlockSpec((tm, N_BRANCH * d), lambda i: (i, 0)),
                  pl.BlockSpec((None, N_BRANCH * bw, d), lambda i: (layer, 0, 0), pipeline_mode=pl.Buffered(1)),
                  pl.BlockSpec((None, d, d), lambda i: (layer, 0, 0), pipeline_mode=pl.Buffered(1)),
                  row,
                  pl.BlockSpec((1, d), lambda i: (0, 0))],
        out_specs=[row, row],
        out_shape=[jax.ShapeDtypeStruct((m, d), F32), jax.ShapeDtypeStruct((m, d), BF16)],
        compiler_params=pltpu.CompilerParams(dimension_semantics=("parallel",), vmem_limit_bytes=VMEM_LIMIT_BIG),
        name="merge_out_proj",
    )(o_a, o_b, o_c, gsig, w_branch_all, w_out_all, x, g_next.reshape(1, d))


def _rope_table_kernel(pos_ref, inv_ref, c_ref, s1_ref, s2_ref):
    ang = pos_ref[...].astype(F32) * inv_ref[...]
    c = jnp.cos(ang)
    s = jnp.sin(ang)
    lane = lax.broadcasted_iota(jnp.int32, ang.shape, 1)
    half = MLA_ROPE // 2
    c_ref[...] = jnp.where(lane < MLA_ROPE, c, 0.0)
    s1_ref[...] = jnp.where((lane >= half) & (lane < MLA_ROPE), s, 0.0)
    s2_ref[...] = jnp.where(lane < half, -s, 0.0)


def _rope_tables(pos_col, inv128, tm=512):
    s = pos_col.shape[0]
    spec = pl.BlockSpec((tm, LANES), lambda i: (i, 0))
    shp = jax.ShapeDtypeStruct((s, LANES), F32)
    return pl.pallas_call(
        _rope_table_kernel,
        grid=(s // tm,),
        in_specs=[pl.BlockSpec((tm, 1), lambda i: (i, 0)),
                  pl.BlockSpec((1, LANES), lambda i: (0, 0))],
        out_specs=[spec, spec, spec],
        out_shape=[shp, shp, shp],
        compiler_params=_cparams(("parallel",)),
        name="rope_tables",
    )(pos_col, inv128)


def _rope128(x, c, s1, s2):
    half = MLA_ROPE // 2
    return x * c + pltpu.roll(x, half, 1) * s1 + pltpu.roll(x, LANES - half, 1) * s2


def _mla_prep_kernel(cq_ref, ckv_ref, kr_ref, wuq_ref, wuk_ref, wuvt_ref, gcq_ref, gckv_ref, gq_ref, gk_ref, gkr_ref,
                     c_ref, s1_ref, s2_ref, q_out, k_out, vt_out):
    def norm(x, g):
        return x * lax.rsqrt(jnp.mean(x * x, axis=-1, keepdims=True) + EPS) * g

    cqn = norm(cq_ref[...], gcq_ref[...]).astype(BF16)
    ckvn = norm(ckv_ref[...], gckv_ref[...]).astype(BF16)
    qa = jnp.dot(cqn, wuq_ref[...], preferred_element_type=F32)
    ka = jnp.dot(ckvn, wuk_ref[...], preferred_element_type=F32)
    _store_values_t(vt_out, lax.dot_general(wuvt_ref[...], ckvn, (((1,), (1,)), ((), ())),
                                            preferred_element_type=F32))
    krp = kr_ref[...]
    kr_ss = jnp.sum(krp * krp, axis=-1, keepdims=True)
    c, s1, s2 = c_ref[...], s1_ref[...], s2_ref[...]
    gq, gk, gkr = gq_ref[...], gk_ref[...], gkr_ref[...]
    scale = MLA_QK ** -0.5 * LOG2E
    kr_roped = _rope128(krp * gkr, c, s1, s2)
    for h in range(MLA_HEADS):
        qh = qa[:, h * HEAD_PAD:(h + 1) * HEAD_PAD]
        rq = lax.rsqrt(jnp.sum(qh * qh, axis=-1, keepdims=True) * (1.0 / MLA_QK) + EPS)
        qn = qh * rq * gq
        q_out[h, :, 0:LANES] = (qn[:, 0:LANES] * scale).astype(BF16)
        q_out[h, :, LANES:HEAD_PAD] = (_rope128(qn[:, LANES:HEAD_PAD], c, s1, s2) * scale).astype(BF16)
        kn = ka[:, h * MLA_NOPE:(h + 1) * MLA_NOPE]
        rk = lax.rsqrt((jnp.sum(kn * kn, axis=-1, keepdims=True) + kr_ss) * (1.0 / MLA_QK) + EPS)
        k_out[h, :, 0:LANES] = (kn * rk * gk).astype(BF16)
        k_out[h, :, LANES:HEAD_PAD] = (kr_roped * rk).astype(BF16)


def _vt_spec(heads, tm):
    per_tile = ATT_TILE // tm
    return pl.BlockSpec((heads, 1, ATT_VROWS, tm), lambda i: (0, i // per_tile, 0, i % per_tile))


def _mla_prep(p_latent, p_small, wuq_p, wuk, wuv_t, g_cq, g_ckv, g_q, g_k, g_kr, tabs, tm=512):
    s = p_latent.shape[0]
    h = MLA_HEADS
    row = lambda w: pl.BlockSpec((1, w), lambda i: (0, 0))
    tab = pl.BlockSpec((tm, LANES), lambda i: (i, 0))
    return pl.pallas_call(
        _mla_prep_kernel,
        grid=(s // tm,),
        in_specs=[pl.BlockSpec((tm, MLA_RANK), lambda i: (i, A_CQ // MLA_RANK)),
                  pl.BlockSpec((tm, MLA_RANK), lambda i: (i, A_CKV // MLA_RANK)),
                  pl.BlockSpec((tm, LANES), lambda i: (i, A_KR // LANES)),
                  pl.BlockSpec(wuq_p.shape, lambda i: (0, 0)),
                  pl.BlockSpec(wuk.shape, lambda i: (0, 0)),
                  pl.BlockSpec(wuv_t.shape, lambda i: (0, 0)),
                  row(MLA_RANK), row(MLA_RANK), row(HEAD_PAD), row(LANES), row(LANES),
                  tab, tab, tab],
        out_specs=[pl.BlockSpec((h, tm, HEAD_PAD), lambda i: (0, i, 0)),
                   pl.BlockSpec((h, tm, HEAD_PAD), lambda i: (0, i, 0)),
                   _vt_spec(h, tm)],
        out_shape=[jax.ShapeDtypeStruct((h, s, HEAD_PAD), BF16),
                   jax.ShapeDtypeStruct((h, s, HEAD_PAD), BF16),
                   jax.ShapeDtypeStruct((h, s // ATT_TILE, ATT_VROWS, ATT_TILE), BF16)],
        compiler_params=_cparams(("parallel",)),
        name="mla_prep",
    )(p_latent, p_latent, p_small, wuq_p, wuk, wuv_t, g_cq, g_ckv, g_q, g_k, g_kr, *tabs)


def _fox_cumsum_kernel(misc_ref, bf_ref, f_ref, ft_ref, carry_ref):
    blk = misc_ref.shape[0]

    @pl.when(pl.program_id(0) == 0)
    def _():
        carry_ref[...] = jnp.zeros(carry_ref.shape, F32)

    r = lax.broadcasted_iota(jnp.int32, (blk, blk), 0)
    c = lax.broadcasted_iota(jnp.int32, (blk, blk), 1)
    tri = jnp.where(c <= r, 1.0, 0.0).astype(BF16)
    x1, x2, x3 = _split3(_log_sigmoid(misc_ref[...] + bf_ref[...]))
    cs = (jnp.dot(tri, x1, preferred_element_type=F32)
          + jnp.dot(tri, x2, preferred_element_type=F32)
          + jnp.dot(tri, x3, preferred_element_type=F32)) + carry_ref[...]
    f_ref[...] = cs
    ft_ref[...] = cs.T
    carry_ref[...] = cs[blk - 1:blk, :]


def _fox_cumsum(p_small, bf128, blk=512):
    s = p_small.shape[0]
    return pl.pallas_call(
        _fox_cumsum_kernel,
        grid=(s // blk,),
        in_specs=[pl.BlockSpec((blk, LANES), lambda i: (i, A_MISC // LANES)),
                  pl.BlockSpec((1, LANES), lambda i: (0, 0))],
        out_specs=[pl.BlockSpec((blk, LANES), lambda i: (i, 0)),
                   pl.BlockSpec((LANES, blk), lambda i: (0, i))],
        out_shape=[jax.ShapeDtypeStruct((s, LANES), F32), jax.ShapeDtypeStruct((LANES, s), F32)],
        scratch_shapes=[pltpu.VMEM((1, LANES), F32)],
        compiler_params=_cparams(("arbitrary",)),
        name="fox_cumsum",
    )(p_small, bf128)


def _fox_prep_kernel(fq_ref, fk_ref, f_ref, gq_ref, gk_ref, sel_ref, q_out, k_out):
    tm = fq_ref.shape[0]
    lane = lax.broadcasted_iota(jnp.int32, (tm, LANES), 1)
    ones_tail = jnp.where(lane < 3, 1.0, 0.0).astype(BF16)
    pieces = jnp.concatenate(_split3(f_ref[...] * (-LOG2E)), axis=1)
    scale = FOX_DH ** -0.5 * LOG2E
    gq, gk = gq_ref[...], gk_ref[...]
    for h in range(FOX_HEADS):
        sl = slice(h * FOX_DH, (h + 1) * FOX_DH)
        q = fq_ref[:, sl].astype(F32)
        k = fk_ref[:, sl].astype(F32)
        qn = q * lax.rsqrt(jnp.mean(q * q, axis=-1, keepdims=True) + EPS) * gq
        kn = k * lax.rsqrt(jnp.mean(k * k, axis=-1, keepdims=True) + EPS) * gk
        q_out[h, :, 0:LANES] = (qn * scale).astype(BF16)
        q_out[h, :, LANES:HEAD_PAD] = ones_tail
        k_out[h, :, 0:LANES] = kn.astype(BF16)
        k_out[h, :, LANES:HEAD_PAD] = jnp.dot(pieces, sel_ref[h], preferred_element_type=F32).astype(BF16)


def _fox_prep(pb, f_cum, g_q, g_k, tm=512):
    s = pb.shape[0]
    h = FOX_HEADS
    w = FOX_HEADS * FOX_DH
    row = pl.BlockSpec((1, FOX_DH), lambda i: (0, 0))
    hh, pp = jnp.meshgrid(jnp.arange(h), jnp.arange(3), indexing="ij")
    sel = jnp.zeros((h, 3 * LANES, LANES), BF16).at[hh, pp * LANES + hh, pp].set(1.0)
    return pl.pallas_call(
        _fox_prep_kernel,
        grid=(s // tm,),
        in_specs=[pl.BlockSpec((tm, w), lambda i: (i, B_FQ // w)),
                  pl.BlockSpec((tm, w), lambda i: (i, B_FK // w)),
                  pl.BlockSpec((tm, LANES), lambda i: (i, 0)),
                  row, row,
                  pl.BlockSpec(sel.shape, lambda i: (0, 0, 0))],
        out_specs=[pl.BlockSpec((h, tm, HEAD_PAD), lambda i: (0, i, 0)),
                   pl.BlockSpec((h, tm, HEAD_PAD), lambda i: (0, i, 0))],
        out_shape=[jax.ShapeDtypeStruct((h, s, HEAD_PAD), BF16),
                   jax.ShapeDtypeStruct((h, s, HEAD_PAD), BF16)],
        compiler_params=_cparams(("parallel",)),
        name="fox_prep",
    )(pb, pb, f_cum, g_q, g_k, sel)


def _value_t_kernel(h_ref, w_ref, vt_out, w_sc):
    @pl.when(pl.program_id(0) == 0)
    def _():
        w_sc[...] = w_ref[0].astype(BF16)

    vt = lax.dot_general(w_sc[...], h_ref[...], (((1,), (1,)), ((), ())), preferred_element_type=F32)
    _store_values_t(vt_out, vt)


def _value_t_proj(h, w_all_t, layer, row_start, heads, tm=512):
    s, d = h.shape
    nv = heads * LANES
    return pl.pallas_call(
        _value_t_kernel,
        grid=(s // tm,),
        in_specs=[pl.BlockSpec((tm, d), lambda i: (i, 0)),
                  pl.BlockSpec((pl.Element(1), pl.Element(nv), pl.Element(d)), lambda i: (layer, row_start, 0))],
        out_specs=_vt_spec(heads, tm),
        out_shape=jax.ShapeDtypeStruct((heads, s // ATT_TILE, ATT_VROWS, ATT_TILE), BF16),
        scratch_shapes=[pltpu.VMEM((nv, d), BF16)],
        compiler_params=_cparams(("arbitrary",)),
        name="fox_value_t",
    )(h, w_all_t)


def _attn_kernel(*refs, tq, use_f):
    if use_f:
        q_ref, k_ref, vt_ref, f_ref, o_ref, m_sc, acc_sc, s_a, s_b = refs
    else:
        q_ref, k_ref, vt_ref, o_ref, m_sc, acc_sc, s_a, s_b = refs
    ns = tq // ATT_SUB
    nq = q_ref.shape[1] // tq

    def reset():
        m_sc[...] = jnp.full(m_sc.shape, NEG_INF, F32)
        acc_sc[...] = jnp.zeros(acc_sc.shape, F32)

    def logits(dst, qi, kb):
        dst[...] = lax.dot_general(k_ref[0, pl.ds(_aligned(kb * tq, tq), tq), :],
                                   q_ref[0, pl.ds(_aligned(qi * tq, tq), tq), :],
                                   (((1,), (1,)), ((), ())), preferred_element_type=F32)

    def update(src, j, qi, kb, k0, size, causal):
        s = src[k0:k0 + size, j * ATT_SUB:(j + 1) * ATT_SUB]
        if causal:
            key = lax.broadcasted_iota(jnp.int32, (size, ATT_SUB), 0)
            qry = lax.broadcasted_iota(jnp.int32, (size, ATT_SUB), 1)
            s = jnp.where(key <= qry, s, NEG_INF)
        m_old = m_sc[j]
        m_cur = jnp.max(s, axis=0, keepdims=True)
        if use_f:
            ft = f_ref[pl.ds(qi, 1), j * ATT_SUB:(j + 1) * ATT_SUB] * LOG2E
            m_new = jnp.maximum(m_old, m_cur + ft)
            shift = m_new - ft
        else:
            m_new = jnp.maximum(m_old, m_cur)
            shift = m_new
        p = jnp.exp2(s - shift).astype(BF16)
        alpha = jnp.exp2(m_old - m_new)
        acc_sc[j] = alpha * acc_sc[j] + jnp.dot(vt_ref[0, kb, :, k0:k0 + size], p, preferred_element_type=F32)
        m_sc[j] = m_new

    def full_tile(src, qi, kb):
        for j in range(ns):
            update(src, j, qi, kb, 0, tq, False)

    def diagonal_tile(src, qi):
        for j in range(ns):
            if j > 0:
                update(src, j, qi, qi, 0, j * ATT_SUB, False)
            update(src, j, qi, qi, j * ATT_SUB, ATT_SUB, True)
        for j in range(ns):
            acc = acc_sc[j]
            o_t = acc[0:LANES, :] / acc[LANES:LANES + 1, :]
            o_ref[pl.ds(_aligned(qi * tq + j * ATT_SUB, ATT_SUB), ATT_SUB), :] = o_t.T.astype(o_ref.dtype)
        reset()

    def query_tile(g, r, cur, oth):
        qi = 4 * g + r

        def four_key_tiles(u, carry):
            t = 4 * u
            for k in range(4):
                this, nxt = (cur, oth) if k % 2 == 0 else (oth, cur)
                logits(nxt, qi, t + k + 1)
                full_tile(this, qi, t + k)
            return carry

        lax.fori_loop(0, g, four_key_tiles, 0)
        for k in range(r):
            this, nxt = (cur, oth) if k % 2 == 0 else (oth, cur)
            logits(nxt, qi, 4 * g + k + 1)
            full_tile(this, qi, 4 * g + k)
        this, nxt = (cur, oth) if r % 2 == 0 else (oth, cur)
        logits(nxt, jnp.minimum(qi + 1, nq - 1), 0)
        diagonal_tile(this, qi)

    assert nq % 4 == 0
    reset()
    logits(s_a, 0, 0)

    def four_query_tiles(g, carry):
        query_tile(g, 0, s_a, s_b)
        query_tile(g, 1, s_b, s_a)
        query_tile(g, 2, s_b, s_a)
        query_tile(g, 3, s_a, s_b)
        return carry

    lax.fori_loop(0, nq // 4, four_query_tiles, 0)


def _attention(q, k, vt, f_t=None):
    hh, s, dk = q.shape
    _, nt, vrows, tq = vt.shape
    use_f = f_t is not None
    in_specs = [pl.BlockSpec((1, s, dk), lambda h: (h, 0, 0)),
                pl.BlockSpec((1, s, dk), lambda h: (h, 0, 0)),
                pl.BlockSpec((1, nt, vrows, tq), lambda h: (h, 0, 0, 0))]
    args = [q, k, vt]
    if use_f:
        in_specs.append(pl.BlockSpec((None, nt, tq), lambda h: (h, 0, 0)))
        args.append(f_t.reshape(f_t.shape[0], nt, tq))
    ns = tq // ATT_SUB
    return pl.pallas_call(
        functools.partial(_attn_kernel, tq=tq, use_f=use_f),
        grid=(hh,),
        in_specs=in_specs,
        out_specs=pl.BlockSpec((s, LANES), lambda h: (0, h)),
        out_shape=jax.ShapeDtypeStruct((s, hh * LANES), BF16),
        scratch_shapes=[pltpu.VMEM((ns, 1, ATT_SUB), F32), pltpu.VMEM((ns, vrows, ATT_SUB), F32),
                        pltpu.VMEM((tq, tq), F32), pltpu.VMEM((tq, tq), F32)],
        compiler_params=_cparams(("parallel",)),
        name="fox_attention" if use_f else "mla_attention",
    )(*args)


def _gla_kernel(q_ref, k_ref, v_ref, r_ref, misc_ref, wa_ref, ba_ref, go_ref, o_ref,
                st_ref, la_ref, b_ref, qe_ref, a_ref, u_ref, dl_ref, stb_ref):
    tb = q_ref.shape[0]
    nchunk = tb // GLA_CHUNK
    nsub = GLA_CHUNK // GLA_SUB
    C = GLA_CHUNK

    @pl.when(pl.program_id(1) == 0)
    def _():
        st_ref[...] = jnp.zeros(st_ref.shape, F32)

    m1, m2, _ = _split3(misc_ref[...])
    w1, w2, _ = _split3(wa_ref[...])
    z = (jnp.dot(m1, w1, preferred_element_type=F32) + jnp.dot(m1, w2, preferred_element_type=F32)
         + jnp.dot(m2, w1, preferred_element_type=F32)) + ba_ref[...]
    la_ref[...] = _log_sigmoid(z) * (LOG2E / GLA_TAU)

    r_i = lax.broadcasted_iota(jnp.int32, (C, C), 0)
    c_i = lax.broadcasted_iota(jnp.int32, (C, C), 1)
    tri = jnp.where(c_i <= r_i, 1.0, 0.0).astype(BF16)
    sub_row = lax.broadcasted_iota(jnp.int32, (GLA_SUB, GLA_DK), 0)
    col_sub = lax.broadcasted_iota(jnp.int32, (GLA_SUB, C), 1)
    scale = GLA_DK ** -0.5
    g_o = go_ref[...]

    chunk_rows = [slice(ci * C, (ci + 1) * C) for ci in range(nchunk)]

    for rows in chunk_rows:
        l1, l2, l3 = _split3(la_ref[rows, :])
        b_ref[rows, :] = (jnp.dot(tri, l1, preferred_element_type=F32) + jnp.dot(tri, l2, preferred_element_type=F32)
                          + jnp.dot(tri, l3, preferred_element_type=F32))

    for ci, rows in enumerate(chunk_rows):
        b = b_ref[rows, :]
        qc = q_ref[rows, :].astype(F32) * scale
        kc = k_ref[rows, :].astype(F32)
        b_last = b[C - 1:C, :]
        qe_ref[rows, :] = (qc * jnp.exp2(b)).astype(BF16)
        ke = (kc * jnp.exp2(b_last - b)).astype(BF16)
        u_ref[ci] = lax.dot_general(v_ref[rows, :], ke, (((0,), (0,)), ((), ())), preferred_element_type=F32)
        dl_ref[ci] = jnp.exp2(b_last)

        a_rows = []
        for si in range(nsub):
            lo = si * GLA_SUB
            qs, ks, bs = qc[lo:lo + GLA_SUB], kc[lo:lo + GLA_SUB], b[lo:lo + GLA_SUB]
            a = jnp.zeros((GLA_SUB, C), F32)
            for j in range(GLA_SUB):
                e = jnp.exp2(jnp.where(sub_row >= j, bs - bs[j:j + 1, :], NEG_INF))
                w = jnp.sum(qs * (ks[j:j + 1, :] * e), axis=-1, keepdims=True)
                a = jnp.where(col_sub == lo + j, w, a)
            if si > 0:
                ref_row = b[lo - 1:lo, :]
                q_i = qs * jnp.exp2(bs - ref_row)
                k_j = jnp.concatenate([kc[0:lo] * jnp.exp2(ref_row - b[0:lo]),
                                       jnp.zeros((C - lo, GLA_DK), F32)], axis=0)
                a = a + lax.dot_general(q_i.astype(BF16), k_j.astype(BF16), (((1,), (1,)), ((), ())),
                                        preferred_element_type=F32)
            a_rows.append(a)
        a_ref[rows, :] = jnp.concatenate(a_rows, axis=0).astype(BF16)

    st = st_ref[...]
    for ci in range(nchunk):
        stb_ref[ci] = st.astype(BF16)
        st = st * dl_ref[ci] + u_ref[ci]
    st_ref[...] = st

    for ci, rows in enumerate(chunk_rows):
        o = lax.dot_general(qe_ref[rows, :], stb_ref[ci], (((1,), (1,)), ((), ())), preferred_element_type=F32)
        o = o + jnp.dot(a_ref[rows, :], v_ref[rows, :], preferred_element_type=F32)
        on = o * lax.rsqrt(jnp.mean(o * o, axis=-1, keepdims=True) + EPS) * g_o
        r = r_ref[rows, :].astype(F32)
        o_ref[rows, :] = (on * (r * _sigmoid(r))).astype(o_ref.dtype)


def _gla(pb, p_small, wa_p, b_a, g_o, tb=GLA_BLOCK):
    s = pb.shape[0]
    nchunk = tb // GLA_CHUNK
    return pl.pallas_call(
        _gla_kernel,
        grid=(GLA_HEADS, s // tb),
        in_specs=[pl.BlockSpec((tb, GLA_DK), lambda h, i: (i, B_GQ // GLA_DK + h)),
                  pl.BlockSpec((tb, GLA_DK), lambda h, i: (i, B_GK // GLA_DK + h)),
                  pl.BlockSpec((tb, GLA_DV), lambda h, i: (i, B_GV // GLA_DV + h)),
                  pl.BlockSpec((tb, GLA_DV), lambda h, i: (i, B_GR // GLA_DV + h)),
                  pl.BlockSpec((tb, LANES), lambda h, i: (i, A_MISC // LANES)),
                  pl.BlockSpec((LANES, GLA_DK), lambda h, i: (0, h)),
                  pl.BlockSpec((1, GLA_DK), lambda h, i: (0, h)),
                  pl.BlockSpec((1, GLA_DV), lambda h, i: (0, 0))],
        out_specs=pl.BlockSpec((tb, GLA_DV), lambda h, i: (i, h)),
        out_shape=jax.ShapeDtypeStruct((s, GLA_HEADS * GLA_DV), BF16),
        scratch_shapes=[pltpu.VMEM((GLA_DV, GLA_DK), F32),
                        pltpu.VMEM((tb, GLA_DK), F32),
                        pltpu.VMEM((tb, GLA_DK), F32),
                        pltpu.VMEM((tb, GLA_DK), BF16),
                        pltpu.VMEM((tb, GLA_CHUNK), BF16),
                        pltpu.VMEM((nchunk, GLA_DV, GLA_DK), F32),
                        pltpu.VMEM((nchunk, 1, GLA_DK), F32),
                        pltpu.VMEM((nchunk, GLA_DV, GLA_DK), BF16)],
        compiler_params=_cparams(("parallel", "arbitrary")),
        name="gla",
    )(pb, pb, pb, pb, p_small, wa_p, b_a, g_o)


def _in_proj_kernel(a_ref, wt_ref, o_ref, w_sc, *, act):
    @pl.when(pl.program_id(1) == 0)
    def _():
        w_sc[...] = wt_ref[0].astype(BF16)

    y = lax.dot_general(a_ref[...], w_sc[...], (((1,), (1,)), ((), ())), preferred_element_type=F32)
    if act == "sigmoid":
        y = _sigmoid(y)
    o_ref[...] = y.astype(o_ref.dtype)


def _in_proj(a, w_all_t, layer, row_start, n, out_dtype, tm, tn, act=None, name="in_proj"):
    m, k = a.shape
    return pl.pallas_call(
        functools.partial(_in_proj_kernel, act=act),
        grid=(n // tn, m // tm),
        in_specs=[pl.BlockSpec((tm, k), lambda j, i: (i, 0)),
                  pl.BlockSpec((pl.Element(1), pl.Element(tn), pl.Element(k)),
                               lambda j, i: (layer, _aligned(row_start(j), F32_SUBLANES), 0))],
        out_specs=pl.BlockSpec((tm, tn), lambda j, i: (i, j)),
        out_shape=jax.ShapeDtypeStruct((m, n), out_dtype),
        scratch_shapes=[pltpu.VMEM((tn, k), BF16)],
        compiler_params=pltpu.CompilerParams(dimension_semantics=("parallel", "arbitrary"),
                                             vmem_limit_bytes=VMEM_LIMIT_BIG),
        name=name,
    )(a, w_all_t)


def _small_w_kernel(kr_ref, fl_ref, ga_ref, o_ref):
    d = o_ref.shape[1]
    z = lambda n: jnp.zeros((n, d), F32)
    o_ref[...] = jnp.concatenate([kr_ref[0], z(LANES - MLA_ROPE), fl_ref[0], ga_ref[0],
                                  z(LANES - FOX_HEADS - GLA_GATE_RANK)], axis=0).astype(BF16)


def _small_in_proj_weights(w_all_t, layer):
    d = w_all_t.shape[2]

    def rows(i):
        return pl.BlockSpec((pl.Element(1), pl.Element(IN_SPLIT[i]), pl.Element(d)),
                            lambda g: (layer, IN_OFF[i], 0))

    return pl.pallas_call(
        _small_w_kernel,
        grid=(1,),
        in_specs=[rows(2), rows(11), rows(6)],
        out_specs=pl.BlockSpec((A_SMALL, d), lambda g: (0, 0)),
        out_shape=jax.ShapeDtypeStruct((A_SMALL, d), BF16),
        compiler_params=_cparams(("arbitrary",)),
        name="small_in_proj_weights",
    )(w_all_t, w_all_t, w_all_t)


def _pad_lanes(v, n):
    return jnp.pad(v, (0, n - v.shape[0])).reshape(1, n)


def _layer(x, h, tabs, g_next, w_in_all_t, layer, g_cq, w_uq, g_ckv, w_ukv, g_mla_q, g_mla_k, w_a2, b_a, g_gla_o,
           g_fox_q, g_fox_k, b_f, w_branch_all, w_out_all, g_ffn, w_gu_all, w_down_all):
    tn = 1024
    off = IN_OFF
    p_latent = _in_proj(h, w_in_all_t, layer, lambda j: off[0], A_LATENT, F32, 1024, tn, name="in_proj_latent")
    p_small = _matmul_nt(h, _small_in_proj_weights(w_in_all_t, layer), F32, 1024, A_SMALL, name="in_proj_small")
    n_first = (off[6] - off[3]) // tn
    pb = _in_proj(h, w_in_all_t, layer,
                  lambda j: jnp.where(j < n_first, off[3] + j * tn, off[7] + (j - n_first) * tn),
                  B_WIDTH, BF16, 2048, tn, name="in_proj_b")
    gsig = _in_proj(h, w_in_all_t, layer, lambda j: off[12] + j * tn, N_BRANCH * D_MODEL, BF16, 2048, tn,
                    act="sigmoid", name="in_proj_gates")

    wuq_p = jnp.pad(w_uq.reshape(MLA_RANK, MLA_HEADS, MLA_QK),
                    ((0, 0), (0, 0), (0, HEAD_PAD - MLA_QK))).reshape(MLA_RANK, MLA_HEADS * HEAD_PAD).astype(BF16)
    w_ukv3 = w_ukv.reshape(MLA_RANK, MLA_HEADS, MLA_NOPE + MLA_V)
    wuk = w_ukv3[:, :, :MLA_NOPE].reshape(MLA_RANK, MLA_HEADS * MLA_NOPE).astype(BF16)
    wuv_t = w_ukv3[:, :, MLA_NOPE:].reshape(MLA_RANK, MLA_HEADS * MLA_V).T.astype(BF16)
    q_a, k_a, vt_a = _mla_prep(p_latent, p_small, wuq_p, wuk, wuv_t, g_cq.reshape(1, -1), g_ckv.reshape(1, -1),
                               _pad_lanes(g_mla_q, HEAD_PAD), g_mla_k[:MLA_NOPE].reshape(1, -1),
                               _pad_lanes(g_mla_k[MLA_NOPE:], LANES), tabs)
    o_a = _attention(q_a, k_a, vt_a)

    wa_p = jnp.zeros((LANES, GLA_HEADS * GLA_DK), F32).at[MISC_GA:MISC_GA + GLA_GATE_RANK].set(w_a2)
    o_b = _gla(pb, p_small, wa_p, b_a.reshape(1, -1), g_gla_o.reshape(1, -1))

    f_cum, f_cum_t = _fox_cumsum(p_small,_pad_lanes(b_f, LANES))
    q_c, k_c = _fox_prep(pb, f_cum, g_fox_q.reshape(1, -1), g_fox_k.reshape(1, -1))
    vt_c = _value_t_proj(h, w_in_all_t, layer, off[10], FOX_HEADS)
    o_c = _attention(q_c, k_c, vt_c, f_cum_t)

    x1, h2 = _merge_out(o_a, o_b, o_c, gsig, w_branch_all, w_out_all, layer, x, g_ffn)
    act = _swiglu_up(h2, w_gu_all, layer, 1024, 512)
    if g_next is None:
        return _matmul_residual(act, w_down_all, layer, x1, 256, "ffn_down"), None
    return _matmul_residual(act, w_down_all, layer, x1, 256, "ffn_down", g_next=g_next)


def kernel(x, positions, g_mix, w_in, g_cq, w_uq, g_ckv, w_ukv, g_mla_q, g_mla_k, w_a2, b_a, g_gla_o,
           g_fox_q, g_fox_k, b_f, w_branch, w_out, g_ffn, w_gu, w_down):
    bsz, s, d = x.shape
    assert bsz == 1 and d == D_MODEL and s % ATT_TILE == 0
    half = MLA_ROPE // 2
    inv = ROPE_THETA ** (-jnp.arange(half, dtype=F32) / half)
    inv128 = jnp.concatenate([inv, inv, jnp.zeros((LANES - MLA_ROPE,), F32)]).reshape(1, LANES)
    tabs = _rope_tables(positions.reshape(s, 1), inv128)
    depth = w_in.shape[0]
    w_in = jnp.swapaxes(w_in, 1, 2)
    w_branch = w_branch.reshape(depth, N_BRANCH * BRANCH_WIDTH, D_MODEL).astype(BF16)
    w_out = w_out.astype(BF16)
    w_down = w_down.astype(BF16)
    xcur = x.reshape(s, d)
    hcur = _rmsnorm(xcur, g_mix[0])
    for l in range(depth):
        g_next = g_mix[l + 1] if l + 1 < depth else None
        xcur, hcur = _layer(xcur, hcur, tabs, g_next, w_in, l, g_cq[l], w_uq[l], g_ckv[l], w_ukv[l], g_mla_q[l],
                            g_mla_k[l], w_a2[l], b_a[l], g_gla_o[l], g_fox_q[l], g_fox_k[l], b_f[l], w_branch,
                            w_out, g_ffn[l], w_gu, w_down)
    return xcur.reshape(bsz, s, d)
```

```python
import functools

import jax
import jax.numpy as jnp
from jax import lax
from jax.experimental import pallas as pl
from jax.experimental.pallas import tpu as pltpu

F32 = jnp.float32
BF16 = jnp.bfloat16

D_MODEL = 2048
BRANCH_WIDTH = D_MODEL // 2
N_BRANCH = 3
MLA_NOPE = 128
MLA_ROPE = 64
MLA_V = 128
MLA_HEADS = BRANCH_WIDTH // MLA_V
MLA_QK = MLA_NOPE + MLA_ROPE
MLA_RANK = 512
ROPE_THETA = 10000.0
GLA_HEADS = 4
GLA_DV = BRANCH_WIDTH // GLA_HEADS
GLA_DK = GLA_DV // 2
GLA_GATE_RANK = 16
GLA_TAU = 16.0
FOX_DH = 128
FOX_HEADS = BRANCH_WIDTH // FOX_DH
EPS = 1e-6
NEG_INF = -1e30

LANES = 128
F32_SUBLANES = 8
BF16_SUBLANES = 16
HEAD_PAD = 256
VMEM_LIMIT = 48 * 1024 * 1024
VMEM_LIMIT_BIG = 56 * 1024 * 1024

IN_SPLIT = (MLA_RANK, MLA_RANK, MLA_ROPE, GLA_HEADS * GLA_DK, GLA_HEADS * GLA_DK, GLA_HEADS * GLA_DV, GLA_GATE_RANK,
            GLA_HEADS * GLA_DV, FOX_HEADS * FOX_DH, FOX_HEADS * FOX_DH, FOX_HEADS * FOX_DH, FOX_HEADS,
            N_BRANCH * D_MODEL)
IN_OFF = tuple(sum(IN_SPLIT[:i]) for i in range(len(IN_SPLIT) + 1))

A_CQ, A_CKV = 0, 512
A_LATENT = 1024
A_KR, A_MISC = 0, 128
A_SMALL = 256
MISC_GA = FOX_HEADS
B_GQ, B_GK, B_GV, B_GR, B_FQ, B_FK = 0, 512, 1024, 2048, 3072, 4096
B_WIDTH = 5120

ATT_TILE = 1024
ATT_SUB = 256
ATT_VROWS = LANES + BF16_SUBLANES
LOG2E = 1.4426950408889634

GLA_CHUNK = 64
GLA_SUB = 8
GLA_BLOCK = 1024


def _aligned(start, m):
    return start if isinstance(start, int) else pl.multiple_of(start, m)


def _cparams(sem):
    return pltpu.CompilerParams(dimension_semantics=sem, vmem_limit_bytes=VMEM_LIMIT)


def _split3(x):
    x1 = x.astype(BF16)
    r1 = x - x1.astype(F32)
    x2 = r1.astype(BF16)
    x3 = (r1 - x2.astype(F32)).astype(BF16)
    return x1, x2, x3


def _sigmoid(x):
    return 0.5 * jnp.tanh(0.5 * x) + 0.5


def _log_sigmoid(z):
    return -(jnp.maximum(-z, 0.0) + jnp.log1p(jnp.exp(-jnp.abs(z))))


def _store_values_t(vt_out, vt):
    rows = vt.shape[1]
    r = lax.broadcasted_iota(jnp.int32, (BF16_SUBLANES, rows), 0)
    ones_rows = jnp.where(r == 0, 1.0, 0.0).astype(BF16)
    for h in range(vt.shape[0] // LANES):
        vt_out[h, 0, 0:LANES, :] = vt[h * LANES:(h + 1) * LANES, :].astype(BF16)
        vt_out[h, 0, LANES:ATT_VROWS, :] = ones_rows


def _rmsnorm_kernel(x_ref, g_ref, o_ref):
    x = x_ref[...]
    ms = jnp.mean(x * x, axis=-1, keepdims=True)
    o_ref[...] = (x * lax.rsqrt(ms + EPS) * g_ref[...]).astype(o_ref.dtype)


def _rmsnorm(x, g, tm=512):
    m, d = x.shape
    return pl.pallas_call(
        _rmsnorm_kernel,
        grid=(m // tm,),
        in_specs=[pl.BlockSpec((tm, d), lambda i: (i, 0)),
                  pl.BlockSpec((1, d), lambda i: (0, 0))],
        out_specs=pl.BlockSpec((tm, d), lambda i: (i, 0)),
        out_shape=jax.ShapeDtypeStruct((m, d), BF16),
        compiler_params=_cparams(("parallel",)),
        name="rmsnorm",
    )(x, g.reshape(1, d))


def _mm_nt_kernel(a_ref, wt_ref, o_ref):
    y = lax.dot_general(a_ref[...], wt_ref[...], (((1,), (1,)), ((), ())), preferred_element_type=F32)
    o_ref[...] = y.astype(o_ref.dtype)


def _matmul_nt(a, w_t, out_dtype, tm, tn, name):
    m, k = a.shape
    n = w_t.shape[0]
    return pl.pallas_call(
        _mm_nt_kernel,
        grid=(m // tm, n // tn),
        in_specs=[pl.BlockSpec((tm, k), lambda i, j: (i, 0)),
                  pl.BlockSpec((tn, k), lambda i, j: (j, 0))],
        out_specs=pl.BlockSpec((tm, tn), lambda i, j: (i, j)),
        out_shape=jax.ShapeDtypeStruct((m, n), out_dtype),
        compiler_params=_cparams(("parallel", "arbitrary")),
        name=name,
    )(a, w_t)


def _mm_res_kernel(*refs, with_norm):
    if with_norm:
        a_ref, w_ref, r_ref, g_ref, o_ref, h_ref = refs
    else:
        a_ref, w_ref, r_ref, o_ref = refs
    y = r_ref[...] + jnp.dot(a_ref[...], w_ref[...], preferred_element_type=F32)
    o_ref[...] = y
    if with_norm:
        ms = jnp.mean(y * y, axis=-1, keepdims=True)
        h_ref[...] = (y * lax.rsqrt(ms + EPS) * g_ref[...]).astype(h_ref.dtype)


def _matmul_residual(a, w_all, layer, res, tm, name, g_next=None):
    m, k = a.shape
    n = w_all.shape[2]
    with_norm = g_next is not None
    row = pl.BlockSpec((tm, n), lambda i: (i, 0))
    in_specs = [pl.BlockSpec((tm, k), lambda i: (i, 0)),
                pl.BlockSpec((None, k, n), lambda i: (layer, 0, 0), pipeline_mode=pl.Buffered(1)),
                row]
    args = [a, w_all, res]
    out_specs, out_shape = row, jax.ShapeDtypeStruct((m, n), F32)
    if with_norm:
        in_specs.append(pl.BlockSpec((1, n), lambda i: (0, 0)))
        args.append(g_next.reshape(1, n))
        out_specs, out_shape = [row, row], [out_shape, jax.ShapeDtypeStruct((m, n), BF16)]
    return pl.pallas_call(
        functools.partial(_mm_res_kernel, with_norm=with_norm),
        grid=(m // tm,),
        in_specs=in_specs,
        out_specs=out_specs,
        out_shape=out_shape,
        compiler_params=pltpu.CompilerParams(dimension_semantics=("parallel",), vmem_limit_bytes=VMEM_LIMIT_BIG),
        name=name,
    )(*args)


def _swiglu_kernel(a_ref, wg_ref, wu_ref, o_ref, wg_sc, wu_sc):
    @pl.when(pl.program_id(1) == 0)
    def _():
        wg_sc[...] = wg_ref[...].astype(BF16)
        wu_sc[...] = wu_ref[...].astype(BF16)

    a = a_ref[...]
    g = jnp.dot(a, wg_sc[...], preferred_element_type=F32)
    u = jnp.dot(a, wu_sc[...], preferred_element_type=F32)
    o_ref[...] = (g * _sigmoid(g) * u).astype(o_ref.dtype)


def _swiglu_up(a, w_gu_all, layer, tm, tn):
    m, k = a.shape
    hidden = w_gu_all.shape[2] // 2
    nj = hidden // tn
    return pl.pallas_call(
        _swiglu_kernel,
        grid=(nj, m // tm),
        in_specs=[pl.BlockSpec((tm, k), lambda j, i: (i, 0)),
                  pl.BlockSpec((None, k, tn), lambda j, i: (layer, 0, j)),
                  pl.BlockSpec((None, k, tn), lambda j, i: (layer, 0, j + nj))],
        out_specs=pl.BlockSpec((tm, tn), lambda j, i: (i, j)),
        out_shape=jax.ShapeDtypeStruct((m, hidden), BF16),
        scratch_shapes=[pltpu.VMEM((k, tn), BF16), pltpu.VMEM((k, tn), BF16)],
        compiler_params=_cparams(("parallel", "arbitrary")),
        name="ffn_up",
    )(a, w_gu_all, w_gu_all)


def _merge_out_kernel(oa_ref, ob_ref, oc_ref, gs_ref, wb_ref, wo_ref, x_ref, g_ref, x1_ref, h_ref):
    bw = oa_ref.shape[1]
    d = x_ref.shape[1]
    merged = None
    for n, o_ref in enumerate((oa_ref, ob_ref, oc_ref)):
        y = jnp.dot(o_ref[...], wb_ref[n * bw:(n + 1) * bw, :], preferred_element_type=F32)
        y = y * gs_ref[:, n * d:(n + 1) * d].astype(F32)
        merged = y if merged is None else merged + y
    x1 = x_ref[...] + jnp.dot(merged.astype(BF16), wo_ref[...], preferred_element_type=F32)
    x1_ref[...] = x1
    ms = jnp.mean(x1 * x1, axis=-1, keepdims=True)
    h_ref[...] = (x1 * lax.rsqrt(ms + EPS) * g_ref[...]).astype(h_ref.dtype)


def _merge_out(o_a, o_b, o_c, gsig, w_branch_all, w_out_all, layer, x, g_next, tm=256):
    m, bw = o_a.shape
    d = x.shape[1]
    o_spec = pl.BlockSpec((tm, bw), lambda i: (i, 0))
    row = pl.BlockSpec((tm, d), lambda i: (i, 0))
    return pl.pallas_call(
        _merge_out_kernel,
        grid=(m // tm,),
        in_specs=[o_spec, o_spec, o_spec,
                  pl.BlockSpec((tm, N_BRANCH * d), lambda i: (i, 0)),
                  pl.BlockSpec((None, N_BRANCH * bw, d), lambda i: (layer, 0, 0), pipeline_mode=pl.Buffered(1)),
                  pl.BlockSpec((None, d, d), lambda i: (layer, 0, 0), pipeline_mode=pl.Buffered(1)),
                  row,
                  pl.BlockSpec((1, d), lambda i: (0, 0))],
        out_specs=[row, row],
        out_shape=[jax.ShapeDtypeStruct((m, d), F32), jax.ShapeDtypeStruct((m, d), BF16)],
        compiler_params=pltpu.CompilerParams(dimension_semantics=("parallel",), vmem_limit_bytes=VMEM_LIMIT_BIG),
        name="merge_out_proj",
    )(o_a, o_b, o_c, gsig, w_branch_all, w_out_all, x, g_next.reshape(1, d))


def _rope_table_kernel(pos_ref, inv_ref, c_ref, s1_ref, s2_ref):
    ang = pos_ref[...].astype(F32) * inv_ref[...]
    c = jnp.cos(ang)
    s = jnp.sin(ang)
    lane = lax.broadcasted_iota(jnp.int32, ang.shape, 1)
    half = MLA_ROPE // 2
    c_ref[...] = jnp.where(lane < MLA_ROPE, c, 0.0)
    s1_ref[...] = jnp.where((lane >= half) & (lane < MLA_ROPE), s, 0.0)
    s2_ref[...] = jnp.where(lane < half, -s, 0.0)


def _rope_tables(pos_col, inv128, tm=512):
    s = pos_col.shape[0]
    spec = pl.BlockSpec((tm, LANES), lambda i: (i, 0))
    shp = jax.ShapeDtypeStruct((s, LANES), F32)
    return pl.pallas_call(
        _rope_table_kernel,
        grid=(s // tm,),
        in_specs=[pl.BlockSpec((tm, 1), lambda i: (i, 0)),
                  pl.BlockSpec((1, LANES), lambda i: (0, 0))],
        out_specs=[spec, spec, spec],
        out_shape=[shp, shp, shp],
        compiler_params=_cparams(("parallel",)),
        name="rope_tables",
    )(pos_col, inv128)


def _rope128(x, c, s1, s2):
    half = MLA_ROPE // 2
    return x * c + pltpu.roll(x, half, 1) * s1 + pltpu.roll(x, LANES - half, 1) * s2


def _mla_prep_kernel(cq_ref, ckv_ref, kr_ref, wuq_ref, wuk_ref, wuvt_ref, gcq_ref, gckv_ref, gq_ref, gk_ref, gkr_ref,
                     c_ref, s1_ref, s2_ref, q_out, k_out, vt_out):
    def norm(x, g):
        return x * lax.rsqrt(jnp.mean(x * x, axis=-1, keepdims=True) + EPS) * g

    cqn = norm(cq_ref[...], gcq_ref[...]).astype(BF16)
    ckvn = norm(ckv_ref[...], gckv_ref[...]).astype(BF16)
    qa = jnp.dot(cqn, wuq_ref[...], preferred_element_type=F32)
    ka = jnp.dot(ckvn, wuk_ref[...], preferred_element_type=F32)
    _store_values_t(vt_out, lax.dot_general(wuvt_ref[...], ckvn, (((1,), (1,)), ((), ())),
                                            preferred_element_type=F32))
    krp = kr_ref[...]
    kr_ss = jnp.sum(krp * krp, axis=-1, keepdims=True)
    c, s1, s2 = c_ref[...], s1_ref[...], s2_ref[...]
    gq, gk, gkr = gq_ref[...], gk_ref[...], gkr_ref[...]
    scale = MLA_QK ** -0.5 * LOG2E
    kr_roped = _rope128(krp * gkr, c, s1, s2)
    for h in range(MLA_HEADS):
        qh = qa[:, h * HEAD_PAD:(h + 1) * HEAD_PAD]
        rq = lax.rsqrt(jnp.sum(qh * qh, axis=-1, keepdims=True) * (1.0 / MLA_QK) + EPS)
        qn = qh * rq * gq
        q_out[h, :, 0:LANES] = (qn[:, 0:LANES] * scale).astype(BF16)
        q_out[h, :, LANES:HEAD_PAD] = (_rope128(qn[:, LANES:HEAD_PAD], c, s1, s2) * scale).astype(BF16)
        kn = ka[:, h * MLA_NOPE:(h + 1) * MLA_NOPE]
        rk = lax.rsqrt((jnp.sum(kn * kn, axis=-1, keepdims=True) + kr_ss) * (1.0 / MLA_QK) + EPS)
        k_out[h, :, 0:LANES] = (kn * rk * gk).astype(BF16)
        k_out[h, :, LANES:HEAD_PAD] = (kr_roped * rk).astype(BF16)


def _vt_spec(heads, tm):
    per_tile = ATT_TILE // tm
    return pl.BlockSpec((heads, 1, ATT_VROWS, tm), lambda i: (0, i // per_tile, 0, i % per_tile))


def _mla_prep(p_latent, p_small, wuq_p, wuk, wuv_t, g_cq, g_ckv, g_q, g_k, g_kr, tabs, tm=1024):
    s = p_latent.shape[0]
    h = MLA_HEADS
    row = lambda w: pl.BlockSpec((1, w), lambda i: (0, 0))
    tab = pl.BlockSpec((tm, LANES), lambda i: (i, 0))
    return pl.pallas_call(
        _mla_prep_kernel,
        grid=(s // tm,),
        in_specs=[pl.BlockSpec((tm, MLA_RANK), lambda i: (i, A_CQ // MLA_RANK)),
                  pl.BlockSpec((tm, MLA_RANK), lambda i: (i, A_CKV // MLA_RANK)),
                  pl.BlockSpec((tm, LANES), lambda i: (i, A_KR // LANES)),
                  pl.BlockSpec(wuq_p.shape, lambda i: (0, 0)),
                  pl.BlockSpec(wuk.shape, lambda i: (0, 0)),
                  pl.BlockSpec(wuv_t.shape, lambda i: (0, 0)),
                  row(MLA_RANK), row(MLA_RANK), row(HEAD_PAD), row(LANES), row(LANES),
                  tab, tab, tab],
        out_specs=[pl.BlockSpec((h, tm, HEAD_PAD), lambda i: (0, i, 0)),
                   pl.BlockSpec((h, tm, HEAD_PAD), lambda i: (0, i, 0)),
                   _vt_spec(h, tm)],
        out_shape=[jax.ShapeDtypeStruct((h, s, HEAD_PAD), BF16),
                   jax.ShapeDtypeStruct((h, s, HEAD_PAD), BF16),
                   jax.ShapeDtypeStruct((h, s // ATT_TILE, ATT_VROWS, ATT_TILE), BF16)],
        compiler_params=_cparams(("parallel",)),
        name="mla_prep",
    )(p_latent, p_latent, p_small, wuq_p, wuk, wuv_t, g_cq, g_ckv, g_q, g_k, g_kr, *tabs)


def _fox_cumsum_kernel(misc_ref, bf_ref, f_ref, ft_ref, carry_ref):
    blk = misc_ref.shape[0]

    @pl.when(pl.program_id(0) == 0)
    def _():
        carry_ref[...] = jnp.zeros(carry_ref.shape, F32)

    r = lax.broadcasted_iota(jnp.int32, (blk, blk), 0)
    c = lax.broadcasted_iota(jnp.int32, (blk, blk), 1)
    tri = jnp.where(c <= r, 1.0, 0.0).astype(BF16)
    x1, x2, x3 = _split3(_log_sigmoid(misc_ref[...] + bf_ref[...]))
    cs = (jnp.dot(tri, x1, preferred_element_type=F32)
          + jnp.dot(tri, x2, preferred_element_type=F32)
          + jnp.dot(tri, x3, preferred_element_type=F32)) + carry_ref[...]
    f_ref[...] = cs
    ft_ref[...] = cs.T
    carry_ref[...] = cs[blk - 1:blk, :]


def _fox_cumsum(p_small, bf128, blk=512):
    s = p_small.shape[0]
    return pl.pallas_call(
        _fox_cumsum_kernel,
        grid=(s // blk,),
        in_specs=[pl.BlockSpec((blk, LANES), lambda i: (i, A_MISC // LANES)),
                  pl.BlockSpec((1, LANES), lambda i: (0, 0))],
        out_specs=[pl.BlockSpec((blk, LANES), lambda i: (i, 0)),
                   pl.BlockSpec((LANES, blk), lambda i: (0, i))],
        out_shape=[jax.ShapeDtypeStruct((s, LANES), F32), jax.ShapeDtypeStruct((LANES, s), F32)],
        scratch_shapes=[pltpu.VMEM((1, LANES), F32)],
        compiler_params=_cparams(("arbitrary",)),
        name="fox_cumsum",
    )(p_small, bf128)


def _fox_prep_kernel(fq_ref, fk_ref, f_ref, gq_ref, gk_ref, sel_ref, q_out, k_out):
    tm = fq_ref.shape[0]
    lane = lax.broadcasted_iota(jnp.int32, (tm, LANES), 1)
    ones_tail = jnp.where(lane < 3, 1.0, 0.0).astype(BF16)
    pieces = jnp.concatenate(_split3(f_ref[...] * (-LOG2E)), axis=1)
    scale = FOX_DH ** -0.5 * LOG2E
    gq, gk = gq_ref[...], gk_ref[...]
    for h in range(FOX_HEADS):
        sl = slice(h * FOX_DH, (h + 1) * FOX_DH)
        q = fq_ref[:, sl].astype(F32)
        k = fk_ref[:, sl].astype(F32)
        qn = q * lax.rsqrt(jnp.mean(q * q, axis=-1, keepdims=True) + EPS) * gq
        kn = k * lax.rsqrt(jnp.mean(k * k, axis=-1, keepdims=True) + EPS) * gk
        q_out[h, :, 0:LANES] = (qn * scale).astype(BF16)
        q_out[h, :, LANES:HEAD_PAD] = ones_tail
        k_out[h, :, 0:LANES] = kn.astype(BF16)
        k_out[h, :, LANES:HEAD_PAD] = jnp.dot(pieces, sel_ref[h], preferred_element_type=F32).astype(BF16)


def _fox_prep(pb, f_cum, g_q, g_k, tm=512):
    s = pb.shape[0]
    h = FOX_HEADS
    w = FOX_HEADS * FOX_DH
    row = pl.BlockSpec((1, FOX_DH), lambda i: (0, 0))
    hh, pp = jnp.meshgrid(jnp.arange(h), jnp.arange(3), indexing="ij")
    sel = jnp.zeros((h, 3 * LANES, LANES), BF16).at[hh, pp * LANES + hh, pp].set(1.0)
    return pl.pallas_call(
        _fox_prep_kernel,
        grid=(s // tm,),
        in_specs=[pl.BlockSpec((tm, w), lambda i: (i, B_FQ // w)),
                  pl.BlockSpec((tm, w), lambda i: (i, B_FK // w)),
                  pl.BlockSpec((tm, LANES), lambda i: (i, 0)),
                  row, row,
                  pl.BlockSpec(sel.shape, lambda i: (0, 0, 0))],
        out_specs=[pl.BlockSpec((h, tm, HEAD_PAD), lambda i: (0, i, 0)),
                   pl.BlockSpec((h, tm, HEAD_PAD), lambda i: (0, i, 0))],
        out_shape=[jax.ShapeDtypeStruct((h, s, HEAD_PAD), BF16),
                   jax.ShapeDtypeStruct((h, s, HEAD_PAD), BF16)],
        compiler_params=_cparams(("parallel",)),
        name="fox_prep",
    )(pb, pb, f_cum, g_q, g_k, sel)


def _value_t_kernel(h_ref, w_ref, vt_out, w_sc):
    @pl.when(pl.program_id(0) == 0)
    def _():
        w_sc[...] = w_ref[0].astype(BF16)

    vt = lax.dot_general(w_sc[...], h_ref[...], (((1,), (1,)), ((), ())), preferred_element_type=F32)
    _store_values_t(vt_out, vt)


def _value_t_proj(h, w_all_t, layer, row_start, heads, tm=1024):
    s, d = h.shape
    nv = heads * LANES
    return pl.pallas_call(
        _value_t_kernel,
        grid=(s // tm,),
        in_specs=[pl.BlockSpec((tm, d), lambda i: (i, 0)),
                  pl.BlockSpec((pl.Element(1), pl.Element(nv), pl.Element(d)), lambda i: (layer, row_start, 0))],
        out_specs=_vt_spec(heads, tm),
        out_shape=jax.ShapeDtypeStruct((heads, s // ATT_TILE, ATT_VROWS, ATT_TILE), BF16),
        scratch_shapes=[pltpu.VMEM((nv, d), BF16)],
        compiler_params=_cparams(("arbitrary",)),
        name="fox_value_t",
    )(h, w_all_t)


def _attn_kernel(*refs, tq, use_f):
    if use_f:
        q_ref, k_ref, vt_ref, f_ref, o_ref, m_sc, acc_sc, s_a, s_b = refs
    else:
        q_ref, k_ref, vt_ref, o_ref, m_sc, acc_sc, s_a, s_b = refs
    ns = tq // ATT_SUB
    nq = q_ref.shape[1] // tq

    def reset():
        m_sc[...] = jnp.full(m_sc.shape, NEG_INF, F32)
        acc_sc[...] = jnp.zeros(acc_sc.shape, F32)

    def logits(dst, qi, kb):
        dst[...] = lax.dot_general(k_ref[0, pl.ds(_aligned(kb * tq, tq), tq), :],
                                   q_ref[0, pl.ds(_aligned(qi * tq, tq), tq), :],
                                   (((1,), (1,)), ((), ())), preferred_element_type=F32)

    def update(src, j, qi, kb, k0, size, causal):
        s = src[k0:k0 + size, j * ATT_SUB:(j + 1) * ATT_SUB]
        if causal:
            key = lax.broadcasted_iota(jnp.int32, (size, ATT_SUB), 0)
            qry = lax.broadcasted_iota(jnp.int32, (size, ATT_SUB), 1)
            s = jnp.where(key <= qry, s, NEG_INF)
        m_old = m_sc[j]
        m_cur = jnp.max(s, axis=0, keepdims=True)
        if use_f:
            ft = f_ref[pl.ds(qi, 1), j * ATT_SUB:(j + 1) * ATT_SUB] * LOG2E
            m_new = jnp.maximum(m_old, m_cur + ft)
            shift = m_new - ft
        else:
            m_new = jnp.maximum(m_old, m_cur)
            shift = m_new
        p = jnp.exp2(s - shift).astype(BF16)
        alpha = jnp.exp2(m_old - m_new)
        acc_sc[j] = alpha * acc_sc[j] + jnp.dot(vt_ref[0, kb, :, k0:k0 + size], p, preferred_element_type=F32)
        m_sc[j] = m_new

    def full_tile(src, qi, kb):
        for j in range(ns):
            update(src, j, qi, kb, 0, tq, False)

    def diagonal_tile(src, qi):
        for j in range(ns):
            if j > 0:
                update(src, j, qi, qi, 0, j * ATT_SUB, False)
            update(src, j, qi, qi, j * ATT_SUB, ATT_SUB, True)
        for j in range(ns):
            acc = acc_sc[j]
            o_t = acc[0:LANES, :] / acc[LANES:LANES + 1, :]
            o_ref[pl.ds(_aligned(qi * tq + j * ATT_SUB, ATT_SUB), ATT_SUB), :] = o_t.T.astype(o_ref.dtype)
        reset()

    def query_tile(g, r, cur, oth):
        qi = 4 * g + r

        def four_key_tiles(u, carry):
            t = 4 * u
            for k in range(4):
                this, nxt = (cur, oth) if k % 2 == 0 else (oth, cur)
                logits(nxt, qi, t + k + 1)
                full_tile(this, qi, t + k)
            return carry

        lax.fori_loop(0, g, four_key_tiles, 0)
        for k in range(r):
            this, nxt = (cur, oth) if k % 2 == 0 else (oth, cur)
            logits(nxt, qi, 4 * g + k + 1)
            full_tile(this, qi, 4 * g + k)
        this, nxt = (cur, oth) if r % 2 == 0 else (oth, cur)
        logits(nxt, jnp.minimum(qi + 1, nq - 1), 0)
        diagonal_tile(this, qi)

    assert nq % 4 == 0
    reset()
    logits(s_a, 0, 0)

    def four_query_tiles(g, carry):
        query_tile(g, 0, s_a, s_b)
        query_tile(g, 1, s_b, s_a)
        query_tile(g, 2, s_b, s_a)
        query_tile(g, 3, s_a, s_b)
        return carry

    lax.fori_loop(0, nq // 4, four_query_tiles, 0)


def _attention(q, k, vt, f_t=None):
    hh, s, dk = q.shape
    _, nt, vrows, tq = vt.shape
    use_f = f_t is not None
    in_specs = [pl.BlockSpec((1, s, dk), lambda h: (h, 0, 0)),
                pl.BlockSpec((1, s, dk), lambda h: (h, 0, 0)),
                pl.BlockSpec((1, nt, vrows, tq), lambda h: (h, 0, 0, 0))]
    args = [q, k, vt]
    if use_f:
        in_specs.append(pl.BlockSpec((None, nt, tq), lambda h: (h, 0, 0)))
        args.append(f_t.reshape(f_t.shape[0], nt, tq))
    ns = tq // ATT_SUB
    return pl.pallas_call(
        functools.partial(_attn_kernel, tq=tq, use_f=use_f),
        grid=(hh,),
        in_specs=in_specs,
        out_specs=pl.BlockSpec((s, LANES), lambda h: (0, h)),
        out_shape=jax.ShapeDtypeStruct((s, hh * LANES), BF16),
        scratch_shapes=[pltpu.VMEM((ns, 1, ATT_SUB), F32), pltpu.VMEM((ns, vrows, ATT_SUB), F32),
                        pltpu.VMEM((tq, tq), F32), pltpu.VMEM((tq, tq), F32)],
        compiler_params=_cparams(("parallel",)),
        name="fox_attention" if use_f else "mla_attention",
    )(*args)


def _gla_kernel(q_ref, k_ref, v_ref, r_ref, misc_ref, wa_ref, ba_ref, go_ref, o_ref,
                st_ref, la_ref, b_ref, qe_ref, a_ref, u_ref, dl_ref, stb_ref):
    tb = q_ref.shape[0]
    nchunk = tb // GLA_CHUNK
    nsub = GLA_CHUNK // GLA_SUB
    C = GLA_CHUNK

    @pl.when(pl.program_id(1) == 0)
    def _():
        st_ref[...] = jnp.zeros(st_ref.shape, F32)

    m1, m2, _ = _split3(misc_ref[...])
    w1, w2, _ = _split3(wa_ref[...])
    z = (jnp.dot(m1, w1, preferred_element_type=F32) + jnp.dot(m1, w2, preferred_element_type=F32)
         + jnp.dot(m2, w1, preferred_element_type=F32)) + ba_ref[...]
    la_ref[...] = _log_sigmoid(z) * (LOG2E / GLA_TAU)

    r_i = lax.broadcasted_iota(jnp.int32, (C, C), 0)
    c_i = lax.broadcasted_iota(jnp.int32, (C, C), 1)
    tri = jnp.where(c_i <= r_i, 1.0, 0.0).astype(BF16)
    sub_row = lax.broadcasted_iota(jnp.int32, (GLA_SUB, GLA_DK), 0)
    col_sub = lax.broadcasted_iota(jnp.int32, (GLA_SUB, C), 1)
    scale = GLA_DK ** -0.5
    g_o = go_ref[...]

    chunk_rows = [slice(ci * C, (ci + 1) * C) for ci in range(nchunk)]

    for rows in chunk_rows:
        l1, l2, l3 = _split3(la_ref[rows, :])
        b_ref[rows, :] = (jnp.dot(tri, l1, preferred_element_type=F32) + jnp.dot(tri, l2, preferred_element_type=F32)
                          + jnp.dot(tri, l3, preferred_element_type=F32))

    for ci, rows in enumerate(chunk_rows):
        b = b_ref[rows, :]
        qc = q_ref[rows, :].astype(F32) * scale
        kc = k_ref[rows, :].astype(F32)
        b_last = b[C - 1:C, :]
        qe_ref[rows, :] = (qc * jnp.exp2(b)).astype(BF16)
        ke = (kc * jnp.exp2(b_last - b)).astype(BF16)
        u_ref[ci] = lax.dot_general(v_ref[rows, :], ke, (((0,), (0,)), ((), ())), preferred_element_type=F32)
        dl_ref[ci] = jnp.exp2(b_last)

        a_rows = []
        for si in range(nsub):
            lo = si * GLA_SUB
            qs, ks, bs = qc[lo:lo + GLA_SUB], kc[lo:lo + GLA_SUB], b[lo:lo + GLA_SUB]
            a = jnp.zeros((GLA_SUB, C), F32)
            for j in range(GLA_SUB):
                e = jnp.exp2(jnp.where(sub_row >= j, bs - bs[j:j + 1, :], NEG_INF))
                w = jnp.sum(qs * (ks[j:j + 1, :] * e), axis=-1, keepdims=True)
                a = jnp.where(col_sub == lo + j, w, a)
            if si > 0:
                ref_row = b[lo - 1:lo, :]
                q_i = qs * jnp.exp2(bs - ref_row)
                k_j = jnp.concatenate([kc[0:lo] * jnp.exp2(ref_row - b[0:lo]),
                                       jnp.zeros((C - lo, GLA_DK), F32)], axis=0)
                a = a + lax.dot_general(q_i.astype(BF16), k_j.astype(BF16), (((1,), (1,)), ((), ())),
                                        preferred_element_type=F32)
            a_rows.append(a)
        a_ref[rows, :] = jnp.concatenate(a_rows, axis=0).astype(BF16)

    st = st_ref[...]
    for ci in range(nchunk):
        stb_ref[ci] = st.astype(BF16)
        st = st * dl_ref[ci] + u_ref[ci]
    st_ref[...] = st

    for ci, rows in enumerate(chunk_rows):
        o = lax.dot_general(qe_ref[rows, :], stb_ref[ci], (((1,), (1,)), ((), ())), preferred_element_type=F32)
        o = o + jnp.dot(a_ref[rows, :], v_ref[rows, :], preferred_element_type=F32)
        on = o * lax.rsqrt(jnp.mean(o * o, axis=-1, keepdims=True) + EPS) * g_o
        r = r_ref[rows, :].astype(F32)
        o_ref[rows, :] = (on * (r * _sigmoid(r))).astype(o_ref.dtype)


def _gla(pb, p_small, wa_p, b_a, g_o, tb=GLA_BLOCK):
    s = pb.shape[0]
    nchunk = tb // GLA_CHUNK
    return pl.pallas_call(
        _gla_kernel,
        grid=(GLA_HEADS, s // tb),
        in_specs=[pl.BlockSpec((tb, GLA_DK), lambda h, i: (i, B_GQ // GLA_DK + h)),
                  pl.BlockSpec((tb, GLA_DK), lambda h, i: (i, B_GK // GLA_DK + h)),
                  pl.BlockSpec((tb, GLA_DV), lambda h, i: (i, B_GV // GLA_DV + h)),
                  pl.BlockSpec((tb, GLA_DV), lambda h, i: (i, B_GR // GLA_DV + h)),
                  pl.BlockSpec((tb, LANES), lambda h, i: (i, A_MISC // LANES)),
                  pl.BlockSpec((LANES, GLA_DK), lambda h, i: (0, h)),
                  pl.BlockSpec((1, GLA_DK), lambda h, i: (0, h)),
                  pl.BlockSpec((1, GLA_DV), lambda h, i: (0, 0))],
        out_specs=pl.BlockSpec((tb, GLA_DV), lambda h, i: (i, h)),
        out_shape=jax.ShapeDtypeStruct((s, GLA_HEADS * GLA_DV), BF16),
        scratch_shapes=[pltpu.VMEM((GLA_DV, GLA_DK), F32),
                        pltpu.VMEM((tb, GLA_DK), F32),
                        pltpu.VMEM((tb, GLA_DK), F32),
                        pltpu.VMEM((tb, GLA_DK), BF16),
                        pltpu.VMEM((tb, GLA_CHUNK), BF16),
                        pltpu.VMEM((nchunk, GLA_DV, GLA_DK), F32),
                        pltpu.VMEM((nchunk, 1, GLA_DK), F32),
                        pltpu.VMEM((nchunk, GLA_DV, GLA_DK), BF16)],
        compiler_params=_cparams(("parallel", "arbitrary")),
        name="gla",
    )(pb, pb, pb, pb, p_small, wa_p, b_a, g_o)


def _in_proj_kernel(a_ref, wt_ref, o_ref, w_sc, *, act):
    @pl.when(pl.program_id(1) == 0)
    def _():
        w_sc[...] = wt_ref[0].astype(BF16)

    y = lax.dot_general(a_ref[...], w_sc[...], (((1,), (1,)), ((), ())), preferred_element_type=F32)
    if act == "sigmoid":
        y = _sigmoid(y)
    o_ref[...] = y.astype(o_ref.dtype)


def _in_proj(a, w_all_t, layer, row_start, n, out_dtype, tm, tn, act=None, name="in_proj"):
    m, k = a.shape
    return pl.pallas_call(
        functools.partial(_in_proj_kernel, act=act),
        grid=(n // tn, m // tm),
        in_specs=[pl.BlockSpec((tm, k), lambda j, i: (i, 0)),
                  pl.BlockSpec((pl.Element(1), pl.Element(tn), pl.Element(k)),
                               lambda j, i: (layer, _aligned(row_start(j), F32_SUBLANES), 0))],
        out_specs=pl.BlockSpec((tm, tn), lambda j, i: (i, j)),
        out_shape=jax.ShapeDtypeStruct((m, n), out_dtype),
        scratch_shapes=[pltpu.VMEM((tn, k), BF16)],
        compiler_params=pltpu.CompilerParams(dimension_semantics=("parallel", "arbitrary"),
                                             vmem_limit_bytes=VMEM_LIMIT_BIG),
        name=name,
    )(a, w_all_t)


def _small_w_kernel(kr_ref, fl_ref, ga_ref, o_ref):
    d = o_ref.shape[1]
    z = lambda n: jnp.zeros((n, d), F32)
    o_ref[...] = jnp.concatenate([kr_ref[0], z(LANES - MLA_ROPE), fl_ref[0], ga_ref[0],
                                  z(LANES - FOX_HEADS - GLA_GATE_RANK)], axis=0).astype(BF16)


def _small_in_proj_weights(w_all_t, layer):
    d = w_all_t.shape[2]

    def rows(i):
        return pl.BlockSpec((pl.Element(1), pl.Element(IN_SPLIT[i]), pl.Element(d)),
                            lambda g: (layer, IN_OFF[i], 0))

    return pl.pallas_call(
        _small_w_kernel,
        grid=(1,),
        in_specs=[rows(2), rows(11), rows(6)],
        out_specs=pl.BlockSpec((A_SMALL, d), lambda g: (0, 0)),
        out_shape=jax.ShapeDtypeStruct((A_SMALL, d), BF16),
        compiler_params=_cparams(("arbitrary",)),
        name="small_in_proj_weights",
    )(w_all_t, w_all_t, w_all_t)


def _pad_lanes(v, n):
    return jnp.pad(v, (0, n - v.shape[0])).reshape(1, n)


def _layer(x, h, tabs, g_next, w_in_all_t, layer, g_cq, w_uq, g_ckv, w_ukv, g_mla_q, g_mla_k, w_a2, b_a, g_gla_o,
           g_fox_q, g_fox_k, b_f, w_branch_all, w_out_all, g_ffn, w_gu_all, w_down_all):
    tn = 1024
    off = IN_OFF
    p_latent = _in_proj(h, w_in_all_t, layer, lambda j: off[0], A_LATENT, F32, 1024, tn, name="in_proj_latent")
    p_small = _matmul_nt(h, _small_in_proj_weights(w_in_all_t, layer), F32, 1024, A_SMALL, name="in_proj_small")
    n_first = (off[6] - off[3]) // tn
    pb = _in_proj(h, w_in_all_t, layer,
                  lambda j: jnp.where(j < n_first, off[3] + j * tn, off[7] + (j - n_first) * tn),
                  B_WIDTH, BF16, 2048, tn, name="in_proj_b")
    gsig = _in_proj(h, w_in_all_t, layer, lambda j: off[12] + j * tn, N_BRANCH * D_MODEL, BF16, 2048, tn,
                    act="sigmoid", name="in_proj_gates")

    wuq_p = jnp.pad(w_uq.reshape(MLA_RANK, MLA_HEADS, MLA_QK),
                    ((0, 0), (0, 0), (0, HEAD_PAD - MLA_QK))).reshape(MLA_RANK, MLA_HEADS * HEAD_PAD).astype(BF16)
    w_ukv3 = w_ukv.reshape(MLA_RANK, MLA_HEADS, MLA_NOPE + MLA_V)
    wuk = w_ukv3[:, :, :MLA_NOPE].reshape(MLA_RANK, MLA_HEADS * MLA_NOPE).astype(BF16)
    wuv_t = w_ukv3[:, :, MLA_NOPE:].reshape(MLA_RANK, MLA_HEADS * MLA_V).T.astype(BF16)
    q_a, k_a, vt_a = _mla_prep(p_latent, p_small, wuq_p, wuk, wuv_t, g_cq.reshape(1, -1), g_ckv.reshape(1, -1),
                               _pad_lanes(g_mla_q, HEAD_PAD), g_mla_k[:MLA_NOPE].reshape(1, -1),
                               _pad_lanes(g_mla_k[MLA_NOPE:], LANES), tabs)
    o_a = _attention(q_a, k_a, vt_a)

    wa_p = jnp.zeros((LANES, GLA_HEADS * GLA_DK), F32).at[MISC_GA:MISC_GA + GLA_GATE_RANK].set(w_a2)
    o_b = _gla(pb, p_small, wa_p, b_a.reshape(1, -1), g_gla_o.reshape(1, -1))

    f_cum, f_cum_t = _fox_cumsum(p_small,_pad_lanes(b_f, LANES))
    q_c, k_c = _fox_prep(pb, f_cum, g_fox_q.reshape(1, -1), g_fox_k.reshape(1, -1))
    vt_c = _value_t_proj(h, w_in_all_t, layer, off[10], FOX_HEADS)
    o_c = _attention(q_c, k_c, vt_c, f_cum_t)

    x1, h2 = _merge_out(o_a, o_b, o_c, gsig, w_branch_all, w_out_all, layer, x, g_ffn)
    act = _swiglu_up(h2, w_gu_all, layer, 1024, 512)
    if g_next is None:
        return _matmul_residual(act, w_down_all, layer, x1, 256, "ffn_down"), None
    return _matmul_residual(act, w_down_all, layer, x1, 256, "ffn_down", g_next=g_next)


def kernel(x, positions, g_mix, w_in, g_cq, w_uq, g_ckv, w_ukv, g_mla_q, g_mla_k, w_a2, b_a, g_gla_o,
           g_fox_q, g_fox_k, b_f, w_branch, w_out, g_ffn, w_gu, w_down):
    bsz, s, d = x.shape
    assert bsz == 1 and d == D_MODEL and s % ATT_TILE == 0
    half = MLA_ROPE // 2
    inv = ROPE_THETA ** (-jnp.arange(half, dtype=F32) / half)
    inv128 = jnp.concatenate([inv, inv, jnp.zeros((LANES - MLA_ROPE,), F32)]).reshape(1, LANES)
    tabs = _rope_tables(positions.reshape(s, 1), inv128)
    depth = w_in.shape[0]
    w_in = jnp.swapaxes(w_in, 1, 2)
    w_branch = w_branch.reshape(depth, N_BRANCH * BRANCH_WIDTH, D_MODEL).astype(BF16)
    w_out = w_out.astype(BF16)
    w_down = w_down.astype(BF16)
    xcur = x.reshape(s, d)
    hcur = _rmsnorm(xcur, g_mix[0])
    for l in range(depth):
        g_next = g_mix[l + 1] if l + 1 < depth else None
        xcur, hcur = _layer(xcur, hcur, tabs, g_next, w_in, l, g_cq[l], w_uq[l], g_ckv[l], w_ukv[l], g_mla_q[l],
                            g_mla_k[l], w_a2[l], b_a[l], g_gla_o[l], g_fox_q[l], g_fox_k[l], b_f[l], w_branch,
                            w_out, g_ffn[l], w_gu, w_down)
    return xcur.reshape(bsz, s, d)
```
